```python
import math
import jax, jax.numpy as jnp
from jax import lax
import numpy as np

D_MODEL = 1024
BATCH = 32
SEQ = 256
DEPTH = 4
DEC_BATCH = 2
DEC_SEQ = 2048
PAST_LEN = 256

GRID_W = 64
NORM_EPS = 1e-6
NEG_INF = -1e30

DN_HEADS = 4
DN_DK = 128
DN_DV = 128
DN_WIDTH = DN_HEADS * DN_DV
DN_CONV = 3
DN_CHUNK = 64
NA_HEADS = 8
NA_HD = 64
NA_WIDTH = NA_HEADS * NA_HD
NA_WIN_R = 8
NA_WIN_C = 16
NA_QBLK_C = 16
NA_KBLK_C = 2 * NA_WIN_C
CTX_QBLK = 128
POOL_WINDOWS = (2, 4, 8, 16)
POOL_GROUPS = len(POOL_WINDOWS)
POOL_GC = 128
POOL_WIDTH = POOL_GROUPS * POOL_GC
N_BRANCH = 3

OFF_DN_QKV = 0
OFF_DN_Z = OFF_DN_QKV + 3 * DN_WIDTH
OFF_DN_BETA = OFF_DN_Z + DN_WIDTH
OFF_DN_A = OFF_DN_BETA + 2 * DN_HEADS
OFF_NA_Q = OFF_DN_A + 2 * DN_HEADS
OFF_NA_K = OFF_NA_Q + NA_WIDTH
OFF_NA_V = OFF_NA_K + NA_WIDTH
OFF_NA_Z = OFF_NA_V + NA_WIDTH
OFF_PL_U = OFF_NA_Z + NA_WIDTH
OFF_PL_Z = OFF_PL_U + POOL_WIDTH
OFF_GATE = OFF_PL_Z + POOL_WIDTH
N_IN = OFF_GATE + N_BRANCH * D_MODEL

kernel_name = 'hybrid_dit_deltanet_na_pool_step'

F32 = jnp.float32


def rmsnorm(x, g):
    xf = x.astype(F32)
    y = xf * lax.rsqrt(jnp.mean(xf * xf, axis=-1, keepdims=True) + NORM_EPS)
    return (y * g.astype(F32)).astype(x.dtype)


def l2norm(x):
    return x * lax.rsqrt(jnp.sum(x * x, axis=-1, keepdims=True) + NORM_EPS)


def dwconv_centred(x, w):
    ch = x.shape[-1]
    return lax.conv_general_dilated(x, w[:, None, :].astype(x.dtype), window_strides=(1,),
                                    padding=[(DN_CONV // 2, DN_CONV // 2)],
                                    dimension_numbers=('NWC', 'WIO', 'NWC'), feature_group_count=ch)


def gated_delta_chunked(q, k, v, log_a, beta, s0):
    b, t, h, _ = q.shape
    n = t // DN_CHUNK

    def chunks(z):
        z = z.reshape((b, n, DN_CHUNK, h) + z.shape[3:])
        return jnp.moveaxis(z, (1, 3), (0, 2))

    qc, kc, vc, bc = chunks(q), chunks(k), chunks(v), chunks(beta)
    g = jnp.cumsum(chunks(log_a), axis=-1)
    tri = jnp.tril(jnp.ones((DN_CHUNK, DN_CHUNK), bool))
    strict = jnp.tril(jnp.ones((DN_CHUNK, DN_CHUNK), bool), -1)
    decay = jnp.exp(jnp.where(tri, g[..., :, None] - g[..., None, :], -jnp.inf))
    kb = kc * bc[..., None]
    lmat = jnp.where(strict, jnp.einsum('nbhid,nbhjd->nbhij', kb, kc) * decay, 0.0)
    eye = jnp.eye(DN_CHUNK, dtype=F32)
    tinv = lax.linalg.triangular_solve(eye + lmat, jnp.broadcast_to(eye, lmat.shape),
                                       left_side=True, lower=True)
    u = jnp.einsum('nbhij,nbhjd->nbhid', tinv, vc * bc[..., None])
    w = jnp.einsum('nbhij,nbhjd->nbhid', tinv, kb * jnp.exp(g)[..., None])
    attn = jnp.einsum('nbhid,nbhjd->nbhij', qc, kc) * decay
    qg = qc * jnp.exp(g)[..., None]
    kg = kc * jnp.exp(g[..., -1:] - g)[..., None]
    g_last = jnp.exp(g[..., -1])

    def step(s, inp):
        u_n, w_n, qg_n, kg_n, attn_n, gl_n = inp
        v_new = u_n - jnp.einsum('bhcd,bhde->bhce', w_n, s)
        o = jnp.einsum('bhcd,bhde->bhce', qg_n, s) + jnp.einsum('bhij,bhje->bhie', attn_n, v_new)
        s = s * gl_n[..., None, None] + jnp.einsum('bhcd,bhce->bhde', kg_n, v_new)
        return s, o

    s_fin, o = lax.scan(step, s0, (u, w, qg, kg, attn, g_last))
    o = jnp.moveaxis(o, (0, 2), (1, 3)).reshape(b, t, h, DN_DV)
    return o, s_fin


def delta_mixer(proj, s0, conv_w, a_log, dt_bias, g_norm):
    b, t, _ = proj.shape
    qkv = jax.nn.silu(dwconv_centred(proj[..., OFF_DN_QKV:OFF_DN_Z], conv_w)).astype(F32)
    q, k, v = jnp.split(qkv, 3, axis=-1)
    q = l2norm(q.reshape(b, t, DN_HEADS, DN_DK)) * (DN_DK ** -0.5)
    k = l2norm(k.reshape(b, t, DN_HEADS, DN_DK))
    v = v.reshape(b, t, DN_HEADS, DN_DV)
    beta = jax.nn.sigmoid(proj[..., OFF_DN_BETA:OFF_DN_A].astype(F32)).reshape(b, t, 2, DN_HEADS)
    a_in = proj[..., OFF_DN_A:OFF_NA_Q].astype(F32).reshape(b, t, 2, DN_HEADS)
    log_a = -jnp.exp(a_log.astype(F32)) * jax.nn.softplus(a_in + dt_bias.astype(F32))
    s0 = s0.astype(F32)
    o_f, s_f = gated_delta_chunked(q, k, v, log_a[:, :, 0], beta[:, :, 0], s0[:, 0])
    flip = lambda a: jnp.flip(a, axis=1)
    o_b, s_b = gated_delta_chunked(flip(q), flip(k), flip(v), flip(log_a[:, :, 1]),
                                   flip(beta[:, :, 1]), s0[:, 1])
    o = rmsnorm(o_f + flip(o_b), g_norm).reshape(b, t, DN_WIDTH)
    y = o * jax.nn.silu(proj[..., OFF_DN_Z:OFF_DN_BETA].astype(F32))
    return y.astype(proj.dtype), jnp.stack([s_f, s_b], axis=1)


def context_attention(q, k, v):
    b, l, h, d = q.shape
    q_blocks = jnp.moveaxis(q.reshape(b, l // CTX_QBLK, CTX_QBLK, h, d), 1, 0)

    def block(q_i):
        s = jnp.einsum('bqhd,bkhd->bhqk', q_i, k).astype(F32) * (d ** -0.5)
        p = jax.nn.softmax(s, axis=-1).astype(v.dtype)
        return jnp.einsum('bhqk,bkhd->bqhd', p, v)

    o = lax.map(block, q_blocks)
    return jnp.moveaxis(o, 0, 1).reshape(b, l, h, d)


def neighbourhood_attention(q, k, v, k_ctx, v_ctx, bias_tab):
    b, t, h, d = q.shape
    rows = t // GRID_W
    wr = min(NA_WIN_R, rows)
    ncb = GRID_W // NA_QBLK_C
    scale = d ** -0.5
    col = np.arange(GRID_W)
    q_col = col.reshape(ncb, NA_QBLK_C)
    q_cs = np.clip(col - NA_WIN_C // 2, 0, GRID_W - NA_WIN_C).reshape(ncb, NA_QBLK_C)
    k_col = (np.clip(np.arange(ncb) * NA_QBLK_C - NA_WIN_C // 2, 0, GRID_W - NA_KBLK_C)[:, None]
             + np.arange(NA_KBLK_C))
    col_ok = (k_col[:, None, :] >= q_cs[:, :, None]) & (k_col[:, None, :] < q_cs[:, :, None] + NA_WIN_C)
    dc_idx = np.clip(k_col[:, None, :] - q_col[:, :, None] + NA_WIN_C - 1, 0, 2 * NA_WIN_C - 2)
    col_bias = bias_tab.astype(F32)[:, :, dc_idx]
    mask = col_ok[:, :, None, :]
    q_g = q.reshape(b, rows, ncb, NA_QBLK_C, h, d)
    k_g = k.reshape(b, rows, GRID_W, h, d)[:, :, k_col]
    v_g = v.reshape(b, rows, GRID_W, h, d)[:, :, k_col]
    n_lat = wr * NA_KBLK_C

    def row_block(r):
        rs = jnp.clip(r - wr // 2, 0, rows - wr)
        k_r = lax.dynamic_slice_in_dim(k_g, rs, wr, axis=1)
        v_r = lax.dynamic_slice_in_dim(v_g, rs, wr, axis=1)
        q_r = lax.dynamic_index_in_dim(q_g, r, axis=1, keepdims=False)
        bias_r = jnp.take(col_bias, rs + jnp.arange(wr) - r + NA_WIN_R - 1, axis=1)
        bias_r = jnp.transpose(bias_r, (0, 2, 3, 1, 4))
        s_lat = jnp.einsum('bnqhd,bwnkhd->bhnqwk', q_r, k_r).astype(F32) * scale + bias_r
        s_lat = jnp.where(mask, s_lat, NEG_INF).reshape(b, h, ncb, NA_QBLK_C, n_lat)
        s_ctx = jnp.einsum('bnqhd,blhd->bhnql', q_r, k_ctx).astype(F32) * scale
        p = jax.nn.softmax(jnp.concatenate([s_lat, s_ctx], axis=-1), axis=-1).astype(v.dtype)
        p_lat = p[..., :n_lat].reshape(b, h, ncb, NA_QBLK_C, wr, NA_KBLK_C)
        return (jnp.einsum('bhnqwk,bwnkhd->bnqhd', p_lat, v_r)
                + jnp.einsum('bhnql,blhd->bnqhd', p[..., n_lat:], v_ctx))

    o = lax.map(row_block, jnp.arange(rows))
    return jnp.moveaxis(o, 0, 1).reshape(b, t, h, d)


def pool_mixer(u, pool_w, pool_scale):
    b, t, _ = u.shape
    uf = u.astype(F32).reshape(b, t, POOL_GROUPS, POOL_GC)
    csum = jnp.concatenate([jnp.zeros_like(uf[:, :1]), jnp.cumsum(uf, axis=1)], axis=1)
    pos = jnp.arange(t)
    means = []
    for gi, win in enumerate(POOL_WINDOWS):
        lo = jnp.maximum(pos - win // 2, 0)
        hi = jnp.minimum(pos + win // 2 - 1, t - 1)
        cg = csum[:, :, gi]
        total = jnp.take(cg, hi + 1, axis=1) - jnp.take(cg, lo, axis=1)
        means.append(total / (hi - lo + 1).astype(F32)[None, :, None])
    pooled = jnp.stack(means, axis=2) - uf
    y = jnp.einsum('btgc,gce->btge', pooled, pool_w.astype(F32)).reshape(b, t, POOL_WIDTH)
    return y * pool_scale.astype(F32)


def trunk_layer(x, mod, s0, k_ctx, v_ctx, g_pre, g_post, w_in, conv_w, a_log, dt_bias, g_norm,
                na_bias, pool_w, pool_scale, w_br_dn, w_br_na, w_br_pl, w_out):
    b, t, _ = x.shape
    shift, scale, gate = jnp.split(mod, 3, axis=-1)
    h = rmsnorm(x, g_pre) * (1.0 + scale[:, None]) + shift[:, None]
    proj = jnp.einsum('btd,dn->btn', h, w_in)
    heads = lambda a: a.reshape(b, t, NA_HEADS, NA_HD)
    q_na = heads(proj[..., OFF_NA_Q:OFF_NA_K])
    k_na = heads(proj[..., OFF_NA_K:OFF_NA_V])
    v_na = heads(proj[..., OFF_NA_V:OFF_NA_Z])
    if k_ctx is None:
        s0 = jnp.zeros((b, 2, DN_HEADS, DN_DK, DN_DV), F32)
        o_na = context_attention(q_na, k_na, v_na)
    else:
        o_na = neighbourhood_attention(q_na, k_na, v_na, k_ctx, v_ctx, na_bias)
    y_dn, s_fin = delta_mixer(proj, s0, conv_w, a_log, dt_bias, g_norm)
    y_na = (o_na.reshape(b, t, NA_WIDTH) * jax.nn.silu(proj[..., OFF_NA_Z:OFF_PL_U])).astype(x.dtype)
    y_pl = (pool_mixer(proj[..., OFF_PL_U:OFF_PL_Z], pool_w, pool_scale)
            * jax.nn.silu(proj[..., OFF_PL_Z:OFF_GATE].astype(F32))).astype(x.dtype)
    g_dn, g_na, g_pl = jnp.split(jax.nn.sigmoid(proj[..., OFF_GATE:]), 3, axis=-1)
    merged = (g_dn * jnp.einsum('btw,wd->btd', y_dn, w_br_dn)
              + g_na * jnp.einsum('btw,wd->btd', y_na, w_br_na)
              + g_pl * jnp.einsum('btw,wd->btd', y_pl, w_br_pl))
    out = rmsnorm(jnp.einsum('btd,de->bte', merged, w_out), g_post)
    return x + gate[:, None] * out, k_na, v_na, s_fin


def setup_inputs(seed: int = 0) -> dict:
    key = jax.random.key(seed)
    ks = jax.random.split(key, 24)
    nrm = lambda kk, shape, s: jax.random.normal(kk, shape, F32) * s
    x_prompt = nrm(ks[0], (BATCH, SEQ, D_MODEL), 1.0)
    x_sample = nrm(ks[1], (DEC_BATCH, DEC_SEQ, D_MODEL), 1.0)
    c = nrm(ks[2], (DEC_BATCH, D_MODEL), 1.0)
    cache_k_na = nrm(ks[3], (DEC_BATCH, DEPTH, PAST_LEN, NA_HEADS, NA_HD), 1.0)
    cache_v_na = nrm(ks[4], (DEC_BATCH, DEPTH, PAST_LEN, NA_HEADS, NA_HD), 1.0)
    state_dn = nrm(ks[5], (DEC_BATCH, DEPTH, 2, DN_HEADS, DN_DK, DN_DV), 0.5)
    c_ctx = nrm(ks[6], (D_MODEL,), 1.0)
    w_ada = nrm(ks[7], (DEPTH, D_MODEL, 3 * D_MODEL), 0.5 * D_MODEL ** -0.5)
    b_ada = nrm(ks[8], (DEPTH, 3 * D_MODEL), 0.02)
    g_pre = 1.0 + nrm(ks[9], (DEPTH, D_MODEL), 0.05)
    g_post = 1.0 + nrm(ks[10], (DEPTH, D_MODEL), 0.05)
    w_in = nrm(ks[11], (DEPTH, D_MODEL, N_IN), D_MODEL ** -0.5)
    conv_dn = nrm(ks[12], (DEPTH, DN_CONV, 3 * DN_WIDTH), DN_CONV ** -0.5)
    a_log_dn = jnp.log(jax.random.uniform(ks[13], (DEPTH, 2, DN_HEADS), F32, 1.0, 16.0))
    dt = jnp.exp(jax.random.uniform(ks[14], (DEPTH, 2, DN_HEADS), F32, math.log(1e-3), math.log(1e-1)))
    dt_bias_dn = dt + jnp.log(-jnp.expm1(-dt))
    g_norm_dn = 1.0 + nrm(ks[15], (DEPTH, DN_DV), 0.05)
    na_bias = nrm(ks[16], (DEPTH, NA_HEADS, 2 * NA_WIN_R - 1, 2 * NA_WIN_C - 1), 0.02)
    pool_w = nrm(ks[17], (DEPTH, POOL_GROUPS, POOL_GC, POOL_GC), POOL_GC ** -0.5)
    pool_scale = 1.0 + nrm(ks[18], (DEPTH, POOL_WIDTH), 0.05)
    w_br_dn = nrm(ks[19], (DEPTH, DN_WIDTH, D_MODEL), DN_WIDTH ** -0.5)
    w_br_na = nrm(ks[20], (DEPTH, NA_WIDTH, D_MODEL), NA_WIDTH ** -0.5)
    w_br_pl = nrm(ks[21], (DEPTH, POOL_WIDTH, D_MODEL), POOL_WIDTH ** -0.5)
    w_out = nrm(ks[22], (DEPTH, D_MODEL, D_MODEL), D_MODEL ** -0.5)
    return {'x_prompt': x_prompt, 'x_sample': x_sample, 'c': c,
            'cache_k_na': cache_k_na, 'cache_v_na': cache_v_na, 'state_dn': state_dn,
            'c_ctx': c_ctx, 'w_ada': w_ada, 'b_ada': b_ada, 'g_pre': g_pre, 'g_post': g_post,
            'w_in': w_in, 'conv_dn': conv_dn, 'a_log_dn': a_log_dn, 'dt_bias_dn': dt_bias_dn,
            'g_norm_dn': g_norm_dn, 'na_bias': na_bias, 'pool_w': pool_w, 'pool_scale': pool_scale,
            'w_br_dn': w_br_dn, 'w_br_na': w_br_na, 'w_br_pl': w_br_pl, 'w_out': w_out}


def reference(x_prompt, x_sample, c, cache_k_na, cache_v_na, state_dn, c_ctx, w_ada, b_ada, g_pre,
              g_post, w_in, conv_dn, a_log_dn, dt_bias_dn, g_norm_dn, na_bias, pool_w, pool_scale,
              w_br_dn, w_br_na, w_br_pl, w_out):
    xp = x_prompt
    xs = x_sample
    k_list, v_list, s_list = [], [], []
    for l in range(DEPTH):
        lw = (g_pre[l], g_post[l], w_in[l], conv_dn[l], a_log_dn[l], dt_bias_dn[l], g_norm_dn[l],
              na_bias[l], pool_w[l], pool_scale[l], w_br_dn[l], w_br_na[l], w_br_pl[l], w_out[l])
        mod_ctx = (jnp.einsum('d,dn->n', jax.nn.silu(c_ctx), w_ada[l]) + b_ada[l])[None]
        mod_lat = jnp.einsum('bd,dn->bn', jax.nn.silu(c), w_ada[l]) + b_ada[l]
        xp, k_l, v_l, s_l = trunk_layer(xp, mod_ctx, None, None, None, *lw)
        k_list.append(k_l)
        v_list.append(v_l)
        s_list.append(s_l)
        xs, _, _, _ = trunk_layer(xs, mod_lat, state_dn[:, l], cache_k_na[:, l], cache_v_na[:, l], *lw)
    new_k_na = jnp.stack(k_list, axis=1)
    new_v_na = jnp.stack(v_list, axis=1)
    new_state_dn = jnp.stack(s_list, axis=1)
    return (xp, xs, new_k_na, new_v_na, new_state_dn)
```

```python
import functools

import numpy as np
import jax
import jax.numpy as jnp
from jax import lax
from jax.experimental import pallas as pl
from jax.experimental.pallas import tpu as pltpu

F32 = jnp.float32
BF16 = jnp.bfloat16
HIGHEST = lax.Precision.HIGHEST

D_MODEL = 1024
DEPTH = 4
GRID_W = 64
NORM_EPS = 1e-6
NEG_INF = -1e30

DN_HEADS = 4
DN_DK = 128
DN_WIDTH = 512
DN_CHUNK = 64
DN_BLK_SHIFT = 4

NA_HEADS = 8
NA_HD = 64
NA_WIDTH = 512
NA_WIN_R = 8
NA_WIN_C = 16

POOL_WINDOWS = (2, 4, 8, 16)
POOL_GC = 128
POOL_WIDTH = 512
POOL_PAD = 8

LANE = 128
P_MAIN = 8192
CB_DN_Q, CB_DN_K, CB_DN_V, CB_DN_Z = 0, 4, 8, 12
CB_NA_Q, CB_NA_K, CB_NA_V, CB_NA_Z = 16, 20, 24, 28
OFF_BA_SRC = 2048
N_BA = 16

VMEM_LIMIT = 56 * 1024 * 1024


def _sigmoid(x):
    return 1.0 / (1.0 + jnp.exp(-x))


def _silu(x):
    return x * _sigmoid(x)


def _softplus(x):
    return jnp.maximum(x, 0.0) + jnp.log1p(jnp.exp(-jnp.abs(x)))


def _dot(a, b):
    return jnp.dot(a, b, preferred_element_type=F32)


def _dot_nt(a, b):
    return lax.dot_general(a, b, (((1,), (1,)), ((), ())), preferred_element_type=F32)


def _dot_tn(a, b):
    return lax.dot_general(a, b, (((0,), (0,)), ((), ())), preferred_element_type=F32)


def _dot_hi(a, b):
    return jnp.dot(a, b, precision=HIGHEST, preferred_element_type=F32)


def _params(sem):
    return pltpu.CompilerParams(dimension_semantics=sem, vmem_limit_bytes=VMEM_LIMIT)


def _mod_kernel(cc_ref, w_ref, b_ref, o_ref):
    a = _silu(cc_ref[...]).astype(BF16)
    o_ref[0] = _dot(a, w_ref[0].astype(BF16)) + b_ref[0]


def _modulation(cc, w_ada, b_ada):
    tn = 1024
    return pl.pallas_call(
        _mod_kernel,
        grid=(DEPTH, 3 * D_MODEL // tn),
        in_specs=[
            pl.BlockSpec((8, D_MODEL), lambda l, n: (0, 0)),
            pl.BlockSpec((1, D_MODEL, tn), lambda l, n: (l, 0, n)),
            pl.BlockSpec((1, 1, tn), lambda l, n: (l, 0, n)),
        ],
        out_specs=pl.BlockSpec((1, 8, tn), lambda l, n: (l, 0, n)),
        out_shape=jax.ShapeDtypeStruct((DEPTH, 8, 3 * D_MODEL), F32),
        compiler_params=_params(("parallel", "parallel")),
        name="adaln_mod",
    )(cc, w_ada, b_ada.reshape(DEPTH, 1, 3 * D_MODEL))


def _mod_row(mod_ref, tiles_per_batch, mod_row0):
    if tiles_per_batch is None:
        return mod_ref[0:1, :]
    row = mod_row0 + pl.program_id(0) // tiles_per_batch
    return mod_ref[pl.ds(row, 1), :]


def _proj_kernel(x_ref, mod_ref, g_ref, w_ref, wba_ref, o_ref, ba_ref, h_ref, *, tiles_per_batch, mod_row0):
    @pl.when(pl.program_id(1) == 0)
    def _():
        mod = _mod_row(mod_ref, tiles_per_batch, mod_row0)
        shift = mod[:, 0:D_MODEL]
        scale = mod[:, D_MODEL:2 * D_MODEL]
        x = x_ref[...]
        y = x * lax.rsqrt(jnp.mean(x * x, axis=-1, keepdims=True) + NORM_EPS) * g_ref[...]
        hb = (y * (1.0 + scale) + shift).astype(BF16)
        h_ref[...] = hb
        ba_ref[...] = _dot(hb, wba_ref[...])

    o_ref[...] = _dot(h_ref[...], w_ref[...])


def _project(x2d, mod_l, g_pre, w_main, w_ba, tiles_per_batch, mod_row0, tm=512, tn=1024):
    ntok = x2d.shape[0]
    kern = functools.partial(_proj_kernel, tiles_per_batch=tiles_per_batch, mod_row0=mod_row0)
    return pl.pallas_call(
        kern,
        grid=(ntok // tm, P_MAIN // tn),
        in_specs=[
            pl.BlockSpec((tm, D_MODEL), lambda m, n: (m, 0)),
            pl.BlockSpec((8, 3 * D_MODEL), lambda m, n: (0, 0)),
            pl.BlockSpec((1, D_MODEL), lambda m, n: (0, 0)),
            pl.BlockSpec((D_MODEL, tn), lambda m, n: (0, n)),
            pl.BlockSpec((D_MODEL, LANE), lambda m, n: (0, 0)),
        ],
        out_specs=[
            pl.BlockSpec((tm, tn), lambda m, n: (m, n)),
            pl.BlockSpec((tm, LANE), lambda m, n: (m, 0)),
        ],
        out_shape=[
            jax.ShapeDtypeStruct((ntok, P_MAIN), F32),
            jax.ShapeDtypeStruct((ntok, LANE), F32),
        ],
        scratch_shapes=[pltpu.VMEM((tm, D_MODEL), BF16)],
        compiler_params=_params(("parallel", "arbitrary")),
        name="in_proj",
    )(x2d, mod_l, g_pre, w_main, w_ba)


def _delta_kernel(*refs, T, has_s0, want_state):
    it = iter(refs)
    q_ref, k_ref, v_ref, z_ref, ba_ref = (next(it) for _ in range(5))
    cwq_ref, cwk_ref, cwv_ref, alog_ref, dtb_ref, gn_ref = (next(it) for _ in range(6))
    s0_ref = next(it) if has_s0 else None
    y_ref = next(it)
    sfin_ref = next(it) if want_state else None
    pad_ref, qn_ref, kn_ref, vn_ref, bb_ref, la_ref, of_ref, ob_ref = (next(it) for _ in range(8))

    head = pl.program_id(1)
    C = DN_CHUNK
    NC = T // C
    RB = min(T, 256)

    zeros8 = jnp.zeros((8, LANE), F32)
    pad_ref[0:8, :] = zeros8
    pad_ref[T + 8:T + 16, :] = zeros8

    def conv_into(x_ref, cw_ref, dst_ref, mode):
        for r0 in range(0, T, RB):
            pad_ref[8 + r0:8 + r0 + RB, :] = x_ref[0, r0:r0 + RB, :]
        w = cw_ref[...]
        for r0 in range(0, T, RB):
            y = (pad_ref[7 + r0:7 + r0 + RB, :] * w[0:1, :]
                 + pad_ref[8 + r0:8 + r0 + RB, :] * w[1:2, :]
                 + pad_ref[9 + r0:9 + r0 + RB, :] * w[2:3, :])
            y = _silu(y)
            if mode != "v":
                y = y * lax.rsqrt(jnp.sum(y * y, axis=-1, keepdims=True) + NORM_EPS)
            if mode == "q":
                y = y * (DN_DK ** -0.5)
            dst_ref[r0:r0 + RB, :] = y

    conv_into(q_ref, cwq_ref, qn_ref, "q")
    conv_into(k_ref, cwk_ref, kn_ref, "k")
    conv_into(v_ref, cwv_ref, vn_ref, "v")

    lane = lax.broadcasted_iota(jnp.int32, (RB, LANE), 1)
    neg_a = -jnp.exp(alog_ref[...])
    for r0 in range(0, T, RB):
        ba = ba_ref[0, r0:r0 + RB, :]
        beta_all = _sigmoid(ba)
        la_all = neg_a * _softplus(ba + dtb_ref[...])
        for d in range(2):
            bcol = jnp.sum(jnp.where(lane == d * DN_HEADS + head, beta_all, 0.0), axis=1, keepdims=True)
            lcol = jnp.sum(jnp.where(lane == 2 * DN_HEADS + d * DN_HEADS + head, la_all, 0.0), axis=1,
                           keepdims=True)
            bb_ref[d, r0:r0 + RB, :] = jnp.broadcast_to(bcol, (RB, LANE))
            la_ref[d, r0:r0 + RB, :] = jnp.broadcast_to(lcol, (RB, LANE))

    ii = lax.broadcasted_iota(jnp.int32, (C, C), 0)
    jj = lax.broadcasted_iota(jnp.int32, (C, C), 1)
    lower_incl = ii >= jj
    upper_incl = ii <= jj
    lower_f = lower_incl.astype(F32)
    upper_f = upper_incl.astype(F32)
    eye_f = (ii == jj).astype(F32)
    ones_f = jnp.ones((C, C), F32)
    same_blk = jnp.right_shift(ii, DN_BLK_SHIFT) == jnp.right_shift(jj, DN_BLK_SHIFT)

    def tri_inv(lm):
        md = jnp.where(same_blk, -lm, 0.0)
        md2 = _dot_hi(md, md)
        md4 = _dot_hi(md2, md2)
        md8 = _dot_hi(md4, md4)
        p = eye_f + md + md2 + _dot_hi(md, md2)
        q = eye_f + md4 + md8 + _dot_hi(md4, md8)
        dinv = _dot_hi(p, q)
        n1 = _dot_hi(dinv, jnp.where(same_blk, 0.0, lm))
        n2 = _dot_hi(n1, n1)
        r = eye_f - n1 + n2 - _dot_hi(n1, n2)
        return _dot_hi(r, dinv)

    def chunk_local(d, c):
        rows = pl.ds(pl.multiple_of(c * C, C), C)
        kc = kn_ref[rows, :]
        qc = qn_ref[rows, :]
        vc = vn_ref[rows, :]
        bb = bb_ref[d, rows, :]
        lab = la_ref[d, rows, :]
        if d == 0:
            tri_c, tri_r, incl, strict = lower_f, upper_f, lower_incl, ii > jj
        else:
            tri_c, tri_r, incl, strict = upper_f, lower_f, upper_incl, ii < jj
        gc = _dot_hi(tri_c, lab)
        gr = _dot_hi(ones_f, tri_r * lab[:, 0:C])
        decay = jnp.exp(jnp.where(incl, gc[:, 0:C] - gr, NEG_INF))
        kb = kc * bb
        kcb = kc.astype(BF16)
        lm = jnp.where(strict, _dot_nt(kb.astype(BF16), kcb) * decay, 0.0)
        attn = _dot_nt(qc.astype(BF16), kcb) * decay
        tinv = tri_inv(lm)
        eg = jnp.exp(gc)
        uw = _dot_hi(tinv, jnp.concatenate([vc * bb, kb * eg], axis=1))
        glast = gc[C - 1:C, :] if d == 0 else gc[0:1, :]
        kg = kc * jnp.exp(glast - gc)
        return rows, uw[:, 0:LANE], uw[:, LANE:2 * LANE], qc * eg, kg, attn, jnp.exp(glast)

    def seq_step(s, loc, o_ref):
        rows, u, w, qg, kg, attn, gl = loc
        sb = s.astype(BF16)
        v_new = u - _dot(w.astype(BF16), sb)
        vb = v_new.astype(BF16)
        o_ref[rows, :] = _dot(qg.astype(BF16), sb) + _dot(attn.astype(BF16), vb)
        return s * gl + _dot_tn(kg.astype(BF16), vb)

    def body(i, carry):
        s_f, s_b = carry
        s_f = seq_step(s_f, chunk_local(0, i), of_ref)
        s_b = seq_step(s_b, chunk_local(1, NC - 1 - i), ob_ref)
        return s_f, s_b

    if has_s0:
        init = (s0_ref[0, 0, 0], s0_ref[0, 1, 0])
    else:
        init = (jnp.zeros((DN_DK, LANE), F32), jnp.zeros((DN_DK, LANE), F32))
    s_f, s_b = lax.fori_loop(0, NC, body, init)
    if want_state:
        sfin_ref[0, 0, 0] = s_f
        sfin_ref[0, 1, 0] = s_b

    for r0 in range(0, T, RB):
        o = of_ref[r0:r0 + RB, :] + ob_ref[r0:r0 + RB, :]
        o = o * lax.rsqrt(jnp.mean(o * o, axis=-1, keepdims=True) + NORM_EPS) * gn_ref[...]
        y_ref[0, r0:r0 + RB, :] = o * _silu(z_ref[0, r0:r0 + RB, :])


def _delta_mixer(proj3, ba3, conv_w, alog_v, dtb_v, g_norm, s0, want_state):
    B, T, _ = proj3.shape
    has_s0 = s0 is not None
    kern = functools.partial(_delta_kernel, T=T, has_s0=has_s0, want_state=want_state)
    col = lambda cb: pl.BlockSpec((1, T, LANE), lambda b, h, cb=cb: (b, 0, cb + h))
    cw = lambda cb: pl.BlockSpec((3, LANE), lambda b, h, cb=cb: (0, cb + h))
    vec = pl.BlockSpec((1, LANE), lambda b, h: (0, 0))
    state_spec = pl.BlockSpec((1, 2, 1, DN_DK, LANE), lambda b, h: (b, 0, h, 0, 0))
    in_specs = [col(CB_DN_Q), col(CB_DN_K), col(CB_DN_V), col(CB_DN_Z),
                pl.BlockSpec((1, T, LANE), lambda b, h: (b, 0, 0)),
                cw(0), cw(4), cw(8), vec, vec, vec]
    args = [proj3, proj3, proj3, proj3, ba3, conv_w, conv_w, conv_w, alog_v, dtb_v, g_norm]
    if has_s0:
        in_specs.append(state_spec)
        args.append(s0)
    out_specs = [pl.BlockSpec((1, T, LANE), lambda b, h: (b, 0, h))]
    out_shape = [jax.ShapeDtypeStruct((B, T, DN_WIDTH), F32)]
    if want_state:
        out_specs.append(state_spec)
        out_shape.append(jax.ShapeDtypeStruct((B, 2, DN_HEADS, DN_DK, LANE), F32))
    res = pl.pallas_call(
        kern,
        grid=(B, DN_HEADS),
        in_specs=in_specs,
        out_specs=out_specs,
        out_shape=out_shape,
        scratch_shapes=[
            pltpu.VMEM((T + 16, LANE), F32),
            pltpu.VMEM((T, LANE), F32), pltpu.VMEM((T, LANE), F32), pltpu.VMEM((T, LANE), F32),
            pltpu.VMEM((2, T, LANE), F32), pltpu.VMEM((2, T, LANE), F32),
            pltpu.VMEM((T, LANE), F32), pltpu.VMEM((T, LANE), F32),
        ],
        compiler_params=_params(("parallel", "parallel")),
        name="delta_mixer",
    )(*args)
    return (res[0], res[1]) if want_state else (res[0], None)


def _head_masks(rows):
    lane = lax.broadcasted_iota(jnp.int32, (rows, LANE), 1)
    return (lane < NA_HD).astype(F32), (lane >= NA_HD).astype(F32)


def _ctx_attn_kernel(q_ref, k_ref, v_ref, z_ref, y_ref, *, T):
    m0, m1 = _head_masks(T)
    scale = NA_HD ** -0.5
    for p in range(NA_WIDTH // LANE):
        cols = slice(p * LANE, (p + 1) * LANE)
        q = q_ref[0, :, cols]
        kb = k_ref[0, :, cols].astype(BF16)
        vb = v_ref[0, :, cols].astype(BF16)
        out = jnp.zeros((T, LANE), F32)
        for hm in (m0, m1):
            s = _dot_nt((q * hm).astype(BF16), kb) * scale
            e = jnp.exp(s - jnp.max(s, axis=-1, keepdims=True))
            o = _dot(e.astype(BF16), vb) / jnp.sum(e, axis=-1, keepdims=True)
            out = out + o * hm
        y_ref[0, :, cols] = out * _silu(z_ref[0, :, cols])


def _ctx_attention(proj3):
    B, T, _ = proj3.shape
    wide = NA_WIDTH // LANE
    spec = lambda cb: pl.BlockSpec((1, T, NA_WIDTH), lambda b, cb=cb: (b, 0, cb // wide))
    return pl.pallas_call(
        functools.partial(_ctx_attn_kernel, T=T),
        grid=(B,),
        in_specs=[spec(CB_NA_Q), spec(CB_NA_K), spec(CB_NA_V), spec(CB_NA_Z)],
        out_specs=pl.BlockSpec((1, T, NA_WIDTH), lambda b: (b, 0, 0)),
        out_shape=jax.ShapeDtypeStruct((B, T, NA_WIDTH), F32),
        compiler_params=_params(("parallel",)),
        name="ctx_attention",
    )(proj3, proj3, proj3, proj3)


def _na_kernel(q_ref, k_ref, v_ref, z_ref, kc_ref, vc_ref, bias_ref, y_ref, sctx_ref, *, T, L):
    W = GRID_W
    rows_total = T // W
    n_win = NA_WIN_R * W
    scale = NA_HD ** -0.5
    m0, m1 = _head_masks(W)
    kcb = kc_ref[0].astype(BF16)
    vcb = vc_ref[0].astype(BF16)

    RB = 256
    mb0, mb1 = _head_masks(RB)
    for r0 in range(0, T, RB):
        q = q_ref[0, r0:r0 + RB, :]
        sctx_ref[0, r0:r0 + RB, :] = _dot_nt((q * mb0).astype(BF16), kcb) * scale
        sctx_ref[1, r0:r0 + RB, :] = _dot_nt((q * mb1).astype(BF16), kcb) * scale

    qc = lax.broadcasted_iota(jnp.int32, (2 * W, n_win), 0) & (W - 1)
    kcol = lax.broadcasted_iota(jnp.int32, (2 * W, n_win), 1) & (W - 1)
    cs = jnp.clip(qc - NA_WIN_C // 2, 0, W - NA_WIN_C)
    col_ok = (kcol >= cs) & (kcol < cs + NA_WIN_C)

    def row_body(r, carry):
        rs = jnp.clip(r - NA_WIN_R // 2, 0, rows_total - NA_WIN_R)
        qrows = pl.ds(pl.multiple_of(r * W, W), W)
        krows = pl.ds(pl.multiple_of(rs * W, W), n_win)
        q = q_ref[0, qrows, :]
        qs = jnp.concatenate([q * m0, q * m1], axis=0).astype(BF16)
        kw = k_ref[0, krows, :].astype(BF16)
        vw = v_ref[0, krows, :].astype(BF16)
        dr0 = rs - r + NA_WIN_R - 1
        bias = jnp.concatenate(
            [jnp.concatenate([bias_ref[hh, dr0 + 2 * m] for m in range(NA_WIN_R // 2)], axis=1)
             for hh in range(2)], axis=0)
        s = jnp.where(col_ok, _dot_nt(qs, kw) * scale + bias, NEG_INF)
        sc = jnp.concatenate([sctx_ref[0, qrows, :], sctx_ref[1, qrows, :]], axis=0)
        mx = jnp.maximum(jnp.max(s, axis=-1, keepdims=True), jnp.max(sc, axis=-1, keepdims=True))
        e1 = jnp.exp(s - mx)
        e2 = jnp.exp(sc - mx)
        den = jnp.sum(e1, axis=-1, keepdims=True) + jnp.sum(e2, axis=-1, keepdims=True)
        o = (_dot(e1.astype(BF16), vw) + _dot(e2.astype(BF16), vcb)) / den
        out = o[0:W, :] * m0 + o[W:2 * W, :] * m1
        y_ref[0, qrows, :] = out * _silu(z_ref[0, qrows, :])
        return carry

    lax.fori_loop(0, rows_total, row_body, 0)


def _neighbourhood_attention(proj3, k_ctx, v_ctx, bias2):
    B, T, _ = proj3.shape
    L = k_ctx.shape[1]
    col = lambda cb: pl.BlockSpec((1, T, LANE), lambda b, p, cb=cb: (b, 0, cb + p))
    ctx = pl.BlockSpec((1, L, LANE), lambda b, p: (b, 0, p))
    return pl.pallas_call(
        functools.partial(_na_kernel, T=T, L=L),
        grid=(B, NA_WIDTH // LANE),
        in_specs=[col(CB_NA_Q), col(CB_NA_K), col(CB_NA_V), col(CB_NA_Z), ctx, ctx,
                  pl.BlockSpec((2, 2 * NA_WIN_R - 2, GRID_W, LANE), lambda b, p: (p, 0, 0, 0))],
        out_specs=pl.BlockSpec((1, T, LANE), lambda b, p: (b, 0, p)),
        out_shape=jax.ShapeDtypeStruct((B, T, NA_WIDTH), F32),
        scratch_shapes=[pltpu.VMEM((2, T, L), F32)],
        compiler_params=_params(("parallel", "parallel")),
        name="nbr_attention",
    )(proj3, proj3, proj3, proj3, k_ctx, v_ctx, bias2)


def _pool_kernel(u_ref, z_ref, pw_ref, ps_ref, y_ref, x_buf, s2_buf, s4_buf, s8_buf, s16_buf, *, T):
    P = POOL_PAD
    E = T + 2 * P
    RB = 256
    bufs = (x_buf, s2_buf, s4_buf, s8_buf, s16_buf)
    for b in bufs:
        zeros = jnp.zeros((P, b.shape[1]), F32)
        b[0:P, :] = zeros
        b[E + P:E + 2 * P, :] = zeros
    zeros = jnp.zeros((P, POOL_WIDTH), F32)
    x_buf[P:2 * P, :] = zeros
    x_buf[T + 2 * P:T + 3 * P, :] = zeros
    for r0 in range(0, T, RB):
        x_buf[2 * P + r0:2 * P + r0 + RB, :] = u_ref[0, r0:r0 + RB, :]

    def level(src, dst, lo, hi, lane0):
        for e0 in range(0, E, RB):
            n = min(RB, E - e0)
            dst[P + e0:P + e0 + n, :] = (src[P + e0 + lo:P + e0 + lo + n, lane0:]
                                         + src[P + e0 + hi:P + e0 + hi + n, lane0:])

    level(x_buf, s2_buf, -1, 0, 0)
    level(s2_buf, s4_buf, -1, 1, LANE)
    level(s4_buf, s8_buf, -2, 2, LANE)
    level(s8_buf, s16_buf, -4, 4, LANE)

    sums = (s2_buf, s4_buf, s8_buf, s16_buf)
    for r0 in range(0, T, RB):
        t = r0 + lax.broadcasted_iota(jnp.int32, (RB, 1), 0)
        for g, win in enumerate(POOL_WINDOWS):
            cols = slice(g * POOL_GC, (g + 1) * POOL_GC)
            lo = jnp.maximum(t - win // 2, 0)
            hi = jnp.minimum(t + win // 2 - 1, T - 1)
            cnt = (hi - lo + 1).astype(F32)
            tot = sums[g][2 * P + r0:2 * P + r0 + RB, 0:POOL_GC]
            pooled = tot / cnt - x_buf[2 * P + r0:2 * P + r0 + RB, cols]
            y = _dot(pooled.astype(BF16), pw_ref[g]) * ps_ref[:, cols]
            y_ref[0, r0:r0 + RB, cols] = y * _silu(z_ref[0, r0:r0 + RB, cols])


def _pool_mixer(proj3, pool_w, pool_scale):
    B, T, _ = proj3.shape
    rows = T + 4 * POOL_PAD
    return pl.pallas_call(
        functools.partial(_pool_kernel, T=T),
        grid=(B,),
        in_specs=[
            pl.BlockSpec((1, T, POOL_WIDTH), lambda b: (b, 0, 8)),
            pl.BlockSpec((1, T, POOL_WIDTH), lambda b: (b, 0, 9)),
            pl.BlockSpec((len(POOL_WINDOWS), POOL_GC, POOL_GC), lambda b: (0, 0, 0)),
            pl.BlockSpec((1, POOL_WIDTH), lambda b: (0, 0)),
        ],
        out_specs=pl.BlockSpec((1, T, POOL_WIDTH), lambda b: (b, 0, 0)),
        out_shape=jax.ShapeDtypeStruct((B, T, POOL_WIDTH), F32),
        scratch_shapes=[
            pltpu.VMEM((rows, 4 * LANE), F32), pltpu.VMEM((rows, 4 * LANE), F32),
            pltpu.VMEM((rows, 3 * LANE), F32), pltpu.VMEM((rows, 2 * LANE), F32),
            pltpu.VMEM((rows, LANE), F32),
        ],
        compiler_params=_params(("parallel",)),
        name="pool_mixer",
    )(proj3, proj3, pool_w, pool_scale)


def _merge_kernel(ydn_ref, yna_ref, ypl_ref, gdn_ref, gna_ref, gpl_ref, x_ref, mod_ref, gpost_ref,
                  wd_ref, wn_ref, wp_ref, wo_ref, o_ref, *, tiles_per_batch, mod_row0):
    merged = (_sigmoid(gdn_ref[...]) * _dot(ydn_ref[...].astype(BF16), wd_ref[...])
              + _sigmoid(gna_ref[...]) * _dot(yna_ref[...].astype(BF16), wn_ref[...])
              + _sigmoid(gpl_ref[...]) * _dot(ypl_ref[...].astype(BF16), wp_ref[...]))
    out = _dot(merged.astype(BF16), wo_ref[...])
    out = out * lax.rsqrt(jnp.mean(out * out, axis=-1, keepdims=True) + NORM_EPS) * gpost_ref[...]
    gate = _mod_row(mod_ref, tiles_per_batch, mod_row0)[:, 2 * D_MODEL:3 * D_MODEL]
    o_ref[...] = x_ref[...] + gate * out


def _merge(y_dn, y_na, y_pl, proj2d, x2d, mod_l, g_post, w_dn, w_na, w_pl, w_out, tiles_per_batch, mod_row0,
           tm=512):
    ntok = x2d.shape[0]
    br = pl.BlockSpec((tm, DN_WIDTH), lambda m: (m, 0))
    gate = lambda j: pl.BlockSpec((tm, D_MODEL), lambda m, j=j: (m, j))
    wbr = pl.BlockSpec((DN_WIDTH, D_MODEL), lambda m: (0, 0))
    return pl.pallas_call(
        functools.partial(_merge_kernel, tiles_per_batch=tiles_per_batch, mod_row0=mod_row0),
        grid=(ntok // tm,),
        in_specs=[br, br, br, gate(5), gate(6), gate(7),
                  pl.BlockSpec((tm, D_MODEL), lambda m: (m, 0)),
                  pl.BlockSpec((8, 3 * D_MODEL), lambda m: (0, 0)),
                  pl.BlockSpec((1, D_MODEL), lambda m: (0, 0)),
                  wbr, wbr, wbr,
                  pl.BlockSpec((D_MODEL, D_MODEL), lambda m: (0, 0))],
        out_specs=pl.BlockSpec((tm, D_MODEL), lambda m: (m, 0)),
        out_shape=jax.ShapeDtypeStruct((ntok, D_MODEL), F32),
        compiler_params=_params(("parallel",)),
        name="merge_out",
    )(y_dn, y_na, y_pl, proj2d, proj2d, proj2d, x2d, mod_l, g_post, w_dn, w_na, w_pl, w_out)


def _layer(x3, mod_l, tiles_per_batch, mod_row0, s0, k_ctx, v_ctx, lw, tm=512):
    B, T, _ = x3.shape
    x2d = x3.reshape(B * T, D_MODEL)
    proj2d, ba2d = _project(x2d, mod_l, lw["g_pre"], lw["w_main"], lw["w_ba"], tiles_per_batch, mod_row0, tm=tm)
    proj3 = proj2d.reshape(B, T, P_MAIN)
    ba3 = ba2d.reshape(B, T, LANE)
    is_ctx = k_ctx is None
    y_dn, s_fin = _delta_mixer(proj3, ba3, lw["conv_w"], lw["alog_v"], lw["dtb_v"], lw["g_norm"], s0,
                               want_state=is_ctx)
    if is_ctx:
        y_na = _ctx_attention(proj3)
    else:
        y_na = _neighbourhood_attention(proj3, k_ctx, v_ctx, lw["bias2"])
    y_pl = _pool_mixer(proj3, lw["pool_w"], lw["pool_scale"])
    n = B * T
    out2d = _merge(y_dn.reshape(n, DN_WIDTH), y_na.reshape(n, NA_WIDTH), y_pl.reshape(n, POOL_WIDTH), proj2d,
                   x2d, mod_l, lw["g_post"], lw["w_dn"], lw["w_na"], lw["w_pl"], lw["w_out"],
                   tiles_per_batch, mod_row0, tm=tm)
    return out2d.reshape(B, T, D_MODEL), proj3, s_fin


def _lane_vec(v, offset):
    return jnp.zeros((1, LANE), F32).at[0, offset:offset + 2 * DN_HEADS].set(v.reshape(-1))


def kernel(x_prompt, x_sample, c, cache_k_na, cache_v_na, state_dn, c_ctx, w_ada, b_ada, g_pre, g_post, w_in,
           conv_dn, a_log_dn, dt_bias_dn, g_norm_dn, na_bias, pool_w, pool_scale, w_br_dn, w_br_na, w_br_pl,
           w_out):
    B, T, _ = x_prompt.shape
    DB, DT, _ = x_sample.shape
    L = cache_k_na.shape[2]
    tm = 512

    cc = jnp.zeros((8, D_MODEL), F32).at[0].set(c_ctx).at[1:1 + DB].set(c)
    mod = _modulation(cc, w_ada, b_ada)

    w_main = jnp.concatenate([w_in[:, :, :OFF_BA_SRC], w_in[:, :, OFF_BA_SRC + N_BA:]], axis=-1).astype(BF16)
    w_ba = jnp.pad(w_in[:, :, OFF_BA_SRC:OFF_BA_SRC + N_BA], ((0, 0), (0, 0), (0, LANE - N_BA))).astype(BF16)
    dcol = np.clip(np.arange(GRID_W)[None, :] - np.arange(GRID_W)[:, None] + NA_WIN_C - 1, 0, 2 * NA_WIN_C - 2)
    bias_t = na_bias[:, :, :, dcol]
    bias2 = jnp.concatenate([bias_t[:, :, :-1], bias_t[:, :, 1:]], axis=-1)

    xp, xs = x_prompt, x_sample
    k_list, v_list, s_list = [], [], []
    for l in range(DEPTH):
        lw = dict(
            g_pre=g_pre[l][None], g_post=g_post[l][None], w_main=w_main[l], w_ba=w_ba[l], conv_w=conv_dn[l],
            alog_v=_lane_vec(a_log_dn[l], 2 * DN_HEADS), dtb_v=_lane_vec(dt_bias_dn[l], 2 * DN_HEADS),
            g_norm=g_norm_dn[l][None], bias2=bias2[l], pool_w=pool_w[l].astype(BF16),
            pool_scale=pool_scale[l][None], w_dn=w_br_dn[l].astype(BF16), w_na=w_br_na[l].astype(BF16),
            w_pl=w_br_pl[l].astype(BF16), w_out=w_out[l].astype(BF16))
        xp, proj_c, s_fin = _layer(xp, mod[l], None, 0, None, None, None, lw, tm=tm)
        k_list.append(proj_c[:, :, CB_NA_K * LANE:CB_NA_K * LANE + NA_WIDTH].reshape(B, T, NA_HEADS, NA_HD))
        v_list.append(proj_c[:, :, CB_NA_V * LANE:CB_NA_V * LANE + NA_WIDTH].reshape(B, T, NA_HEADS, NA_HD))
        s_list.append(s_fin)
        xs, _, _ = _layer(xs, mod[l], DT // tm, 1, state_dn[:, l],
                          cache_k_na[:, l].reshape(DB, L, NA_WIDTH), cache_v_na[:, l].reshape(DB, L, NA_WIDTH),
                          lw, tm=tm)
    return (xp, xs, jnp.stack(k_list, axis=1), jnp.stack(v_list, axis=1), jnp.stack(s_list, axis=1))
```

```python
import functools

import numpy as np
import jax
import jax.numpy as jnp
from jax import lax
from jax.experimental import pallas as pl
from jax.experimental.pallas import tpu as pltpu

F32 = jnp.float32
BF16 = jnp.bfloat16
HIGHEST = lax.Precision.HIGHEST

D_MODEL = 1024
DEPTH = 4
GRID_W = 64
NORM_EPS = 1e-6
NEG_INF = -1e30

DN_HEADS = 4
DN_DK = 128
DN_WIDTH = 512
DN_CHUNK = 64
DN_BLK_SHIFT = 4
DN_GROUP = 4

NA_HEADS = 8
NA_HD = 64
NA_WIDTH = 512
NA_WIN_R = 8
NA_WIN_C = 16

POOL_WINDOWS = (2, 4, 8, 16)
POOL_GC = 128
POOL_WIDTH = 512
POOL_PAD = 8

LANE = 128
P_MAIN = 8192
CB_DN_Q, CB_DN_K, CB_DN_V, CB_DN_Z = 0, 4, 8, 12
CB_NA_Q, CB_NA_K, CB_NA_V, CB_NA_Z = 16, 20, 24, 28
OFF_BA_SRC = 2048
N_BA = 16

VMEM_LIMIT = 56 * 1024 * 1024


def _sigmoid(x):
    return 1.0 / (1.0 + jnp.exp(-x))


def _silu(x):
    return x * _sigmoid(x)


def _softplus(x):
    return jnp.maximum(x, 0.0) + jnp.log1p(jnp.exp(-jnp.abs(x)))


def _dot(a, b):
    return jnp.dot(a, b, preferred_element_type=F32)


def _dot_nt(a, b):
    return lax.dot_general(a, b, (((1,), (1,)), ((), ())), preferred_element_type=F32)


def _dot_tn(a, b):
    return lax.dot_general(a, b, (((0,), (0,)), ((), ())), preferred_element_type=F32)


def _dot_hi(a, b):
    return jnp.dot(a, b, precision=HIGHEST, preferred_element_type=F32)


def _aligned(x, m):
    return x if isinstance(x, int) else pl.multiple_of(x, m)


def _params(sem):
    return pltpu.CompilerParams(dimension_semantics=sem, vmem_limit_bytes=VMEM_LIMIT)


def _mod_kernel(cc_ref, w_ref, b_ref, o_ref):
    a = _silu(cc_ref[...]).astype(BF16)
    o_ref[0] = _dot(a, w_ref[0].astype(BF16)) + b_ref[0]


def _modulation(cc, w_ada, b_ada):
    tn = 1024
    return pl.pallas_call(
        _mod_kernel,
        grid=(DEPTH, 3 * D_MODEL // tn),
        in_specs=[
            pl.BlockSpec((8, D_MODEL), lambda l, n: (0, 0)),
            pl.BlockSpec((1, D_MODEL, tn), lambda l, n: (l, 0, n)),
            pl.BlockSpec((1, 1, tn), lambda l, n: (l, 0, n)),
        ],
        out_specs=pl.BlockSpec((1, 8, tn), lambda l, n: (l, 0, n)),
        out_shape=jax.ShapeDtypeStruct((DEPTH, 8, 3 * D_MODEL), F32),
        compiler_params=_params(("parallel", "parallel")),
        name="adaln_mod",
    )(cc, w_ada, b_ada.reshape(DEPTH, 1, 3 * D_MODEL))


def _mod_row(mod_ref, tiles_per_batch, mod_row0):
    if tiles_per_batch is None:
        return mod_ref[0:1, :]
    row = mod_row0 + pl.program_id(0) // tiles_per_batch
    return mod_ref[pl.ds(row, 1), :]


def _proj_kernel(x_ref, mod_ref, g_ref, w_ref, wba_ref, o_ref, ba_ref, h_ref, *, tiles_per_batch, mod_row0):
    @pl.when(pl.program_id(1) == 0)
    def _():
        mod = _mod_row(mod_ref, tiles_per_batch, mod_row0)
        shift = mod[:, 0:D_MODEL]
        scale = mod[:, D_MODEL:2 * D_MODEL]
        x = x_ref[...]
        y = x * lax.rsqrt(jnp.mean(x * x, axis=-1, keepdims=True) + NORM_EPS) * g_ref[...]
        hb = (y * (1.0 + scale) + shift).astype(BF16)
        h_ref[...] = hb
        ba_ref[...] = _dot(hb, wba_ref[...])

    o_ref[...] = _dot(h_ref[...], w_ref[...])


def _project(x2d, mod_l, g_pre, w_main, w_ba, tiles_per_batch, mod_row0, tm=512, tn=1024):
    ntok = x2d.shape[0]
    kern = functools.partial(_proj_kernel, tiles_per_batch=tiles_per_batch, mod_row0=mod_row0)
    return pl.pallas_call(
        kern,
        grid=(ntok // tm, P_MAIN // tn),
        in_specs=[
            pl.BlockSpec((tm, D_MODEL), lambda m, n: (m, 0)),
            pl.BlockSpec((8, 3 * D_MODEL), lambda m, n: (0, 0)),
            pl.BlockSpec((1, D_MODEL), lambda m, n: (0, 0)),
            pl.BlockSpec((D_MODEL, tn), lambda m, n: (0, n)),
            pl.BlockSpec((D_MODEL, LANE), lambda m, n: (0, 0)),
        ],
        out_specs=[
            pl.BlockSpec((tm, tn), lambda m, n: (m, n)),
            pl.BlockSpec((tm, LANE), lambda m, n: (m, 0)),
        ],
        out_shape=[
            jax.ShapeDtypeStruct((ntok, P_MAIN), F32),
            jax.ShapeDtypeStruct((ntok, LANE), F32),
        ],
        scratch_shapes=[pltpu.VMEM((tm, D_MODEL), BF16)],
        compiler_params=_params(("parallel", "arbitrary")),
        name="in_proj",
    )(x2d, mod_l, g_pre, w_main, w_ba)


def _delta_kernel(*refs, T, has_s0, want_state):
    it = iter(refs)
    q_ref, k_ref, v_ref, z_ref, ba_ref = (next(it) for _ in range(5))
    cwq_ref, cwk_ref, cwv_ref, alog_ref, dtb_ref, gn_ref = (next(it) for _ in range(6))
    s0_ref = next(it) if has_s0 else None
    y_ref = next(it)
    sfin_ref = next(it) if want_state else None
    pad_ref, qn_ref, kn_ref, vn_ref, bb_ref, la_ref, of_ref, ob_ref = (next(it) for _ in range(8))
    u_ref, wq_ref, kg_ref, attn_ref, gl_ref = (next(it) for _ in range(5))

    head = pl.program_id(1)
    C = DN_CHUNK
    NC = T // C
    RB = min(T, 256)

    zeros8 = jnp.zeros((8, LANE), F32)
    pad_ref[0:8, :] = zeros8
    pad_ref[T + 8:T + 16, :] = zeros8

    def conv_into(x_ref, cw_ref, dst_ref, mode):
        for r0 in range(0, T, RB):
            pad_ref[8 + r0:8 + r0 + RB, :] = x_ref[0, r0:r0 + RB, :]
        w = cw_ref[...]
        for r0 in range(0, T, RB):
            y = (pad_ref[7 + r0:7 + r0 + RB, :] * w[0:1, :]
                 + pad_ref[8 + r0:8 + r0 + RB, :] * w[1:2, :]
                 + pad_ref[9 + r0:9 + r0 + RB, :] * w[2:3, :])
            y = _silu(y)
            if mode != "v":
                y = y * lax.rsqrt(jnp.sum(y * y, axis=-1, keepdims=True) + NORM_EPS)
            if mode == "q":
                y = y * (DN_DK ** -0.5)
            dst_ref[r0:r0 + RB, :] = y

    conv_into(q_ref, cwq_ref, qn_ref, "q")
    conv_into(k_ref, cwk_ref, kn_ref, "k")
    conv_into(v_ref, cwv_ref, vn_ref, "v")

    lane = lax.broadcasted_iota(jnp.int32, (RB, LANE), 1)
    neg_a = -jnp.exp(alog_ref[...])
    for r0 in range(0, T, RB):
        ba = ba_ref[0, r0:r0 + RB, :]
        beta_all = _sigmoid(ba)
        la_all = neg_a * _softplus(ba + dtb_ref[...])
        for d in range(2):
            bcol = jnp.sum(jnp.where(lane == d * DN_HEADS + head, beta_all, 0.0), axis=1, keepdims=True)
            lcol = jnp.sum(jnp.where(lane == 2 * DN_HEADS + d * DN_HEADS + head, la_all, 0.0), axis=1,
                           keepdims=True)
            bb_ref[d, r0:r0 + RB, :] = jnp.broadcast_to(bcol, (RB, LANE))
            la_ref[d, r0:r0 + RB, :] = jnp.broadcast_to(lcol, (RB, LANE))

    GR = DN_GROUP * C
    NG = T // GR
    gr_r = lax.broadcasted_iota(jnp.int32, (GR, LANE), 0) & (C - 1)
    gr_c = lax.broadcasted_iota(jnp.int32, (GR, LANE), 1)
    is_b = gr_c >= C
    gr_j = gr_c & (C - 1)
    dist = jnp.where(is_b, gr_j - gr_r, gr_r - gr_j)
    incl_g = dist >= 0
    strict_g = dist > 0
    rowsum_g = dist <= 0
    blk_g = jnp.right_shift(gr_r, DN_BLK_SHIFT) == jnp.right_shift(gr_j, DN_BLK_SHIFT)
    bd_r = lax.broadcasted_iota(jnp.int32, (GR, GR), 0)
    bd_c = lax.broadcasted_iota(jnp.int32, (GR, GR), 1)
    same_chunk = jnp.right_shift(bd_r, 6) == jnp.right_shift(bd_c, 6)
    bd_lower = (same_chunk & (bd_r >= bd_c)).astype(BF16)
    bd_upper = (same_chunk & (bd_r <= bd_c)).astype(BF16)
    bd_ones = same_chunk.astype(BF16)
    half_l = (lax.broadcasted_iota(jnp.int32, (C, LANE), 1) < C)

    def split3(x):
        hi = x.astype(BF16)
        r1 = x - hi.astype(F32)
        mid = r1.astype(BF16)
        lo = (r1 - mid.astype(F32)).astype(BF16)
        return hi, mid, lo

    def dot01(m01, x):
        hi, mid, lo = split3(x)
        return _dot(m01, hi) + _dot(m01, mid) + _dot(m01, lo)

    def block_diag(x):
        return jnp.concatenate([jnp.where(half_l, x, 0.0), jnp.where(half_l, 0.0, x)], axis=0).astype(BF16)

    def mm2(x, y):
        return _dot(x.astype(BF16), block_diag(y))

    def mm2s(xs, ys):
        return [mm2(x, y) for x, y in zip(xs, ys)]

    def tri_inv_minus_eye(lms):
        blk = blk_g[0:C, :]
        md = [jnp.where(blk, -lm, 0.0) for lm in lms]
        md2 = mm2s(md, md)
        md4 = mm2s(md2, md2)
        md3 = mm2s(md, md2)
        md8 = mm2s(md4, md4)
        px = [a + b + c for a, b, c in zip(md, md2, md3)]
        md12 = mm2s(md4, md8)
        qx = [a + b + c for a, b, c in zip(md4, md8, md12)]
        pq = mm2s(px, qx)
        dx = [a + b + c for a, b, c in zip(px, qx, pq)]
        loff = [jnp.where(blk, 0.0, lm) for lm in lms]
        dl = mm2s(dx, loff)
        n1 = [a + b for a, b in zip(loff, dl)]
        n2 = mm2s(n1, n1)
        n3 = mm2s(n1, n2)
        rx = [b - a - c for a, b, c in zip(n1, n2, n3)]
        rd = mm2s(rx, dx)
        return [a + b + c for a, b, c in zip(rx, dx, rd)]

    def local_group(g, carry):
        rows = pl.ds(_aligned(g * GR, GR), GR)
        la_f = la_ref[0, rows, :]
        la_b = la_ref[1, rows, :]
        gc_f = dot01(bd_lower, la_f)
        gc_b = dot01(bd_upper, la_b)
        g_row = dot01(bd_ones, jnp.where(rowsum_g, jnp.where(is_b, la_b, la_f), 0.0))
        decay = jnp.exp(jnp.where(incl_g, jnp.where(is_b, gc_b, gc_f) - g_row, NEG_INF))
        eg_f = jnp.exp(gc_f)
        eg_b = jnp.exp(gc_b)
        bb_f = bb_ref[0, rows, :]
        bb_b = bb_ref[1, rows, :]
        bb_p = jnp.where(is_b, bb_b, bb_f)
        k = kn_ref[rows, :]
        q = qn_ref[rows, :]
        v = vn_ref[rows, :]
        kb_f = k * bb_f
        kb_b = k * bb_b
        rhs_f = jnp.concatenate([v * bb_f, kb_f * eg_f], axis=1)
        rhs_b = jnp.concatenate([v * bb_b, kb_b * eg_b], axis=1)
        qg_f = (q * eg_f).astype(BF16)
        qg_b = (q * eg_b).astype(BF16)
        zeros_r = jnp.zeros((C, 2 * LANE), BF16)
        crs = [slice(ci * C, (ci + 1) * C) for ci in range(DN_GROUP)]
        cs = [g * DN_GROUP + ci for ci in range(DN_GROUP)]
        crows = [pl.ds(_aligned(c * C, C), C) for c in cs]
        kb16 = k.astype(BF16)
        q16 = q.astype(BF16)
        grams = [_dot_nt(jnp.concatenate([kb16[cr], q16[cr]], axis=0),
                         jnp.concatenate([kb16[cr], kb16[cr]], axis=0)) for cr in crs]
        lms = [jnp.where(strict_g[cr], gm[0:C] * bb_p[cr] * decay[cr], 0.0) for gm, cr in zip(grams, crs)]
        for gm, cr, crow in zip(grams, crs, crows):
            attn_ref[crow, :] = (gm[C:2 * C] * decay[cr]).astype(BF16)
        txs = tri_inv_minus_eye(lms)
        rhs16_f = rhs_f.astype(BF16)
        rhs16_b = rhs_b.astype(BF16)
        uws = [_dot(tx.astype(BF16), jnp.concatenate(
                    [jnp.concatenate([rhs16_f[cr], zeros_r], axis=1),
                     jnp.concatenate([zeros_r, rhs16_b[cr]], axis=1)], axis=0))
               for tx, cr in zip(txs, crs)]
        for ci, (uw, cr, c, crow) in enumerate(zip(uws, crs, cs, crows)):
            uw_f = rhs_f[cr] + uw[:, 0:2 * LANE]
            uw_b = rhs_b[cr] + uw[:, 2 * LANE:4 * LANE]
            u_ref[0, crow, :] = uw_f[:, 0:LANE]
            u_ref[1, crow, :] = uw_b[:, 0:LANE]
            wq_ref[0, c, 0:C, :] = uw_f[:, LANE:2 * LANE].astype(BF16)
            wq_ref[1, c, 0:C, :] = uw_b[:, LANE:2 * LANE].astype(BF16)
            wq_ref[0, c, C:2 * C, :] = qg_f[cr]
            wq_ref[1, c, C:2 * C, :] = qg_b[cr]
            gl_f = gc_f[ci * C + C - 1:ci * C + C, :]
            gl_b = gc_b[ci * C:ci * C + 1, :]
            kg_ref[0, crow, :] = (k[cr] * jnp.exp(gl_f - gc_f[cr])).astype(BF16)
            kg_ref[1, crow, :] = (k[cr] * jnp.exp(gl_b - gc_b[cr])).astype(BF16)
            gl_ref[0, c] = jnp.broadcast_to(jnp.exp(gl_f), (8, LANE))
            gl_ref[1, c] = jnp.broadcast_to(jnp.exp(gl_b), (8, LANE))
        return carry

    if NG == 1:
        local_group(0, 0)
    else:
        lax.fori_loop(0, NG, local_group, 0)

    zeros_v = jnp.zeros((C, LANE), BF16)

    def body(i, carry):
        s_f, s_b = carry
        cf, cb = i, NC - 1 - i
        rows_f = pl.ds(pl.multiple_of(cf * C, C), C)
        rows_b = pl.ds(pl.multiple_of(cb * C, C), C)
        r_f = _dot(wq_ref[0, cf], s_f.astype(BF16))
        r_b = _dot(wq_ref[1, cb], s_b.astype(BF16))
        vb_f = (u_ref[0, rows_f, :] - r_f[0:C]).astype(BF16)
        vb_b = (u_ref[1, rows_b, :] - r_b[0:C]).astype(BF16)
        a_f = attn_ref[rows_f, :]
        a_b = attn_ref[rows_b, :]
        sn_f = _dot_tn(kg_ref[0, rows_f, :], vb_f)
        sn_b = _dot_tn(kg_ref[1, rows_b, :], vb_b)
        o2_f = _dot(jnp.where(half_l, a_f, jnp.zeros_like(a_f)), jnp.concatenate([vb_f, zeros_v], axis=0))
        o2_b = _dot(jnp.where(half_l, jnp.zeros_like(a_b), a_b), jnp.concatenate([zeros_v, vb_b], axis=0))
        of_ref[rows_f, :] = r_f[C:2 * C] + o2_f
        ob_ref[rows_b, :] = r_b[C:2 * C] + o2_b
        return s_f * gl_ref[0, cf, 0:1, :] + sn_f, s_b * gl_ref[1, cb, 0:1, :] + sn_b

    if has_s0:
        init = (s0_ref[0, 0, 0], s0_ref[0, 1, 0])
    else:
        init = (jnp.zeros((DN_DK, LANE), F32), jnp.zeros((DN_DK, LANE), F32))
    s_f, s_b = lax.fori_loop(0, NC, body, init)
    if want_state:
        sfin_ref[0, 0, 0] = s_f
        sfin_ref[0, 1, 0] = s_b

    for r0 in range(0, T, RB):
        o = of_ref[r0:r0 + RB, :] + ob_ref[r0:r0 + RB, :]
        o = o * lax.rsqrt(jnp.mean(o * o, axis=-1, keepdims=True) + NORM_EPS) * gn_ref[...]
        y_ref[0, r0:r0 + RB, :] = o * _silu(z_ref[0, r0:r0 + RB, :])


def _delta_mixer(proj3, ba3, conv_w, alog_v, dtb_v, g_norm, s0, want_state):
    B, T, _ = proj3.shape
    has_s0 = s0 is not None
    kern = functools.partial(_delta_kernel, T=T, has_s0=has_s0, want_state=want_state)
    col = lambda cb: pl.BlockSpec((1, T, LANE), lambda b, h, cb=cb: (b, 0, cb + h))
    cw = lambda cb: pl.BlockSpec((3, LANE), lambda b, h, cb=cb: (0, cb + h))
    vec = pl.BlockSpec((1, LANE), lambda b, h: (0, 0))
    state_spec = pl.BlockSpec((1, 2, 1, DN_DK, LANE), lambda b, h: (b, 0, h, 0, 0))
    in_specs = [col(CB_DN_Q), col(CB_DN_K), col(CB_DN_V), col(CB_DN_Z),
                pl.BlockSpec((1, T, LANE), lambda b, h: (b, 0, 0)),
                cw(0), cw(4), cw(8), vec, vec, vec]
    args = [proj3, proj3, proj3, proj3, ba3, conv_w, conv_w, conv_w, alog_v, dtb_v, g_norm]
    if has_s0:
        in_specs.append(state_spec)
        args.append(s0)
    out_specs = [pl.BlockSpec((1, T, LANE), lambda b, h: (b, 0, h))]
    out_shape = [jax.ShapeDtypeStruct((B, T, DN_WIDTH), F32)]
    if want_state:
        out_specs.append(state_spec)
        out_shape.append(jax.ShapeDtypeStruct((B, 2, DN_HEADS, DN_DK, LANE), F32))
    res = pl.pallas_call(
        kern,
        grid=(B, DN_HEADS),
        in_specs=in_specs,
        out_specs=out_specs,
        out_shape=out_shape,
        scratch_shapes=[
            pltpu.VMEM((T + 16, LANE), F32),
            pltpu.VMEM((T, LANE), F32), pltpu.VMEM((T, LANE), F32), pltpu.VMEM((T, LANE), F32),
            pltpu.VMEM((2, T, LANE), F32), pltpu.VMEM((2, T, LANE), F32),
            pltpu.VMEM((T, LANE), F32), pltpu.VMEM((T, LANE), F32),
            pltpu.VMEM((2, T, LANE), F32),
            pltpu.VMEM((2, T // DN_CHUNK, 2 * DN_CHUNK, LANE), BF16),
            pltpu.VMEM((2, T, LANE), BF16),
            pltpu.VMEM((T, LANE), BF16),
            pltpu.VMEM((2, T // DN_CHUNK, 8, LANE), F32),
        ],
        compiler_params=_params(("parallel", "parallel")),
        name="delta_mixer",
    )(*args)
    return (res[0], res[1]) if want_state else (res[0], None)


def _head_masks(rows):
    lane = lax.broadcasted_iota(jnp.int32, (rows, LANE), 1)
    return (lane < NA_HD).astype(F32), (lane >= NA_HD).astype(F32)


def _ctx_attn_kernel(q_ref, k_ref, v_ref, z_ref, y_ref, *, T):
    m0, m1 = _head_masks(T)
    scale = NA_HD ** -0.5
    for p in range(NA_WIDTH // LANE):
        cols = slice(p * LANE, (p + 1) * LANE)
        q = q_ref[0, :, cols]
        kb = k_ref[0, :, cols].astype(BF16)
        vb = v_ref[0, :, cols].astype(BF16)
        out = jnp.zeros((T, LANE), F32)
        for hm in (m0, m1):
            s = _dot_nt((q * hm).astype(BF16), kb) * scale
            e = jnp.exp(s - jnp.max(s, axis=-1, keepdims=True))
            o = _dot(e.astype(BF16), vb) / jnp.sum(e, axis=-1, keepdims=True)
            out = out + o * hm
        y_ref[0, :, cols] = out * _silu(z_ref[0, :, cols])


def _ctx_attention(proj3):
    B, T, _ = proj3.shape
    wide = NA_WIDTH // LANE
    spec = lambda cb: pl.BlockSpec((1, T, NA_WIDTH), lambda b, cb=cb: (b, 0, cb // wide))
    return pl.pallas_call(
        functools.partial(_ctx_attn_kernel, T=T),
        grid=(B,),
        in_specs=[spec(CB_NA_Q), spec(CB_NA_K), spec(CB_NA_V), spec(CB_NA_Z)],
        out_specs=pl.BlockSpec((1, T, NA_WIDTH), lambda b: (b, 0, 0)),
        out_shape=jax.ShapeDtypeStruct((B, T, NA_WIDTH), F32),
        compiler_params=_params(("parallel",)),
        name="ctx_attention",
    )(proj3, proj3, proj3, proj3)


def _na_kernel(q_ref, k_ref, v_ref, z_ref, kc_ref, vc_ref, bias_ref, y_ref, sctx_ref, *, T, L):
    W = GRID_W
    rows_total = T // W
    n_win = NA_WIN_R * W
    scale = NA_HD ** -0.5
    m0, m1 = _head_masks(W)
    kcb = kc_ref[0].astype(BF16)
    vcb = vc_ref[0].astype(BF16)

    RB = 256
    mb0, mb1 = _head_masks(RB)
    for r0 in range(0, T, RB):
        q = q_ref[0, r0:r0 + RB, :]
        sctx_ref[0, r0:r0 + RB, :] = _dot_nt((q * mb0).astype(BF16), kcb) * scale
        sctx_ref[1, r0:r0 + RB, :] = _dot_nt((q * mb1).astype(BF16), kcb) * scale

    qc = lax.broadcasted_iota(jnp.int32, (2 * W, n_win), 0) & (W - 1)
    kcol = lax.broadcasted_iota(jnp.int32, (2 * W, n_win), 1) & (W - 1)
    cs = jnp.clip(qc - NA_WIN_C // 2, 0, W - NA_WIN_C)
    col_ok = (kcol >= cs) & (kcol < cs + NA_WIN_C)

    def row_body(r, carry):
        rs = jnp.clip(r - NA_WIN_R // 2, 0, rows_total - NA_WIN_R)
        qrows = pl.ds(pl.multiple_of(r * W, W), W)
        krows = pl.ds(pl.multiple_of(rs * W, W), n_win)
        q = q_ref[0, qrows, :]
        qs = jnp.concatenate([q * m0, q * m1], axis=0).astype(BF16)
        kw = k_ref[0, krows, :].astype(BF16)
        vw = v_ref[0, krows, :].astype(BF16)
        dr0 = rs - r + NA_WIN_R - 1
        bias = jnp.concatenate(
            [jnp.concatenate([bias_ref[hh, dr0 + 2 * m] for m in range(NA_WIN_R // 2)], axis=1)
             for hh in range(2)], axis=0)
        s = jnp.where(col_ok, _dot_nt(qs, kw) * scale + bias, NEG_INF)
        sc = jnp.concatenate([sctx_ref[0, qrows, :], sctx_ref[1, qrows, :]], axis=0)
        mx = jnp.maximum(jnp.max(s, axis=-1, keepdims=True), jnp.max(sc, axis=-1, keepdims=True))
        e1 = jnp.exp(s - mx)
        e2 = jnp.exp(sc - mx)
        den = jnp.sum(e1, axis=-1, keepdims=True) + jnp.sum(e2, axis=-1, keepdims=True)
        o = (_dot(e1.astype(BF16), vw) + _dot(e2.astype(BF16), vcb)) / den
        out = o[0:W, :] * m0 + o[W:2 * W, :] * m1
        y_ref[0, qrows, :] = out * _silu(z_ref[0, qrows, :])
        return carry

    lax.fori_loop(0, rows_total, row_body, 0)


def _neighbourhood_attention(proj3, k_ctx, v_ctx, bias2):
    B, T, _ = proj3.shape
    L = k_ctx.shape[1]
    col = lambda cb: pl.BlockSpec((1, T, LANE), lambda b, p, cb=cb: (b, 0, cb + p))
    ctx = pl.BlockSpec((1, L, LANE), lambda b, p: (b, 0, p))
    return pl.pallas_call(
        functools.partial(_na_kernel, T=T, L=L),
        grid=(B, NA_WIDTH // LANE),
        in_specs=[col(CB_NA_Q), col(CB_NA_K), col(CB_NA_V), col(CB_NA_Z), ctx, ctx,
                  pl.BlockSpec((2, 2 * NA_WIN_R - 2, GRID_W, LANE), lambda b, p: (p, 0, 0, 0))],
        out_specs=pl.BlockSpec((1, T, LANE), lambda b, p: (b, 0, p)),
        out_shape=jax.ShapeDtypeStruct((B, T, NA_WIDTH), F32),
        scratch_shapes=[pltpu.VMEM((2, T, L), F32)],
        compiler_params=_params(("parallel", "parallel")),
        name="nbr_attention",
    )(proj3, proj3, proj3, proj3, k_ctx, v_ctx, bias2)


def _pool_kernel(u_ref, z_ref, pw_ref, ps_ref, y_ref, x_buf, s2_buf, s4_buf, s8_buf, s16_buf, *, T):
    P = POOL_PAD
    E = T + 2 * P
    RB = 256
    bufs = (x_buf, s2_buf, s4_buf, s8_buf, s16_buf)
    for b in bufs:
        zeros = jnp.zeros((P, b.shape[1]), F32)
        b[0:P, :] = zeros
        b[E + P:E + 2 * P, :] = zeros
    zeros = jnp.zeros((P, POOL_WIDTH), F32)
    x_buf[P:2 * P, :] = zeros
    x_buf[T + 2 * P:T + 3 * P, :] = zeros
    for r0 in range(0, T, RB):
        x_buf[2 * P + r0:2 * P + r0 + RB, :] = u_ref[0, r0:r0 + RB, :]

    def level(src, dst, lo, hi, lane0):
        for e0 in range(0, E, RB):
            n = min(RB, E - e0)
            dst[P + e0:P + e0 + n, :] = (src[P + e0 + lo:P + e0 + lo + n, lane0:]
                                         + src[P + e0 + hi:P + e0 + hi + n, lane0:])

    level(x_buf, s2_buf, -1, 0, 0)
    level(s2_buf, s4_buf, -1, 1, LANE)
    level(s4_buf, s8_buf, -2, 2, LANE)
    level(s8_buf, s16_buf, -4, 4, LANE)

    sums = (s2_buf, s4_buf, s8_buf, s16_buf)
    for r0 in range(0, T, RB):
        t = r0 + lax.broadcasted_iota(jnp.int32, (RB, 1), 0)
        for g, win in enumerate(POOL_WINDOWS):
            cols = slice(g * POOL_GC, (g + 1) * POOL_GC)
            lo = jnp.maximum(t - win // 2, 0)
            hi = jnp.minimum(t + win // 2 - 1, T - 1)
            cnt = (hi - lo + 1).astype(F32)
            tot = sums[g][2 * P + r0:2 * P + r0 + RB, 0:POOL_GC]
            pooled = tot / cnt - x_buf[2 * P + r0:2 * P + r0 + RB, cols]
            y = _dot(pooled.astype(BF16), pw_ref[g]) * ps_ref[:, cols]
            y_ref[0, r0:r0 + RB, cols] = y * _silu(z_ref[0, r0:r0 + RB, cols])


def _pool_mixer(proj3, pool_w, pool_scale):
    B, T, _ = proj3.shape
    rows = T + 4 * POOL_PAD
    return pl.pallas_call(
        functools.partial(_pool_kernel, T=T),
        grid=(B,),
        in_specs=[
            pl.BlockSpec((1, T, POOL_WIDTH), lambda b: (b, 0, 8)),
            pl.BlockSpec((1, T, POOL_WIDTH), lambda b: (b, 0, 9)),
            pl.BlockSpec((len(POOL_WINDOWS), POOL_GC, POOL_GC), lambda b: (0, 0, 0)),
            pl.BlockSpec((1, POOL_WIDTH), lambda b: (0, 0)),
        ],
        out_specs=pl.BlockSpec((1, T, POOL_WIDTH), lambda b: (b, 0, 0)),
        out_shape=jax.ShapeDtypeStruct((B, T, POOL_WIDTH), F32),
        scratch_shapes=[
            pltpu.VMEM((rows, 4 * LANE), F32), pltpu.VMEM((rows, 4 * LANE), F32),
            pltpu.VMEM((rows, 3 * LANE), F32), pltpu.VMEM((rows, 2 * LANE), F32),
            pltpu.VMEM((rows, LANE), F32),
        ],
        compiler_params=_params(("parallel",)),
        name="pool_mixer",
    )(proj3, proj3, pool_w, pool_scale)


def _merge_kernel(ydn_ref, yna_ref, ypl_ref, gdn_ref, gna_ref, gpl_ref, x_ref, mod_ref, gpost_ref,
                  wd_ref, wn_ref, wp_ref, wo_ref, o_ref, *, tiles_per_batch, mod_row0):
    merged = (_sigmoid(gdn_ref[...]) * _dot(ydn_ref[...].astype(BF16), wd_ref[...])
              + _sigmoid(gna_ref[...]) * _dot(yna_ref[...].astype(BF16), wn_ref[...])
              + _sigmoid(gpl_ref[...]) * _dot(ypl_ref[...].astype(BF16), wp_ref[...]))
    out = _dot(merged.astype(BF16), wo_ref[...])
    out = out * lax.rsqrt(jnp.mean(out * out, axis=-1, keepdims=True) + NORM_EPS) * gpost_ref[...]
    gate = _mod_row(mod_ref, tiles_per_batch, mod_row0)[:, 2 * D_MODEL:3 * D_MODEL]
    o_ref[...] = x_ref[...] + gate * out


def _merge(y_dn, y_na, y_pl, proj2d, x2d, mod_l, g_post, w_dn, w_na, w_pl, w_out, tiles_per_batch, mod_row0,
           tm=512):
    ntok = x2d.shape[0]
    br = pl.BlockSpec((tm, DN_WIDTH), lambda m: (m, 0))
    gate = lambda j: pl.BlockSpec((tm, D_MODEL), lambda m, j=j: (m, j))
    wbr = pl.BlockSpec((DN_WIDTH, D_MODEL), lambda m: (0, 0))
    return pl.pallas_call(
        functools.partial(_merge_kernel, tiles_per_batch=tiles_per_batch, mod_row0=mod_row0),
        grid=(ntok // tm,),
        in_specs=[br, br, br, gate(5), gate(6), gate(7),
                  pl.BlockSpec((tm, D_MODEL), lambda m: (m, 0)),
                  pl.BlockSpec((8, 3 * D_MODEL), lambda m: (0, 0)),
                  pl.BlockSpec((1, D_MODEL), lambda m: (0, 0)),
                  wbr, wbr, wbr,
                  pl.BlockSpec((D_MODEL, D_MODEL), lambda m: (0, 0))],
        out_specs=pl.BlockSpec((tm, D_MODEL), lambda m: (m, 0)),
        out_shape=jax.ShapeDtypeStruct((ntok, D_MODEL), F32),
        compiler_params=_params(("parallel",)),
        name="merge_out",
    )(y_dn, y_na, y_pl, proj2d, proj2d, proj2d, x2d, mod_l, g_post, w_dn, w_na, w_pl, w_out)


def _layer(x3, mod_l, tiles_per_batch, mod_row0, s0, k_ctx, v_ctx, lw, tm=512):
    B, T, _ = x3.shape
    x2d = x3.reshape(B * T, D_MODEL)
    proj2d, ba2d = _project(x2d, mod_l, lw["g_pre"], lw["w_main"], lw["w_ba"], tiles_per_batch, mod_row0, tm=tm)
    proj3 = proj2d.reshape(B, T, P_MAIN)
    ba3 = ba2d.reshape(B, T, LANE)
    is_ctx = k_ctx is None
    y_dn, s_fin = _delta_mixer(proj3, ba3, lw["conv_w"], lw["alog_v"], lw["dtb_v"], lw["g_norm"], s0,
                               want_state=is_ctx)
    if is_ctx:
        y_na = _ctx_attention(proj3)
    else:
        y_na = _neighbourhood_attention(proj3, k_ctx, v_ctx, lw["bias2"])
    y_pl = _pool_mixer(proj3, lw["pool_w"], lw["pool_scale"])
    n = B * T
    out2d = _merge(y_dn.reshape(n, DN_WIDTH), y_na.reshape(n, NA_WIDTH), y_pl.reshape(n, POOL_WIDTH), proj2d,
                   x2d, mod_l, lw["g_post"], lw["w_dn"], lw["w_na"], lw["w_pl"], lw["w_out"],
                   tiles_per_batch, mod_row0, tm=tm)
    return out2d.reshape(B, T, D_MODEL), proj3, s_fin


def _lane_vec(v, offset):
    return jnp.zeros((1, LANE), F32).at[0, offset:offset + 2 * DN_HEADS].set(v.reshape(-1))


def kernel(x_prompt, x_sample, c, cache_k_na, cache_v_na, state_dn, c_ctx, w_ada, b_ada, g_pre, g_post, w_in,
           conv_dn, a_log_dn, dt_bias_dn, g_norm_dn, na_bias, pool_w, pool_scale, w_br_dn, w_br_na, w_br_pl,
           w_out):
    B, T, _ = x_prompt.shape
    DB, DT, _ = x_sample.shape
    L = cache_k_na.shape[2]
    tm = 512

    cc = jnp.zeros((8, D_MODEL), F32).at[0].set(c_ctx).at[1:1 + DB].set(c)
    mod = _modulation(cc, w_ada, b_ada)

    w_main = jnp.concatenate([w_in[:, :, :OFF_BA_SRC], w_in[:, :, OFF_BA_SRC + N_BA:]], axis=-1).astype(BF16)
    w_ba = jnp.pad(w_in[:, :, OFF_BA_SRC:OFF_BA_SRC + N_BA], ((0, 0), (0, 0), (0, LANE - N_BA))).astype(BF16)
    dcol = np.clip(np.arange(GRID_W)[None, :] - np.arange(GRID_W)[:, None] + NA_WIN_C - 1, 0, 2 * NA_WIN_C - 2)
    bias_t = na_bias[:, :, :, dcol]
    bias2 = jnp.concatenate([bias_t[:, :, :-1], bias_t[:, :, 1:]], axis=-1)

    xp, xs = x_prompt, x_sample
    k_list, v_list, s_list = [], [], []
    for l in range(DEPTH):
        lw = dict(
            g_pre=g_pre[l][None], g_post=g_post[l][None], w_main=w_main[l], w_ba=w_ba[l], conv_w=conv_dn[l],
            alog_v=_lane_vec(a_log_dn[l], 2 * DN_HEADS), dtb_v=_lane_vec(dt_bias_dn[l], 2 * DN_HEADS),
            g_norm=g_norm_dn[l][None], bias2=bias2[l], pool_w=pool_w[l].astype(BF16),
            pool_scale=pool_scale[l][None], w_dn=w_br_dn[l].astype(BF16), w_na=w_br_na[l].astype(BF16),
            w_pl=w_br_pl[l].astype(BF16), w_out=w_out[l].astype(BF16))
        xp, proj_c, s_fin = _layer(xp, mod[l], None, 0, None, None, None, lw, tm=tm)
        k_list.append(proj_c[:, :, CB_NA_K * LANE:CB_NA_K * LANE + NA_WIDTH].reshape(B, T, NA_HEADS, NA_HD))
        v_list.append(proj_c[:, :, CB_NA_V * LANE:CB_NA_V * LANE + NA_WIDTH].reshape(B, T, NA_HEADS, NA_HD))
        s_list.append(s_fin)
        xs, _, _ = _layer(xs, mod[l], DT // tm, 1, state_dn[:, l],
                          cache_k_na[:, l].reshape(DB, L, NA_WIDTH), cache_v_na[:, l].reshape(DB, L, NA_WIDTH),
                          lw, tm=tm)
    return (xp, xs, jnp.stack(k_list, axis=1), jnp.stack(v_list, axis=1), jnp.stack(s_list, axis=1))
```

```python
import functools

import numpy as np
import jax
import jax.numpy as jnp
from jax import lax
from jax.experimental import pallas as pl
from jax.experimental.pallas import tpu as pltpu

F32 = jnp.float32
BF16 = jnp.bfloat16

D_MODEL = 1024
DEPTH = 4
GRID_W = 64
NORM_EPS = 1e-6
NEG_INF = -1e30

DN_HEADS = 4
DN_DK = 128
DN_WIDTH = 512
DN_CHUNK = 64
DN_BLK_SHIFT = 4
DN_UNIT = 4

NA_HEADS = 8
NA_HD = 64
NA_WIDTH = 512
NA_WIN_R = 8
NA_WIN_C = 16
NA_ROWS_PER_STEP = 4

POOL_WINDOWS = (2, 4, 8, 16)
POOL_GC = 128
POOL_WIDTH = 512
POOL_PAD = 8

LANE = 128
P_MAIN = 8192
CB_DN_Q, CB_DN_K, CB_DN_V, CB_DN_Z = 0, 4, 8, 12
CB_NA_K, CB_NA_V = 16, 20
CB_NA_Q, CB_NA_Z = 24, 28
CB_PL_U, CB_PL_Z = 32, 36
CB_GATE = 40
SRC_COLS = ((0, 2048), (2576, 3600), (2064, 2576), (3600, 8208))
OFF_BA_SRC = 2048
N_BA = 16

VMEM_LIMIT = 56 * 1024 * 1024


def _sigmoid(x):
    return 1.0 / (1.0 + jnp.exp(-x))


def _silu(x):
    return x * _sigmoid(x)


def _softplus(x):
    return jnp.maximum(x, 0.0) + jnp.log1p(jnp.exp(-jnp.abs(x)))


def _dot(a, b):
    return jnp.dot(a, b, preferred_element_type=F32)


def _dot_nt(a, b):
    return lax.dot_general(a, b, (((1,), (1,)), ((), ())), preferred_element_type=F32)


def _dot_tn(a, b):
    return lax.dot_general(a, b, (((0,), (0,)), ((), ())), preferred_element_type=F32)


def _aligned(x, m):
    return x if isinstance(x, int) else pl.multiple_of(x, m)


def _params(sem):
    return pltpu.CompilerParams(dimension_semantics=sem, vmem_limit_bytes=VMEM_LIMIT)


def _mod_kernel(cc_ref, w_ref, b_ref, o_ref):
    a = _silu(cc_ref[...]).astype(BF16)
    o_ref[0] = _dot(a, w_ref[0].astype(BF16)) + b_ref[0]


def _modulation(cc, w_ada, b_ada):
    tn = 1024
    return pl.pallas_call(
        _mod_kernel,
        grid=(DEPTH, 3 * D_MODEL // tn),
        in_specs=[
            pl.BlockSpec((8, D_MODEL), lambda l, n: (0, 0)),
            pl.BlockSpec((1, D_MODEL, tn), lambda l, n: (l, 0, n)),
            pl.BlockSpec((1, 1, tn), lambda l, n: (l, 0, n)),
        ],
        out_specs=pl.BlockSpec((1, 8, tn), lambda l, n: (l, 0, n)),
        out_shape=jax.ShapeDtypeStruct((DEPTH, 8, 3 * D_MODEL), F32),
        compiler_params=_params(("parallel", "parallel")),
        name="adaln_mod",
    )(cc, w_ada, b_ada.reshape(DEPTH, 1, 3 * D_MODEL))


def _mod_row(mod_ref, tiles_per_batch, mod_row0):
    if tiles_per_batch is None:
        return mod_ref[0:1, :]
    row = mod_row0 + pl.program_id(0) // tiles_per_batch
    return mod_ref[pl.ds(row, 1), :]


def _proj_kernel(*refs, tiles_per_batch, mod_row0, kv_tile):
    x_ref, mod_ref, g_ref, w_ref, wba_ref, alog_ref, dtb_ref, o_ref, bl_ref = refs[:9]
    kv_ref = refs[9] if kv_tile is not None else None
    h_ref = refs[-1]

    @pl.when(pl.program_id(1) == 0)
    def _():
        mod = _mod_row(mod_ref, tiles_per_batch, mod_row0)
        shift = mod[:, 0:D_MODEL]
        scale = mod[:, D_MODEL:2 * D_MODEL]
        x = x_ref[...]
        y = x * lax.rsqrt(jnp.mean(x * x, axis=-1, keepdims=True) + NORM_EPS) * g_ref[...]
        hb = (y * (1.0 + scale) + shift).astype(BF16)
        h_ref[...] = hb
        ba = _dot(hb, wba_ref[...])
        lane = lax.broadcasted_iota(jnp.int32, ba.shape, 1)
        bl_ref[...] = jnp.where(lane < 2 * DN_HEADS, _sigmoid(ba),
                                -jnp.exp(alog_ref[...]) * _softplus(ba + dtb_ref[...]))

    acc = _dot(h_ref[...], w_ref[...])
    o_ref[...] = acc.astype(BF16)
    if kv_tile is not None:
        @pl.when(pl.program_id(1) == kv_tile)
        def _():
            kv_ref[...] = acc


def _project(x2d, mod_l, g_pre, w_main, w_ba, alog_v, dtb_v, tiles_per_batch, mod_row0, want_kv, tm, tn=1024):
    ntok = x2d.shape[0]
    kv_tile = (CB_NA_K * LANE) // tn if want_kv else None
    kern = functools.partial(_proj_kernel, tiles_per_batch=tiles_per_batch, mod_row0=mod_row0, kv_tile=kv_tile)
    vec = pl.BlockSpec((1, LANE), lambda m, n: (0, 0))
    out_specs = [pl.BlockSpec((tm, tn), lambda m, n: (m, n)),
                 pl.BlockSpec((tm, LANE), lambda m, n: (m, 0))]
    out_shape = [jax.ShapeDtypeStruct((ntok, P_MAIN), BF16),
                 jax.ShapeDtypeStruct((ntok, LANE), F32)]
    if want_kv:
        out_specs.append(pl.BlockSpec((tm, tn), lambda m, n: (m, 0)))
        out_shape.append(jax.ShapeDtypeStruct((ntok, tn), F32))
    return pl.pallas_call(
        kern,
        grid=(ntok // tm, P_MAIN // tn),
        in_specs=[
            pl.BlockSpec((tm, D_MODEL), lambda m, n: (m, 0)),
            pl.BlockSpec((8, 3 * D_MODEL), lambda m, n: (0, 0)),
            pl.BlockSpec((1, D_MODEL), lambda m, n: (0, 0)),
            pl.BlockSpec((D_MODEL, tn), lambda m, n: (0, n)),
            pl.BlockSpec((D_MODEL, LANE), lambda m, n: (0, 0)),
            vec, vec,
        ],
        out_specs=out_specs,
        out_shape=out_shape,
        scratch_shapes=[pltpu.VMEM((tm, D_MODEL), BF16)],
        compiler_params=_params(("parallel", "arbitrary")),
        name="in_proj",
    )(x2d, mod_l, g_pre, w_main, w_ba, alog_v, dtb_v)


def _delta_kernel(*refs, T, HB, NU, has_s0, want_state):
    it = iter(refs)
    q_ref, k_ref, v_ref, z_ref, bl_ref = (next(it) for _ in range(5))
    cwq_ref, cwk_ref, cwv_ref, gn_ref = (next(it) for _ in range(4))
    s0_ref = next(it) if has_s0 else None
    y_ref = next(it)
    sfin_ref = next(it) if want_state else None
    pad_ref, qn_ref, kn_ref, vn_ref, aq_ref, bm_ref, oacc_ref, gl_ref = (next(it) for _ in range(8))

    C = DN_CHUNK
    NC = T // C
    RB = min(T, 256)
    UR = DN_UNIT * C
    head0 = pl.program_id(1) * HB

    zeros8 = jnp.zeros((8, LANE), F32)
    pad_ref[0:8, :] = zeros8
    pad_ref[T + 8:T + 16, :] = zeros8

    def conv_into(x_ref, cw_ref, dst_ref, hh, mode):
        cols = slice(hh * LANE, (hh + 1) * LANE)
        for r0 in range(0, T, RB):
            pad_ref[8 + r0:8 + r0 + RB, :] = x_ref[0, r0:r0 + RB, cols].astype(F32)
        w = cw_ref[:, cols]
        for r0 in range(0, T, RB):
            y = (pad_ref[7 + r0:7 + r0 + RB, :] * w[0:1, :]
                 + pad_ref[8 + r0:8 + r0 + RB, :] * w[1:2, :]
                 + pad_ref[9 + r0:9 + r0 + RB, :] * w[2:3, :])
            y = _silu(y)
            if mode != "v":
                y = y * lax.rsqrt(jnp.sum(y * y, axis=-1, keepdims=True) + NORM_EPS)
            if mode == "q":
                y = y * (DN_DK ** -0.5)
            dst_ref[hh, r0:r0 + RB, :] = y

    for hh in range(HB):
        conv_into(q_ref, cwq_ref, qn_ref, hh, "q")
        conv_into(k_ref, cwk_ref, kn_ref, hh, "k")
        conv_into(v_ref, cwv_ref, vn_ref, hh, "v")

    u_r = lax.broadcasted_iota(jnp.int32, (UR, LANE), 0) & (C - 1)
    u_c = lax.broadcasted_iota(jnp.int32, (UR, LANE), 1)
    is_b = u_c >= C
    u_j = u_c & (C - 1)
    dist = jnp.where(is_b, u_j - u_r, u_r - u_j)
    incl_u = dist >= 0
    strict_u = dist > 0
    rowsum_u = dist <= 0
    blk = (jnp.right_shift(u_r, DN_BLK_SHIFT) == jnp.right_shift(u_j, DN_BLK_SHIFT))[0:C, :]
    bd_r = lax.broadcasted_iota(jnp.int32, (UR, UR), 0)
    bd_c = lax.broadcasted_iota(jnp.int32, (UR, UR), 1)
    same_chunk = jnp.right_shift(bd_r, 6) == jnp.right_shift(bd_c, 6)
    bd_lower = (same_chunk & (bd_r >= bd_c)).astype(BF16)
    bd_upper = (same_chunk & (bd_r <= bd_c)).astype(BF16)
    bd_ones = same_chunk.astype(BF16)
    half_l = (lax.broadcasted_iota(jnp.int32, (C, LANE), 1) < C)

    def split3(x):
        hi = x.astype(BF16)
        r1 = x - hi.astype(F32)
        mid = r1.astype(BF16)
        lo = (r1 - mid.astype(F32)).astype(BF16)
        return hi, mid, lo

    def dot01(m01, parts):
        return _dot(m01, parts[0]) + _dot(m01, parts[1]) + _dot(m01, parts[2])

    def block_diag(x):
        return jnp.concatenate([jnp.where(half_l, x, 0.0), jnp.where(half_l, 0.0, x)], axis=0).astype(BF16)

    def mm2s(xs, ys):
        return [_dot(x.astype(BF16), block_diag(y)) for x, y in zip(xs, ys)]

    def tri_inv_minus_eye(lms):
        md = [jnp.where(blk, -lm, 0.0) for lm in lms]
        md2 = mm2s(md, md)
        md4 = mm2s(md2, md2)
        md3 = mm2s(md, md2)
        md8 = mm2s(md4, md4)
        px = [a + b + c for a, b, c in zip(md, md2, md3)]
        md12 = mm2s(md4, md8)
        qx = [a + b + c for a, b, c in zip(md4, md8, md12)]
        pq = mm2s(px, qx)
        dx = [a + b + c for a, b, c in zip(px, qx, pq)]
        loff = [jnp.where(blk, 0.0, lm) for lm in lms]
        dl = mm2s(dx, loff)
        n1 = [a + b for a, b in zip(loff, dl)]
        n2 = mm2s(n1, n1)
        n3 = mm2s(n1, n2)
        rx = [b - a - c for a, b, c in zip(n1, n2, n3)]
        rd = mm2s(rx, dx)
        return [a + b + c for a, b, c in zip(rx, dx, rd)]

    lane_u = u_c

    def pick(x, idx):
        return jnp.broadcast_to(jnp.sum(jnp.where(lane_u == idx, x, 0.0), axis=1, keepdims=True), x.shape)

    zeros_r = jnp.zeros((C, 2 * LANE), BF16)

    def local_group(g, carry):
        units = [(hh, g * NU + s) for hh in range(HB) for s in range(NU)]
        U = []
        for hh, un in units:
            rows = pl.ds(_aligned(un * UR, UR), UR)
            bl = bl_ref[0, rows, :]
            hg = head0 + hh
            U.append(dict(hh=hh, un=un, rows=rows,
                          bb_f=pick(bl, hg), bb_b=pick(bl, DN_HEADS + hg),
                          la_f=pick(bl, 2 * DN_HEADS + hg), la_b=pick(bl, 3 * DN_HEADS + hg)))
        for u in U:
            u["sf"] = split3(u["la_f"])
            u["sb"] = split3(u["la_b"])
            u["sr"] = split3(jnp.where(rowsum_u, jnp.where(is_b, u["la_b"], u["la_f"]), 0.0))
        for u in U:
            u["gc_f"] = dot01(bd_lower, u["sf"])
        for u in U:
            u["gc_b"] = dot01(bd_upper, u["sb"])
        for u in U:
            u["g_row"] = dot01(bd_ones, u["sr"])
        for u in U:
            hh, rows = u["hh"], u["rows"]
            u["k"] = kn_ref[hh, rows, :]
            u["q"] = qn_ref[hh, rows, :]
            u["k16"] = u["k"].astype(BF16)
            u["q16"] = u["q"].astype(BF16)
        chunks = [(u, ci) for u in U for ci in range(DN_UNIT)]
        crs = [slice(ci * C, (ci + 1) * C) for _, ci in chunks]
        grams = [_dot_nt(jnp.concatenate([u["k16"][cr], u["q16"][cr]], axis=0),
                         jnp.concatenate([u["k16"][cr], u["k16"][cr]], axis=0))
                 for (u, _), cr in zip(chunks, crs)]
        for u in U:
            u["decay"] = jnp.exp(jnp.where(incl_u, jnp.where(is_b, u["gc_b"], u["gc_f"]) - u["g_row"], NEG_INF))
            u["bb_p"] = jnp.where(is_b, u["bb_b"], u["bb_f"])
        lms = [jnp.where(strict_u[cr], gm[0:C] * u["bb_p"][cr] * u["decay"][cr], 0.0)
               for (u, _), cr, gm in zip(chunks, crs, grams)]
        attn = [(gm[C:2 * C] * u["decay"][cr]).astype(BF16) for (u, _), cr, gm in zip(chunks, crs, grams)]
        txs = tri_inv_minus_eye(lms)
        for u in U:
            v = vn_ref[u["hh"], u["rows"], :]
            eg_f = jnp.exp(u["gc_f"])
            eg_b = jnp.exp(u["gc_b"])
            kb_f = u["k"] * u["bb_f"]
            kb_b = u["k"] * u["bb_b"]
            u["rhs_f"] = jnp.concatenate([v * u["bb_f"], kb_f * eg_f], axis=1)
            u["rhs_b"] = jnp.concatenate([v * u["bb_b"], kb_b * eg_b], axis=1)
            u["r16_f"] = u["rhs_f"].astype(BF16)
            u["r16_b"] = u["rhs_b"].astype(BF16)
            u["qg_f"] = u["q"] * eg_f
            u["qg_b"] = u["q"] * eg_b
        uws = [_dot(tx.astype(BF16), jnp.concatenate(
                    [jnp.concatenate([u["r16_f"][cr], zeros_r], axis=1),
                     jnp.concatenate([zeros_r, u["r16_b"][cr]], axis=1)], axis=0))
               for (u, _), cr, tx in zip(chunks, crs, txs)]
        wu_f = [(u["rhs_f"][cr] + uw[:, 0:2 * LANE]) for (u, _), cr, uw in zip(chunks, crs, uws)]
        wu_b = [(u["rhs_b"][cr] + uw[:, 2 * LANE:4 * LANE]) for (u, _), cr, uw in zip(chunks, crs, uws)]
        wu16_f = [x.astype(BF16) for x in wu_f]
        wu16_b = [x.astype(BF16) for x in wu_b]
        aws = [_dot(a, jnp.concatenate([jnp.concatenate([xf, zeros_r], axis=1),
                                        jnp.concatenate([zeros_r, xb], axis=1)], axis=0))
               for a, xf, xb in zip(attn, wu16_f, wu16_b)]
        kg_f, kg_b, gls = [], [], []
        for (u, ci), cr in zip(chunks, crs):
            gl_f = u["gc_f"][ci * C + C - 1:ci * C + C, :]
            gl_b = u["gc_b"][ci * C:ci * C + 1, :]
            kg_f.append((u["k"][cr] * jnp.exp(gl_f - u["gc_f"][cr])).astype(BF16))
            kg_b.append((u["k"][cr] * jnp.exp(gl_b - u["gc_b"][cr])).astype(BF16))
            gls.append((jnp.exp(gl_f), jnp.exp(gl_b)))
        kwu_f = [_dot_tn(kg, x) for kg, x in zip(kg_f, wu16_f)]
        kwu_b = [_dot_tn(kg, x) for kg, x in zip(kg_b, wu16_b)]
        for idx, ((u, ci), cr) in enumerate(zip(chunks, crs)):
            hh = u["hh"]
            c = u["un"] * DN_UNIT + ci
            crow = pl.ds(_aligned(c * C, C), C)
            aw = aws[idx]
            for d, kwu, qg, a0 in ((0, kwu_f[idx], u["qg_f"], 0), (1, kwu_b[idx], u["qg_b"], 2 * LANE)):
                aq_ref[hh, d, c, 0:DN_DK, :] = kwu[:, LANE:2 * LANE].astype(BF16)
                aq_ref[hh, d, c, DN_DK:DN_DK + C, :] = (qg[cr] - aw[:, a0 + LANE:a0 + 2 * LANE]).astype(BF16)
                bm_ref[hh, d, c] = kwu[:, 0:LANE]
                oacc_ref[hh, d, crow, :] = aw[:, a0:a0 + LANE]
                gl_ref[hh, d, c] = jnp.broadcast_to(gls[idx][d], (8, LANE))
        return carry

    n_groups = T // (NU * UR)
    if n_groups == 1:
        local_group(0, 0)
    else:
        lax.fori_loop(0, n_groups, local_group, 0)

    def body(i, carry):
        cidx = (i, NC - 1 - i)
        rs = [_dot(aq_ref[hh, d, cidx[d]], carry[2 * hh + d].astype(BF16)) for hh in range(HB) for d in range(2)]
        new = []
        for hh in range(HB):
            for d in range(2):
                r = rs[2 * hh + d]
                c = cidx[d]
                rows = pl.ds(pl.multiple_of(c * C, C), C)
                oacc_ref[hh, d, rows, :] = oacc_ref[hh, d, rows, :] + r[DN_DK:DN_DK + C]
                new.append(carry[2 * hh + d] * gl_ref[hh, d, c, 0:1, :] + bm_ref[hh, d, c] - r[0:DN_DK])
        return tuple(new)

    if has_s0:
        init = tuple(s0_ref[0, d, hh] for hh in range(HB) for d in range(2))
    else:
        init = tuple(jnp.zeros((DN_DK, LANE), F32) for _ in range(2 * HB))
    fin = lax.fori_loop(0, NC, body, init)
    if want_state:
        for hh in range(HB):
            for d in range(2):
                sfin_ref[0, d, hh] = fin[2 * hh + d]

    for hh in range(HB):
        cols = slice(hh * LANE, (hh + 1) * LANE)
        for r0 in range(0, T, RB):
            o = oacc_ref[hh, 0, r0:r0 + RB, :] + oacc_ref[hh, 1, r0:r0 + RB, :]
            o = o * lax.rsqrt(jnp.mean(o * o, axis=-1, keepdims=True) + NORM_EPS) * gn_ref[...]
            y_ref[0, r0:r0 + RB, cols] = (o * _silu(z_ref[0, r0:r0 + RB, cols].astype(F32))).astype(BF16)


def _delta_mixer(proj3, bl3, conv_w, g_norm, s0, want_state, HB, NU):
    B, T, _ = proj3.shape
    has_s0 = s0 is not None
    NC = T // DN_CHUNK
    W = HB * LANE
    kern = functools.partial(_delta_kernel, T=T, HB=HB, NU=NU, has_s0=has_s0, want_state=want_state)
    col = lambda cb: pl.BlockSpec((1, T, W), lambda b, h, cb=cb: (b, 0, cb // HB + h))
    cw = lambda cb: pl.BlockSpec((3, W), lambda b, h, cb=cb: (0, cb // HB + h))
    state_spec = pl.BlockSpec((1, 2, HB, DN_DK, LANE), lambda b, h: (b, 0, h, 0, 0))
    in_specs = [col(CB_DN_Q), col(CB_DN_K), col(CB_DN_V), col(CB_DN_Z),
                pl.BlockSpec((1, T, LANE), lambda b, h: (b, 0, 0)),
                cw(0), cw(4), cw(8), pl.BlockSpec((1, LANE), lambda b, h: (0, 0))]
    args = [proj3, proj3, proj3, proj3, bl3, conv_w, conv_w, conv_w, g_norm]
    if has_s0:
        in_specs.append(state_spec)
        args.append(s0)
    out_specs = [pl.BlockSpec((1, T, W), lambda b, h: (b, 0, h))]
    out_shape = [jax.ShapeDtypeStruct((B, T, DN_WIDTH), BF16)]
    if want_state:
        out_specs.append(state_spec)
        out_shape.append(jax.ShapeDtypeStruct((B, 2, DN_HEADS, DN_DK, LANE), F32))
    res = pl.pallas_call(
        kern,
        grid=(B, DN_HEADS // HB),
        in_specs=in_specs,
        out_specs=out_specs,
        out_shape=out_shape,
        scratch_shapes=[
            pltpu.VMEM((T + 16, LANE), F32),
            pltpu.VMEM((HB, T, LANE), F32), pltpu.VMEM((HB, T, LANE), F32), pltpu.VMEM((HB, T, LANE), F32),
            pltpu.VMEM((HB, 2, NC, DN_DK + DN_CHUNK, LANE), BF16),
            pltpu.VMEM((HB, 2, NC, DN_DK, LANE), F32),
            pltpu.VMEM((HB, 2, T, LANE), F32),
            pltpu.VMEM((HB, 2, NC, 8, LANE), F32),
        ],
        compiler_params=_params(("parallel", "parallel")),
        name="delta_mixer",
    )(*args)
    return (res[0], res[1]) if want_state else (res[0], None)


def _head_masks(rows, dtype):
    lane = lax.broadcasted_iota(jnp.int32, (rows, LANE), 1)
    return (lane < NA_HD).astype(dtype), (lane >= NA_HD).astype(dtype)


def _ctx_attn_kernel(q_ref, k_ref, v_ref, z_ref, y_ref, *, T):
    m0, m1 = _head_masks(T, BF16)
    f0, f1 = _head_masks(T, F32)
    scale = NA_HD ** -0.5
    pairs = range(NA_WIDTH // LANE)
    cols = [slice(p * LANE, (p + 1) * LANE) for p in pairs]
    s = [_dot_nt(q_ref[0, :, cols[p]] * hm, k_ref[0, :, cols[p]]) * scale for p in pairs for hm in (m0, m1)]
    e = [jnp.exp(x - jnp.max(x, axis=-1, keepdims=True)) for x in s]
    o = [_dot(e[2 * p + hh].astype(BF16), v_ref[0, :, cols[p]]) for p in pairs for hh in range(2)]
    o = [x / jnp.sum(ee, axis=-1, keepdims=True) for x, ee in zip(o, e)]
    for p in pairs:
        out = o[2 * p] * f0 + o[2 * p + 1] * f1
        y_ref[0, :, cols[p]] = (out * _silu(z_ref[0, :, cols[p]].astype(F32))).astype(BF16)


def _ctx_attention(proj3):
    B, T, _ = proj3.shape
    wide = NA_WIDTH // LANE
    spec = lambda cb: pl.BlockSpec((1, T, NA_WIDTH), lambda b, cb=cb: (b, 0, cb // wide))
    return pl.pallas_call(
        functools.partial(_ctx_attn_kernel, T=T),
        grid=(B,),
        in_specs=[spec(CB_NA_Q), spec(CB_NA_K), spec(CB_NA_V), spec(CB_NA_Z)],
        out_specs=pl.BlockSpec((1, T, NA_WIDTH), lambda b: (b, 0, 0)),
        out_shape=jax.ShapeDtypeStruct((B, T, NA_WIDTH), BF16),
        compiler_params=_params(("parallel",)),
        name="ctx_attention",
    )(proj3, proj3, proj3, proj3)


def _na_kernel(q_ref, k_ref, v_ref, z_ref, kc_ref, vc_ref, bias_ref, y_ref, sctx_ref, *, T, L):
    W = GRID_W
    rows_total = T // W
    n_win = NA_WIN_R * W
    RS = NA_ROWS_PER_STEP
    scale = NA_HD ** -0.5
    m0, m1 = _head_masks(W, BF16)
    f0, f1 = _head_masks(W, F32)
    kcb = kc_ref[0].astype(BF16)
    vcb = vc_ref[0].astype(BF16)

    RB = 256
    mb0, mb1 = _head_masks(RB, BF16)
    for r0 in range(0, T, RB):
        q = q_ref[0, r0:r0 + RB, :]
        sctx_ref[0, r0:r0 + RB, :] = _dot_nt(q * mb0, kcb) * scale
        sctx_ref[1, r0:r0 + RB, :] = _dot_nt(q * mb1, kcb) * scale

    qc = lax.broadcasted_iota(jnp.int32, (2 * W, n_win), 0) & (W - 1)
    kcol = lax.broadcasted_iota(jnp.int32, (2 * W, n_win), 1) & (W - 1)
    cs = jnp.clip(qc - NA_WIN_C // 2, 0, W - NA_WIN_C)
    col_ok = (kcol >= cs) & (kcol < cs + NA_WIN_C)

    def step(i, carry):
        rr = [i * RS + j for j in range(RS)]
        rs_ = [jnp.clip(r - NA_WIN_R // 2, 0, rows_total - NA_WIN_R) for r in rr]
        qrows = [pl.ds(pl.multiple_of(r * W, W), W) for r in rr]
        krows = [pl.ds(pl.multiple_of(rs * W, W), n_win) for rs in rs_]
        qs = []
        for qr in qrows:
            q = q_ref[0, qr, :]
            qs.append(jnp.concatenate([q * m0, q * m1], axis=0))
        s = [_dot_nt(x, k_ref[0, kr, :]) * scale for x, kr in zip(qs, krows)]
        e1, e2, den = [], [], []
        for j in range(RS):
            dr0 = rs_[j] - rr[j] + NA_WIN_R - 1
            bias = jnp.concatenate(
                [jnp.concatenate([bias_ref[hh, dr0 + 2 * m] for m in range(NA_WIN_R // 2)], axis=1)
                 for hh in range(2)], axis=0)
            sl = jnp.where(col_ok, s[j] + bias, NEG_INF)
            sc = jnp.concatenate([sctx_ref[0, qrows[j], :], sctx_ref[1, qrows[j], :]], axis=0)
            mx = jnp.maximum(jnp.max(sl, axis=-1, keepdims=True), jnp.max(sc, axis=-1, keepdims=True))
            a = jnp.exp(sl - mx)
            b = jnp.exp(sc - mx)
            e1.append(a.astype(BF16))
            e2.append(b.astype(BF16))
            den.append(jnp.sum(a, axis=-1, keepdims=True) + jnp.sum(b, axis=-1, keepdims=True))
        o = [_dot(a, v_ref[0, kr, :]) + _dot(b, vcb) for a, b, kr in zip(e1, e2, krows)]
        for j in range(RS):
            oj = o[j] / den[j]
            out = oj[0:W, :] * f0 + oj[W:2 * W, :] * f1
            y_ref[0, qrows[j], :] = (out * _silu(z_ref[0, qrows[j], :].astype(F32))).astype(BF16)
        return carry

    lax.fori_loop(0, rows_total // RS, step, 0)


def _neighbourhood_attention(proj3, k_ctx, v_ctx, bias2):
    B, T, _ = proj3.shape
    L = k_ctx.shape[1]
    col = lambda cb: pl.BlockSpec((1, T, LANE), lambda b, p, cb=cb: (b, 0, cb + p))
    ctx = pl.BlockSpec((1, L, LANE), lambda b, p: (b, 0, p))
    return pl.pallas_call(
        functools.partial(_na_kernel, T=T, L=L),
        grid=(B, NA_WIDTH // LANE),
        in_specs=[col(CB_NA_Q), col(CB_NA_K), col(CB_NA_V), col(CB_NA_Z), ctx, ctx,
                  pl.BlockSpec((2, 2 * NA_WIN_R - 2, GRID_W, LANE), lambda b, p: (p, 0, 0, 0))],
        out_specs=pl.BlockSpec((1, T, LANE), lambda b, p: (b, 0, p)),
        out_shape=jax.ShapeDtypeStruct((B, T, NA_WIDTH), BF16),
        scratch_shapes=[pltpu.VMEM((2, T, L), F32)],
        compiler_params=_params(("parallel", "parallel")),
        name="nbr_attention",
    )(proj3, proj3, proj3, proj3, k_ctx, v_ctx, bias2)


def _pool_kernel(u_ref, z_ref, pw_ref, ps_ref, y_ref, x_buf, s2_buf, s4_buf, s8_buf, s16_buf, *, T):
    P = POOL_PAD
    E = T + 2 * P
    RB = 256
    bufs = (x_buf, s2_buf, s4_buf, s8_buf, s16_buf)
    for b in bufs:
        zeros = jnp.zeros((P, b.shape[1]), F32)
        b[0:P, :] = zeros
        b[E + P:E + 2 * P, :] = zeros
    zeros = jnp.zeros((P, POOL_WIDTH), F32)
    x_buf[P:2 * P, :] = zeros
    x_buf[T + 2 * P:T + 3 * P, :] = zeros
    for r0 in range(0, T, RB):
        x_buf[2 * P + r0:2 * P + r0 + RB, :] = u_ref[0, r0:r0 + RB, :].astype(F32)

    def level(src, dst, lo, hi, lane0):
        for e0 in range(0, E, RB):
            n = min(RB, E - e0)
            dst[P + e0:P + e0 + n, :] = (src[P + e0 + lo:P + e0 + lo + n, lane0:]
                                         + src[P + e0 + hi:P + e0 + hi + n, lane0:])

    level(x_buf, s2_buf, -1, 0, 0)
    level(s2_buf, s4_buf, -1, 1, LANE)
    level(s4_buf, s8_buf, -2, 2, LANE)
    level(s8_buf, s16_buf, -4, 4, LANE)

    sums = (s2_buf, s4_buf, s8_buf, s16_buf)
    for r0 in range(0, T, RB):
        t = r0 + lax.broadcasted_iota(jnp.int32, (RB, 1), 0)
        for g, win in enumerate(POOL_WINDOWS):
            cols = slice(g * POOL_GC, (g + 1) * POOL_GC)
            lo = jnp.maximum(t - win // 2, 0)
            hi = jnp.minimum(t + win // 2 - 1, T - 1)
            cnt = (hi - lo + 1).astype(F32)
            tot = sums[g][2 * P + r0:2 * P + r0 + RB, 0:POOL_GC]
            pooled = tot / cnt - x_buf[2 * P + r0:2 * P + r0 + RB, cols]
            y = _dot(pooled.astype(BF16), pw_ref[g]) * ps_ref[:, cols]
            y_ref[0, r0:r0 + RB, cols] = (y * _silu(z_ref[0, r0:r0 + RB, cols].astype(F32))).astype(BF16)


def _pool_mixer(proj3, pool_w, pool_scale):
    B, T, _ = proj3.shape
    rows = T + 4 * POOL_PAD
    wide = POOL_WIDTH // LANE
    return pl.pallas_call(
        functools.partial(_pool_kernel, T=T),
        grid=(B,),
        in_specs=[
            pl.BlockSpec((1, T, POOL_WIDTH), lambda b: (b, 0, CB_PL_U // wide)),
            pl.BlockSpec((1, T, POOL_WIDTH), lambda b: (b, 0, CB_PL_Z // wide)),
            pl.BlockSpec((len(POOL_WINDOWS), POOL_GC, POOL_GC), lambda b: (0, 0, 0)),
            pl.BlockSpec((1, POOL_WIDTH), lambda b: (0, 0)),
        ],
        out_specs=pl.BlockSpec((1, T, POOL_WIDTH), lambda b: (b, 0, 0)),
        out_shape=jax.ShapeDtypeStruct((B, T, POOL_WIDTH), BF16),
        scratch_shapes=[
            pltpu.VMEM((rows, 4 * LANE), F32), pltpu.VMEM((rows, 4 * LANE), F32),
            pltpu.VMEM((rows, 3 * LANE), F32), pltpu.VMEM((rows, 2 * LANE), F32),
            pltpu.VMEM((rows, LANE), F32),
        ],
        compiler_params=_params(("parallel",)),
        name="pool_mixer",
    )(proj3, proj3, pool_w, pool_scale)


def _merge_kernel(ydn_ref, yna_ref, ypl_ref, gdn_ref, gna_ref, gpl_ref, x_ref, mod_ref, gpost_ref,
                  wd_ref, wn_ref, wp_ref, wo_ref, o_ref, *, tiles_per_batch, mod_row0):
    merged = (_sigmoid(gdn_ref[...].astype(F32)) * _dot(ydn_ref[...], wd_ref[...])
              + _sigmoid(gna_ref[...].astype(F32)) * _dot(yna_ref[...], wn_ref[...])
              + _sigmoid(gpl_ref[...].astype(F32)) * _dot(ypl_ref[...], wp_ref[...]))
    out = _dot(merged.astype(BF16), wo_ref[...])
    out = out * lax.rsqrt(jnp.mean(out * out, axis=-1, keepdims=True) + NORM_EPS) * gpost_ref[...]
    gate = _mod_row(mod_ref, tiles_per_batch, mod_row0)[:, 2 * D_MODEL:3 * D_MODEL]
    o_ref[...] = x_ref[...] + gate * out


def _merge(y_dn, y_na, y_pl, proj2d, x2d, mod_l, g_post, w_dn, w_na, w_pl, w_out, tiles_per_batch, mod_row0, tm):
    ntok = x2d.shape[0]
    g0 = CB_GATE * LANE // D_MODEL
    br = pl.BlockSpec((tm, DN_WIDTH), lambda m: (m, 0))
    gate = lambda j: pl.BlockSpec((tm, D_MODEL), lambda m, j=j: (m, g0 + j))
    wbr = pl.BlockSpec((DN_WIDTH, D_MODEL), lambda m: (0, 0))
    return pl.pallas_call(
        functools.partial(_merge_kernel, tiles_per_batch=tiles_per_batch, mod_row0=mod_row0),
        grid=(ntok // tm,),
        in_specs=[br, br, br, gate(0), gate(1), gate(2),
                  pl.BlockSpec((tm, D_MODEL), lambda m: (m, 0)),
                  pl.BlockSpec((8, 3 * D_MODEL), lambda m: (0, 0)),
                  pl.BlockSpec((1, D_MODEL), lambda m: (0, 0)),
                  wbr, wbr, wbr,
                  pl.BlockSpec((D_MODEL, D_MODEL), lambda m: (0, 0))],
        out_specs=pl.BlockSpec((tm, D_MODEL), lambda m: (m, 0)),
        out_shape=jax.ShapeDtypeStruct((ntok, D_MODEL), F32),
        compiler_params=_params(("parallel",)),
        name="merge_out",
    )(y_dn, y_na, y_pl, proj2d, proj2d, proj2d, x2d, mod_l, g_post, w_dn, w_na, w_pl, w_out)


def _layer(x3, mod_l, mod_row0, s0, k_ctx, v_ctx, lw, tm_proj, tm_merge, delta_cfg):
    B, T, _ = x3.shape
    n = B * T
    x2d = x3.reshape(n, D_MODEL)
    is_ctx = k_ctx is None
    per_batch = None if is_ctx else True
    res = _project(x2d, mod_l, lw["g_pre"], lw["w_main"], lw["w_ba"], lw["alog_v"], lw["dtb_v"],
                   T // tm_proj if per_batch else None, mod_row0, want_kv=is_ctx, tm=tm_proj)
    proj2d, bl2d = res[0], res[1]
    kv = res[2] if is_ctx else None
    proj3 = proj2d.reshape(B, T, P_MAIN)
    y_dn, s_fin = _delta_mixer(proj3, bl2d.reshape(B, T, LANE), lw["conv_w"], lw["g_norm"], s0,
                               want_state=is_ctx, HB=delta_cfg[0], NU=delta_cfg[1])
    if is_ctx:
        y_na = _ctx_attention(proj3)
    else:
        y_na = _neighbourhood_attention(proj3, k_ctx, v_ctx, lw["bias2"])
    y_pl = _pool_mixer(proj3, lw["pool_w"], lw["pool_scale"])
    out2d = _merge(y_dn.reshape(n, DN_WIDTH), y_na.reshape(n, NA_WIDTH), y_pl.reshape(n, POOL_WIDTH), proj2d,
                   x2d, mod_l, lw["g_post"], lw["w_dn"], lw["w_na"], lw["w_pl"], lw["w_out"],
                   T // tm_merge if per_batch else None, mod_row0, tm=tm_merge)
    return out2d.reshape(B, T, D_MODEL), kv, s_fin


def _lane_vec(v, offset):
    return jnp.zeros((1, LANE), F32).at[0, offset:offset + 2 * DN_HEADS].set(v.reshape(-1))


def kernel(x_prompt, x_sample, c, cache_k_na, cache_v_na, state_dn, c_ctx, w_ada, b_ada, g_pre, g_post, w_in,
           conv_dn, a_log_dn, dt_bias_dn, g_norm_dn, na_bias, pool_w, pool_scale, w_br_dn, w_br_na, w_br_pl,
           w_out):
    B, T, _ = x_prompt.shape
    DB, DT, _ = x_sample.shape
    L = cache_k_na.shape[2]

    cc = jnp.zeros((8, D_MODEL), F32).at[0].set(c_ctx).at[1:1 + DB].set(c)
    mod = _modulation(cc, w_ada, b_ada)

    w_main = jnp.concatenate([w_in[:, :, a:b] for a, b in SRC_COLS], axis=-1).astype(BF16)
    w_ba = jnp.pad(w_in[:, :, OFF_BA_SRC:OFF_BA_SRC + N_BA], ((0, 0), (0, 0), (0, LANE - N_BA))).astype(BF16)
    dcol = np.clip(np.arange(GRID_W)[None, :] - np.arange(GRID_W)[:, None] + NA_WIN_C - 1, 0, 2 * NA_WIN_C - 2)
    bias_t = na_bias[:, :, :, dcol]
    bias2 = jnp.concatenate([bias_t[:, :, :-1], bias_t[:, :, 1:]], axis=-1)

    xp, xs = x_prompt, x_sample
    k_list, v_list, s_list = [], [], []
    for l in range(DEPTH):
        lw = dict(
            g_pre=g_pre[l][None], g_post=g_post[l][None], w_main=w_main[l], w_ba=w_ba[l], conv_w=conv_dn[l],
            alog_v=_lane_vec(a_log_dn[l], 2 * DN_HEADS), dtb_v=_lane_vec(dt_bias_dn[l], 2 * DN_HEADS),
            g_norm=g_norm_dn[l][None], bias2=bias2[l], pool_w=pool_w[l].astype(BF16),
            pool_scale=pool_scale[l][None], w_dn=w_br_dn[l].astype(BF16), w_na=w_br_na[l].astype(BF16),
            w_pl=w_br_pl[l].astype(BF16), w_out=w_out[l].astype(BF16))
        xp, kv, s_fin = _layer(xp, mod[l], 0, None, None, None, lw, tm_proj=1024, tm_merge=512,
                               delta_cfg=(4, 1))
        k_list.append(kv[:, 0:NA_WIDTH].reshape(B, T, NA_HEADS, NA_HD))
        v_list.append(kv[:, NA_WIDTH:2 * NA_WIDTH].reshape(B, T, NA_HEADS, NA_HD))
        s_list.append(s_fin)
        xs, _, _ = _layer(xs, mod[l], 1, state_dn[:, l],
                          cache_k_na[:, l].reshape(DB, L, NA_WIDTH).astype(BF16),
                          cache_v_na[:, l].reshape(DB, L, NA_WIDTH).astype(BF16),
                          lw, tm_proj=1024, tm_merge=512, delta_cfg=(1, 2))
    return (xp, xs, jnp.stack(k_list, axis=1), jnp.stack(v_list, axis=1), jnp.stack(s_list, axis=1))
```

```python
import functools

import numpy as np
import jax
import jax.numpy as jnp
from jax import lax
from jax.experimental import pallas as pl
from jax.experimental.pallas import tpu as pltpu

F32 = jnp.float32
BF16 = jnp.bfloat16

D_MODEL = 1024
DEPTH = 4
GRID_W = 64
NORM_EPS = 1e-6
NEG_INF = -1e30

DN_HEADS = 4
DN_DK = 128
DN_WIDTH = 512
DN_CHUNK = 64
DN_BLK_SHIFT = 4
DN_UNIT = 4

NA_HEADS = 8
NA_HD = 64
NA_WIDTH = 512
NA_WIN_R = 8
NA_WIN_C = 16
NA_ROWS_PER_STEP = 4

POOL_WINDOWS = (2, 4, 8, 16)
POOL_GC = 128
POOL_WIDTH = 512
POOL_PAD = 8

LANE = 128
P_MAIN = 8192
CB_DN_Q, CB_DN_K, CB_DN_V, CB_DN_Z = 0, 4, 8, 12
CB_NA_K, CB_NA_V = 16, 20
CB_NA_Q, CB_NA_Z = 24, 28
CB_PL_U, CB_PL_Z = 32, 36
CB_GATE = 40
SRC_COLS = ((0, 2048), (2576, 3600), (2064, 2576), (3600, 8208))
OFF_BA_SRC = 2048
N_BA = 16

VMEM_LIMIT = 56 * 1024 * 1024


def _sigmoid(x):
    return 1.0 / (1.0 + jnp.exp(-x))


def _silu(x):
    return x * _sigmoid(x)


def _softplus(x):
    return jnp.maximum(x, 0.0) + jnp.log1p(jnp.exp(-jnp.abs(x)))


def _dot(a, b):
    return jnp.dot(a, b, preferred_element_type=F32)


def _dot_nt(a, b):
    return lax.dot_general(a, b, (((1,), (1,)), ((), ())), preferred_element_type=F32)


def _dot_tn(a, b):
    return lax.dot_general(a, b, (((0,), (0,)), ((), ())), preferred_element_type=F32)


def _aligned(x, m):
    return x if isinstance(x, int) else pl.multiple_of(x, m)


def _params(sem):
    return pltpu.CompilerParams(dimension_semantics=sem, vmem_limit_bytes=VMEM_LIMIT)


def _mod_kernel(cc_ref, w_ref, b_ref, o_ref):
    a = _silu(cc_ref[...]).astype(BF16)
    o_ref[0] = _dot(a, w_ref[0].astype(BF16)) + b_ref[0]


def _modulation(cc, w_ada, b_ada):
    tn = 1024
    return pl.pallas_call(
        _mod_kernel,
        grid=(DEPTH, 3 * D_MODEL // tn),
        in_specs=[
            pl.BlockSpec((8, D_MODEL), lambda l, n: (0, 0)),
            pl.BlockSpec((1, D_MODEL, tn), lambda l, n: (l, 0, n)),
            pl.BlockSpec((1, 1, tn), lambda l, n: (l, 0, n)),
        ],
        out_specs=pl.BlockSpec((1, 8, tn), lambda l, n: (l, 0, n)),
        out_shape=jax.ShapeDtypeStruct((DEPTH, 8, 3 * D_MODEL), F32),
        compiler_params=_params(("parallel", "parallel")),
        name="adaln_mod",
    )(cc, w_ada, b_ada.reshape(DEPTH, 1, 3 * D_MODEL))


def _regroup_kernel(w_ref, o_ref, ba_ref):
    dst = 0
    step = 4 * LANE
    for a, b in SRC_COLS:
        for c0 in range(a, b, step):
            n = min(step, b - c0)
            o_ref[0, :, dst:dst + n] = w_ref[0, :, c0:c0 + n].astype(BF16)
            dst += n
    ba = w_ref[0, :, OFF_BA_SRC:OFF_BA_SRC + LANE]
    lane = lax.broadcasted_iota(jnp.int32, ba.shape, 1)
    ba_ref[0] = jnp.where(lane < N_BA, ba, 0.0).astype(BF16)


def _regroup_weights(w_in, rows=256):
    depth, d_in, n_in = w_in.shape
    return pl.pallas_call(
        _regroup_kernel,
        grid=(depth, d_in // rows),
        in_specs=[pl.BlockSpec((1, rows, n_in), lambda l, r: (l, r, 0))],
        out_specs=[pl.BlockSpec((1, rows, P_MAIN), lambda l, r: (l, r, 0)),
                   pl.BlockSpec((1, rows, LANE), lambda l, r: (l, r, 0))],
        out_shape=[jax.ShapeDtypeStruct((depth, d_in, P_MAIN), BF16),
                   jax.ShapeDtypeStruct((depth, d_in, LANE), BF16)],
        compiler_params=_params(("parallel", "parallel")),
        name="regroup_weights",
    )(w_in)


def _mod_row(mod_ref, tiles_per_batch, mod_row0):
    if tiles_per_batch is None:
        return mod_ref[0, 0:1, :]
    row = mod_row0 + pl.program_id(0) // tiles_per_batch
    return mod_ref[0, pl.ds(row, 1), :]


def _proj_kernel(*refs, tiles_per_batch, mod_row0, kv_tile):
    x_ref, mod_ref, g_ref, w_ref, wba_ref, alog_ref, dtb_ref, o_ref, bl_ref = refs[:9]
    k_ref, v_ref = refs[9:11] if kv_tile is not None else (None, None)
    h_ref = refs[-1]

    @pl.when(pl.program_id(1) == 0)
    def _():
        mod = _mod_row(mod_ref, tiles_per_batch, mod_row0)
        shift = mod[:, 0:D_MODEL]
        scale = mod[:, D_MODEL:2 * D_MODEL]
        x = x_ref[...]
        y = x * lax.rsqrt(jnp.mean(x * x, axis=-1, keepdims=True) + NORM_EPS) * g_ref[0]
        hb = (y * (1.0 + scale) + shift).astype(BF16)
        h_ref[...] = hb
        ba = _dot(hb, wba_ref[0])
        lane = lax.broadcasted_iota(jnp.int32, ba.shape, 1)
        bl_ref[...] = jnp.where(lane < 2 * DN_HEADS, _sigmoid(ba),
                                -jnp.exp(alog_ref[0]) * _softplus(ba + dtb_ref[0]))

    acc = _dot(h_ref[...], w_ref[0])
    o_ref[...] = acc.astype(BF16)
    if kv_tile is not None:
        @pl.when(pl.program_id(1) == kv_tile)
        def _():
            k_ref[...] = acc[:, 0:NA_WIDTH]
            v_ref[...] = acc[:, NA_WIDTH:2 * NA_WIDTH]


def _project(x2d, l, mod, g_pre, w_main, w_ba, alog_v, dtb_v, tiles_per_batch, mod_row0, want_kv, tm, tn=1024):
    ntok = x2d.shape[0]
    kv_tile = (CB_NA_K * LANE) // tn if want_kv else None
    kern = functools.partial(_proj_kernel, tiles_per_batch=tiles_per_batch, mod_row0=mod_row0, kv_tile=kv_tile)
    vec = pl.BlockSpec((1, 1, LANE), lambda m, n: (l, 0, 0))
    out_specs = [pl.BlockSpec((tm, tn), lambda m, n: (m, n)),
                 pl.BlockSpec((tm, LANE), lambda m, n: (m, 0))]
    out_shape = [jax.ShapeDtypeStruct((ntok, P_MAIN), BF16),
                 jax.ShapeDtypeStruct((ntok, LANE), F32)]
    if want_kv:
        out_specs += [pl.BlockSpec((tm, NA_WIDTH), lambda m, n: (m, 0))] * 2
        out_shape += [jax.ShapeDtypeStruct((ntok, NA_WIDTH), F32)] * 2
    return pl.pallas_call(
        kern,
        grid=(ntok // tm, P_MAIN // tn),
        in_specs=[
            pl.BlockSpec((tm, D_MODEL), lambda m, n: (m, 0)),
            pl.BlockSpec((1, 8, 3 * D_MODEL), lambda m, n: (l, 0, 0)),
            pl.BlockSpec((1, 1, D_MODEL), lambda m, n: (l, 0, 0)),
            pl.BlockSpec((1, D_MODEL, tn), lambda m, n: (l, 0, n)),
            pl.BlockSpec((1, D_MODEL, LANE), lambda m, n: (l, 0, 0)),
            vec, vec,
        ],
        out_specs=out_specs,
        out_shape=out_shape,
        scratch_shapes=[pltpu.VMEM((tm, D_MODEL), BF16)],
        compiler_params=_params(("parallel", "arbitrary")),
        name="in_proj",
    )(x2d, mod, g_pre, w_main, w_ba, alog_v, dtb_v)


def _delta_kernel(*refs, T, HB, NU, has_s0, want_state):
    it = iter(refs)
    q_ref, k_ref, v_ref, z_ref, bl_ref = (next(it) for _ in range(5))
    cwq_ref, cwk_ref, cwv_ref, gn_ref = (next(it) for _ in range(4))
    s0_ref = next(it) if has_s0 else None
    y_ref = next(it)
    sfin_ref = next(it) if want_state else None
    pad_ref, qn_ref, kn_ref, vn_ref, aq_ref, bm_ref, oacc_ref, gl_ref = (next(it) for _ in range(8))

    C = DN_CHUNK
    NC = T // C
    RB = min(T, 256)
    UR = DN_UNIT * C
    head0 = pl.program_id(1) * HB

    zeros8 = jnp.zeros((8, LANE), F32)
    pad_ref[0:8, :] = zeros8
    pad_ref[T + 8:T + 16, :] = zeros8

    def conv_into(x_ref, cw_ref, dst_ref, hh, mode):
        cols = slice(hh * LANE, (hh + 1) * LANE)
        for r0 in range(0, T, RB):
            pad_ref[8 + r0:8 + r0 + RB, :] = x_ref[0, r0:r0 + RB, cols].astype(F32)
        w = cw_ref[0, :, cols]
        for r0 in range(0, T, RB):
            y = (pad_ref[7 + r0:7 + r0 + RB, :] * w[0:1, :]
                 + pad_ref[8 + r0:8 + r0 + RB, :] * w[1:2, :]
                 + pad_ref[9 + r0:9 + r0 + RB, :] * w[2:3, :])
            y = _silu(y)
            if mode != "v":
                y = y * lax.rsqrt(jnp.sum(y * y, axis=-1, keepdims=True) + NORM_EPS)
            if mode == "q":
                y = y * (DN_DK ** -0.5)
            dst_ref[hh, r0:r0 + RB, :] = y

    for hh in range(HB):
        conv_into(q_ref, cwq_ref, qn_ref, hh, "q")
        conv_into(k_ref, cwk_ref, kn_ref, hh, "k")
        conv_into(v_ref, cwv_ref, vn_ref, hh, "v")

    u_r = lax.broadcasted_iota(jnp.int32, (UR, LANE), 0) & (C - 1)
    u_c = lax.broadcasted_iota(jnp.int32, (UR, LANE), 1)
    is_b = u_c >= C
    u_j = u_c & (C - 1)
    dist = jnp.where(is_b, u_j - u_r, u_r - u_j)
    incl_u = dist >= 0
    strict_u = dist > 0
    rowsum_u = dist <= 0
    blk = (jnp.right_shift(u_r, DN_BLK_SHIFT) == jnp.right_shift(u_j, DN_BLK_SHIFT))[0:C, :]
    bd_r = lax.broadcasted_iota(jnp.int32, (UR, UR), 0)
    bd_c = lax.broadcasted_iota(jnp.int32, (UR, UR), 1)
    same_chunk = jnp.right_shift(bd_r, 6) == jnp.right_shift(bd_c, 6)
    bd_lower = (same_chunk & (bd_r >= bd_c)).astype(BF16)
    bd_upper = (same_chunk & (bd_r <= bd_c)).astype(BF16)
    bd_ones = same_chunk.astype(BF16)
    half_l = (lax.broadcasted_iota(jnp.int32, (C, LANE), 1) < C)

    def split_hi_mid(x):
        hi = x.astype(BF16)
        mid = (x - hi.astype(F32)).astype(BF16)
        return jnp.concatenate([hi, mid], axis=1)

    def dot01(m01, parts):
        r = _dot(m01, parts)
        return r[:, 0:LANE] + r[:, LANE:2 * LANE]

    def block_diag(x):
        return jnp.concatenate([jnp.where(half_l, x, 0.0), jnp.where(half_l, 0.0, x)], axis=0).astype(BF16)

    def mm2s(xs, ys):
        return [_dot(x.astype(BF16), block_diag(y)) for x, y in zip(xs, ys)]

    def tri_inv_minus_eye(lms):
        md = [jnp.where(blk, -lm, 0.0) for lm in lms]
        md2 = mm2s(md, md)
        md4 = mm2s(md2, md2)
        md3 = mm2s(md, md2)
        md8 = mm2s(md4, md4)
        px = [a + b + c for a, b, c in zip(md, md2, md3)]
        md12 = mm2s(md4, md8)
        qx = [a + b + c for a, b, c in zip(md4, md8, md12)]
        pq = mm2s(px, qx)
        dx = [a + b + c for a, b, c in zip(px, qx, pq)]
        loff = [jnp.where(blk, 0.0, lm) for lm in lms]
        dl = mm2s(dx, loff)
        n1 = [a + b for a, b in zip(loff, dl)]
        n2 = mm2s(n1, n1)
        n3 = mm2s(n1, n2)
        rx = [b - a - c for a, b, c in zip(n1, n2, n3)]
        rd = mm2s(rx, dx)
        return [a + b + c for a, b, c in zip(rx, dx, rd)]

    lane_u = u_c

    def pick(x, idx):
        return jnp.broadcast_to(jnp.sum(jnp.where(lane_u == idx, x, 0.0), axis=1, keepdims=True), x.shape)

    zeros_r = jnp.zeros((C, 2 * LANE), BF16)

    def local_group(g, carry):
        units = [(hh, g * NU + s) for hh in range(HB) for s in range(NU)]
        U = []
        for hh, un in units:
            rows = pl.ds(_aligned(un * UR, UR), UR)
            bl = bl_ref[0, rows, :]
            hg = head0 + hh
            U.append(dict(hh=hh, un=un, rows=rows,
                          bb_f=pick(bl, hg), bb_b=pick(bl, DN_HEADS + hg),
                          la_f=pick(bl, 2 * DN_HEADS + hg), la_b=pick(bl, 3 * DN_HEADS + hg)))
        for u in U:
            u["sf"] = split_hi_mid(u["la_f"])
            u["sb"] = split_hi_mid(u["la_b"])
            u["sr"] = split_hi_mid(jnp.where(rowsum_u, jnp.where(is_b, u["la_b"], u["la_f"]), 0.0))
        for u in U:
            u["gc_f"] = dot01(bd_lower, u["sf"])
        for u in U:
            u["gc_b"] = dot01(bd_upper, u["sb"])
        for u in U:
            u["g_row"] = dot01(bd_ones, u["sr"])
        for u in U:
            hh, rows = u["hh"], u["rows"]
            u["k"] = kn_ref[hh, rows, :]
            u["q"] = qn_ref[hh, rows, :]
            u["k16"] = u["k"].astype(BF16)
            u["q16"] = u["q"].astype(BF16)
        chunks = [(u, ci) for u in U for ci in range(DN_UNIT)]
        crs = [slice(ci * C, (ci + 1) * C) for _, ci in chunks]
        grams = [_dot_nt(jnp.concatenate([u["k16"][cr], u["q16"][cr]], axis=0),
                         jnp.concatenate([u["k16"][cr], u["k16"][cr]], axis=0))
                 for (u, _), cr in zip(chunks, crs)]
        for u in U:
            u["decay"] = jnp.exp(jnp.where(incl_u, jnp.where(is_b, u["gc_b"], u["gc_f"]) - u["g_row"], NEG_INF))
            u["bb_p"] = jnp.where(is_b, u["bb_b"], u["bb_f"])
        lms = [jnp.where(strict_u[cr], gm[0:C] * u["bb_p"][cr] * u["decay"][cr], 0.0)
               for (u, _), cr, gm in zip(chunks, crs, grams)]
        attn = [(gm[C:2 * C] * u["decay"][cr]).astype(BF16) for (u, _), cr, gm in zip(chunks, crs, grams)]
        txs = tri_inv_minus_eye(lms)
        for u in U:
            v = vn_ref[u["hh"], u["rows"], :]
            eg_f = jnp.exp(u["gc_f"])
            eg_b = jnp.exp(u["gc_b"])
            kb_f = u["k"] * u["bb_f"]
            kb_b = u["k"] * u["bb_b"]
            u["rhs_f"] = jnp.concatenate([v * u["bb_f"], kb_f * eg_f], axis=1)
            u["rhs_b"] = jnp.concatenate([v * u["bb_b"], kb_b * eg_b], axis=1)
            u["r16_f"] = u["rhs_f"].astype(BF16)
            u["r16_b"] = u["rhs_b"].astype(BF16)
            u["qg_f"] = u["q"] * eg_f
            u["qg_b"] = u["q"] * eg_b
        uws = [_dot(tx.astype(BF16), jnp.concatenate(
                    [jnp.concatenate([u["r16_f"][cr], zeros_r], axis=1),
                     jnp.concatenate([zeros_r, u["r16_b"][cr]], axis=1)], axis=0))
               for (u, _), cr, tx in zip(chunks, crs, txs)]
        wu_f = [(u["rhs_f"][cr] + uw[:, 0:2 * LANE]) for (u, _), cr, uw in zip(chunks, crs, uws)]
        wu_b = [(u["rhs_b"][cr] + uw[:, 2 * LANE:4 * LANE]) for (u, _), cr, uw in zip(chunks, crs, uws)]
        wu16_f = [x.astype(BF16) for x in wu_f]
        wu16_b = [x.astype(BF16) for x in wu_b]
        aws = [_dot(a, jnp.concatenate([jnp.concatenate([xf, zeros_r], axis=1),
                                        jnp.concatenate([zeros_r, xb], axis=1)], axis=0))
               for a, xf, xb in zip(attn, wu16_f, wu16_b)]
        kg_f, kg_b, gls = [], [], []
        for (u, ci), cr in zip(chunks, crs):
            gl_f = u["gc_f"][ci * C + C - 1:ci * C + C, :]
            gl_b = u["gc_b"][ci * C:ci * C + 1, :]
            kg_f.append((u["k"][cr] * jnp.exp(gl_f - u["gc_f"][cr])).astype(BF16))
            kg_b.append((u["k"][cr] * jnp.exp(gl_b - u["gc_b"][cr])).astype(BF16))
            gls.append((jnp.exp(gl_f), jnp.exp(gl_b)))
        kwu_f = [_dot_tn(kg, x) for kg, x in zip(kg_f, wu16_f)]
        kwu_b = [_dot_tn(kg, x) for kg, x in zip(kg_b, wu16_b)]
        for idx, ((u, ci), cr) in enumerate(zip(chunks, crs)):
            hh = u["hh"]
            c = u["un"] * DN_UNIT + ci
            crow = pl.ds(_aligned(c * C, C), C)
            aw = aws[idx]
            for d, kwu, qg, a0 in ((0, kwu_f[idx], u["qg_f"], 0), (1, kwu_b[idx], u["qg_b"], 2 * LANE)):
                aq_ref[hh, d, c, 0:DN_DK, :] = kwu[:, LANE:2 * LANE].astype(BF16)
                aq_ref[hh, d, c, DN_DK:DN_DK + C, :] = (qg[cr] - aw[:, a0 + LANE:a0 + 2 * LANE]).astype(BF16)
                bm_ref[hh, d, c] = kwu[:, 0:LANE]
                oacc_ref[hh, d, crow, :] = aw[:, a0:a0 + LANE]
                gl_ref[hh, d, c] = jnp.broadcast_to(gls[idx][d], (8, LANE))
        return carry

    n_groups = T // (NU * UR)
    if n_groups == 1:
        local_group(0, 0)
    else:
        lax.fori_loop(0, n_groups, local_group, 0)

    def body(i, carry):
        cidx = (i, NC - 1 - i)
        rs = [_dot(aq_ref[hh, d, cidx[d]], carry[2 * hh + d].astype(BF16)) for hh in range(HB) for d in range(2)]
        new = []
        for hh in range(HB):
            for d in range(2):
                r = rs[2 * hh + d]
                c = cidx[d]
                rows = pl.ds(pl.multiple_of(c * C, C), C)
                oacc_ref[hh, d, rows, :] = oacc_ref[hh, d, rows, :] + r[DN_DK:DN_DK + C]
                new.append(carry[2 * hh + d] * gl_ref[hh, d, c, 0:1, :] + bm_ref[hh, d, c] - r[0:DN_DK])
        return tuple(new)

    if has_s0:
        init = tuple(s0_ref[0, 0, d, hh] for hh in range(HB) for d in range(2))
    else:
        init = tuple(jnp.zeros((DN_DK, LANE), F32) for _ in range(2 * HB))
    fin = lax.fori_loop(0, NC, body, init)
    if want_state:
        for hh in range(HB):
            for d in range(2):
                sfin_ref[0, d, hh] = fin[2 * hh + d]

    for hh in range(HB):
        cols = slice(hh * LANE, (hh + 1) * LANE)
        for r0 in range(0, T, RB):
            o = oacc_ref[hh, 0, r0:r0 + RB, :] + oacc_ref[hh, 1, r0:r0 + RB, :]
            o = o * lax.rsqrt(jnp.mean(o * o, axis=-1, keepdims=True) + NORM_EPS) * gn_ref[0]
            y_ref[0, r0:r0 + RB, cols] = (o * _silu(z_ref[0, r0:r0 + RB, cols].astype(F32))).astype(BF16)


def _delta_mixer(proj3, bl3, l, conv_w, g_norm, s0, want_state, HB, NU):
    B, T, _ = proj3.shape
    has_s0 = s0 is not None
    NC = T // DN_CHUNK
    W = HB * LANE
    kern = functools.partial(_delta_kernel, T=T, HB=HB, NU=NU, has_s0=has_s0, want_state=want_state)
    col = lambda cb: pl.BlockSpec((1, T, W), lambda b, h, cb=cb: (b, 0, cb // HB + h))
    cw = lambda cb: pl.BlockSpec((1, 3, W), lambda b, h, cb=cb: (l, 0, cb // HB + h))
    state_spec = pl.BlockSpec((1, 2, HB, DN_DK, LANE), lambda b, h: (b, 0, h, 0, 0))
    in_specs = [col(CB_DN_Q), col(CB_DN_K), col(CB_DN_V), col(CB_DN_Z),
                pl.BlockSpec((1, T, LANE), lambda b, h: (b, 0, 0)),
                cw(0), cw(4), cw(8), pl.BlockSpec((1, 1, LANE), lambda b, h: (l, 0, 0))]
    args = [proj3, proj3, proj3, proj3, bl3, conv_w, conv_w, conv_w, g_norm]
    if has_s0:
        in_specs.append(pl.BlockSpec((1, 1, 2, HB, DN_DK, LANE), lambda b, h: (b, l, 0, h, 0, 0)))
        args.append(s0)
    out_specs = [pl.BlockSpec((1, T, W), lambda b, h: (b, 0, h))]
    out_shape = [jax.ShapeDtypeStruct((B, T, DN_WIDTH), BF16)]
    if want_state:
        out_specs.append(state_spec)
        out_shape.append(jax.ShapeDtypeStruct((B, 2, DN_HEADS, DN_DK, LANE), F32))
    res = pl.pallas_call(
        kern,
        grid=(B, DN_HEADS // HB),
        in_specs=in_specs,
        out_specs=out_specs,
        out_shape=out_shape,
        scratch_shapes=[
            pltpu.VMEM((T + 16, LANE), F32),
            pltpu.VMEM((HB, T, LANE), F32), pltpu.VMEM((HB, T, LANE), F32), pltpu.VMEM((HB, T, LANE), F32),
            pltpu.VMEM((HB, 2, NC, DN_DK + DN_CHUNK, LANE), BF16),
            pltpu.VMEM((HB, 2, NC, DN_DK, LANE), F32),
            pltpu.VMEM((HB, 2, T, LANE), F32),
            pltpu.VMEM((HB, 2, NC, 8, LANE), F32),
        ],
        compiler_params=_params(("parallel", "parallel")),
        name="delta_mixer",
    )(*args)
    return (res[0], res[1]) if want_state else (res[0], None)


def _head_masks(rows, dtype):
    lane = lax.broadcasted_iota(jnp.int32, (rows, LANE), 1)
    return (lane < NA_HD).astype(dtype), (lane >= NA_HD).astype(dtype)


def _ctx_attn_kernel(q_ref, k_ref, v_ref, z_ref, y_ref, *, T):
    m0, m1 = _head_masks(T, BF16)
    f0, f1 = _head_masks(T, F32)
    scale = NA_HD ** -0.5
    pairs = range(NA_WIDTH // LANE)
    cols = [slice(p * LANE, (p + 1) * LANE) for p in pairs]
    s = [_dot_nt(q_ref[0, :, cols[p]] * hm, k_ref[0, :, cols[p]]) * scale for p in pairs for hm in (m0, m1)]
    e = [jnp.exp(x - jnp.max(x, axis=-1, keepdims=True)) for x in s]
    o = [_dot(e[2 * p + hh].astype(BF16), v_ref[0, :, cols[p]]) for p in pairs for hh in range(2)]
    o = [x / jnp.sum(ee, axis=-1, keepdims=True) for x, ee in zip(o, e)]
    for p in pairs:
        out = o[2 * p] * f0 + o[2 * p + 1] * f1
        y_ref[0, :, cols[p]] = (out * _silu(z_ref[0, :, cols[p]].astype(F32))).astype(BF16)


def _ctx_attention(proj3):
    B, T, _ = proj3.shape
    wide = NA_WIDTH // LANE
    spec = lambda cb: pl.BlockSpec((1, T, NA_WIDTH), lambda b, cb=cb: (b, 0, cb // wide))
    return pl.pallas_call(
        functools.partial(_ctx_attn_kernel, T=T),
        grid=(B,),
        in_specs=[spec(CB_NA_Q), spec(CB_NA_K), spec(CB_NA_V), spec(CB_NA_Z)],
        out_specs=pl.BlockSpec((1, T, NA_WIDTH), lambda b: (b, 0, 0)),
        out_shape=jax.ShapeDtypeStruct((B, T, NA_WIDTH), BF16),
        compiler_params=_params(("parallel",)),
        name="ctx_attention",
    )(proj3, proj3, proj3, proj3)


def _na_kernel(q_ref, k_ref, v_ref, z_ref, kc_ref, vc_ref, bias_ref, y_ref, sctx_ref, *, T, L):
    W = GRID_W
    rows_total = T // W
    n_win = NA_WIN_R * W
    RS = NA_ROWS_PER_STEP
    scale = NA_HD ** -0.5
    m0, m1 = _head_masks(W, BF16)
    f0, f1 = _head_masks(W, F32)
    kcb = kc_ref[0, 0].astype(BF16)
    vcb = vc_ref[0, 0].astype(BF16)

    RB = 256
    mb0, mb1 = _head_masks(RB, BF16)
    for r0 in range(0, T, RB):
        q = q_ref[0, r0:r0 + RB, :]
        sctx_ref[0, r0:r0 + RB, :] = _dot_nt(q * mb0, kcb) * scale
        sctx_ref[1, r0:r0 + RB, :] = _dot_nt(q * mb1, kcb) * scale

    qc = lax.broadcasted_iota(jnp.int32, (2 * W, n_win), 0) & (W - 1)
    kcol = lax.broadcasted_iota(jnp.int32, (2 * W, n_win), 1) & (W - 1)
    cs = jnp.clip(qc - NA_WIN_C // 2, 0, W - NA_WIN_C)
    col_ok = (kcol >= cs) & (kcol < cs + NA_WIN_C)

    def step(i, carry):
        rr = [i * RS + j for j in range(RS)]
        rs_ = [jnp.clip(r - NA_WIN_R // 2, 0, rows_total - NA_WIN_R) for r in rr]
        qrows = [pl.ds(pl.multiple_of(r * W, W), W) for r in rr]
        krows = [pl.ds(pl.multiple_of(rs * W, W), n_win) for rs in rs_]
        qs = []
        for qr in qrows:
            q = q_ref[0, qr, :]
            qs.append(jnp.concatenate([q * m0, q * m1], axis=0))
        s = [_dot_nt(x, k_ref[0, kr, :]) * scale for x, kr in zip(qs, krows)]
        e1, e2, den = [], [], []
        for j in range(RS):
            dr0 = rs_[j] - rr[j] + NA_WIN_R - 1
            bias = jnp.concatenate(
                [jnp.concatenate([bias_ref[0, hh, dr0 + 2 * m] for m in range(NA_WIN_R // 2)], axis=1)
                 for hh in range(2)], axis=0)
            sl = jnp.where(col_ok, s[j] + bias, NEG_INF)
            sc = jnp.concatenate([sctx_ref[0, qrows[j], :], sctx_ref[1, qrows[j], :]], axis=0)
            mx = jnp.maximum(jnp.max(sl, axis=-1, keepdims=True), jnp.max(sc, axis=-1, keepdims=True))
            a = jnp.exp(sl - mx)
            b = jnp.exp(sc - mx)
            e1.append(a.astype(BF16))
            e2.append(b.astype(BF16))
            den.append(jnp.sum(a, axis=-1, keepdims=True) + jnp.sum(b, axis=-1, keepdims=True))
        o = [_dot(a, v_ref[0, kr, :]) + _dot(b, vcb) for a, b, kr in zip(e1, e2, krows)]
        for j in range(RS):
            oj = o[j] / den[j]
            out = oj[0:W, :] * f0 + oj[W:2 * W, :] * f1
            y_ref[0, qrows[j], :] = (out * _silu(z_ref[0, qrows[j], :].astype(F32))).astype(BF16)
        return carry

    lax.fori_loop(0, rows_total // RS, step, 0)


def _neighbourhood_attention(proj3, l, k_ctx, v_ctx, bias2):
    B, T, _ = proj3.shape
    L = k_ctx.shape[2]
    col = lambda cb: pl.BlockSpec((1, T, LANE), lambda b, p, cb=cb: (b, 0, cb + p))
    ctx = pl.BlockSpec((1, 1, L, LANE), lambda b, p: (b, l, 0, p))
    return pl.pallas_call(
        functools.partial(_na_kernel, T=T, L=L),
        grid=(B, NA_WIDTH // LANE),
        in_specs=[col(CB_NA_Q), col(CB_NA_K), col(CB_NA_V), col(CB_NA_Z), ctx, ctx,
                  pl.BlockSpec((1, 2, 2 * NA_WIN_R - 2, GRID_W, LANE), lambda b, p: (l, p, 0, 0, 0))],
        out_specs=pl.BlockSpec((1, T, LANE), lambda b, p: (b, 0, p)),
        out_shape=jax.ShapeDtypeStruct((B, T, NA_WIDTH), BF16),
        scratch_shapes=[pltpu.VMEM((2, T, L), F32)],
        compiler_params=_params(("parallel", "parallel")),
        name="nbr_attention",
    )(proj3, proj3, proj3, proj3, k_ctx, v_ctx, bias2)


def _pool_kernel(u_ref, z_ref, pw_ref, ps_ref, y_ref, x_buf, s2_buf, s4_buf, s8_buf, s16_buf, *, T):
    P = POOL_PAD
    E = T + 2 * P
    RB = 256
    bufs = (x_buf, s2_buf, s4_buf, s8_buf, s16_buf)
    for b in bufs:
        zeros = jnp.zeros((P, b.shape[1]), F32)
        b[0:P, :] = zeros
        b[E + P:E + 2 * P, :] = zeros
    zeros = jnp.zeros((P, POOL_WIDTH), F32)
    x_buf[P:2 * P, :] = zeros
    x_buf[T + 2 * P:T + 3 * P, :] = zeros
    for r0 in range(0, T, RB):
        x_buf[2 * P + r0:2 * P + r0 + RB, :] = u_ref[0, r0:r0 + RB, :].astype(F32)

    def level(src, dst, lo, hi, lane0):
        for e0 in range(0, E, RB):
            n = min(RB, E - e0)
            dst[P + e0:P + e0 + n, :] = (src[P + e0 + lo:P + e0 + lo + n, lane0:]
                                         + src[P + e0 + hi:P + e0 + hi + n, lane0:])

    level(x_buf, s2_buf, -1, 0, 0)
    level(s2_buf, s4_buf, -1, 1, LANE)
    level(s4_buf, s8_buf, -2, 2, LANE)
    level(s8_buf, s16_buf, -4, 4, LANE)

    sums = (s2_buf, s4_buf, s8_buf, s16_buf)
    for r0 in range(0, T, RB):
        t = r0 + lax.broadcasted_iota(jnp.int32, (RB, 1), 0)
        for g, win in enumerate(POOL_WINDOWS):
            cols = slice(g * POOL_GC, (g + 1) * POOL_GC)
            lo = jnp.maximum(t - win // 2, 0)
            hi = jnp.minimum(t + win // 2 - 1, T - 1)
            cnt = (hi - lo + 1).astype(F32)
            tot = sums[g][2 * P + r0:2 * P + r0 + RB, 0:POOL_GC]
            pooled = tot / cnt - x_buf[2 * P + r0:2 * P + r0 + RB, cols]
            y = _dot(pooled.astype(BF16), pw_ref[0, g]) * ps_ref[0, :, cols]
            y_ref[0, r0:r0 + RB, cols] = (y * _silu(z_ref[0, r0:r0 + RB, cols].astype(F32))).astype(BF16)


def _pool_mixer(proj3, l, pool_w, pool_scale):
    B, T, _ = proj3.shape
    rows = T + 4 * POOL_PAD
    wide = POOL_WIDTH // LANE
    return pl.pallas_call(
        functools.partial(_pool_kernel, T=T),
        grid=(B,),
        in_specs=[
            pl.BlockSpec((1, T, POOL_WIDTH), lambda b: (b, 0, CB_PL_U // wide)),
            pl.BlockSpec((1, T, POOL_WIDTH), lambda b: (b, 0, CB_PL_Z // wide)),
            pl.BlockSpec((1, len(POOL_WINDOWS), POOL_GC, POOL_GC), lambda b: (l, 0, 0, 0)),
            pl.BlockSpec((1, 1, POOL_WIDTH), lambda b: (l, 0, 0)),
        ],
        out_specs=pl.BlockSpec((1, T, POOL_WIDTH), lambda b: (b, 0, 0)),
        out_shape=jax.ShapeDtypeStruct((B, T, POOL_WIDTH), BF16),
        scratch_shapes=[
            pltpu.VMEM((rows, 4 * LANE), F32), pltpu.VMEM((rows, 4 * LANE), F32),
            pltpu.VMEM((rows, 3 * LANE), F32), pltpu.VMEM((rows, 2 * LANE), F32),
            pltpu.VMEM((rows, LANE), F32),
        ],
        compiler_params=_params(("parallel",)),
        name="pool_mixer",
    )(proj3, proj3, pool_w, pool_scale)


def _merge_kernel(ydn_ref, yna_ref, ypl_ref, gdn_ref, gna_ref, gpl_ref, x_ref, mod_ref, gpost_ref,
                  wd_ref, wn_ref, wp_ref, wo_ref, o_ref, *, tiles_per_batch, mod_row0):
    merged = (_sigmoid(gdn_ref[...].astype(F32)) * _dot(ydn_ref[...], wd_ref[0])
              + _sigmoid(gna_ref[...].astype(F32)) * _dot(yna_ref[...], wn_ref[0])
              + _sigmoid(gpl_ref[...].astype(F32)) * _dot(ypl_ref[...], wp_ref[0]))
    out = _dot(merged.astype(BF16), wo_ref[0])
    out = out * lax.rsqrt(jnp.mean(out * out, axis=-1, keepdims=True) + NORM_EPS) * gpost_ref[0]
    gate = _mod_row(mod_ref, tiles_per_batch, mod_row0)[:, 2 * D_MODEL:3 * D_MODEL]
    o_ref[...] = x_ref[...] + gate * out


def _merge(y_dn, y_na, y_pl, proj2d, x2d, l, mod, g_post, w_dn, w_na, w_pl, w_out, tiles_per_batch, mod_row0, tm):
    ntok = x2d.shape[0]
    g0 = CB_GATE * LANE // D_MODEL
    br = pl.BlockSpec((tm, DN_WIDTH), lambda m: (m, 0))
    gate = lambda j: pl.BlockSpec((tm, D_MODEL), lambda m, j=j: (m, g0 + j))
    wbr = pl.BlockSpec((1, DN_WIDTH, D_MODEL), lambda m: (l, 0, 0))
    return pl.pallas_call(
        functools.partial(_merge_kernel, tiles_per_batch=tiles_per_batch, mod_row0=mod_row0),
        grid=(ntok // tm,),
        in_specs=[br, br, br, gate(0), gate(1), gate(2),
                  pl.BlockSpec((tm, D_MODEL), lambda m: (m, 0)),
                  pl.BlockSpec((1, 8, 3 * D_MODEL), lambda m: (l, 0, 0)),
                  pl.BlockSpec((1, 1, D_MODEL), lambda m: (l, 0, 0)),
                  wbr, wbr, wbr,
                  pl.BlockSpec((1, D_MODEL, D_MODEL), lambda m: (l, 0, 0))],
        out_specs=pl.BlockSpec((tm, D_MODEL), lambda m: (m, 0)),
        out_shape=jax.ShapeDtypeStruct((ntok, D_MODEL), F32),
        compiler_params=_params(("parallel",)),
        name="merge_out",
    )(y_dn, y_na, y_pl, proj2d, proj2d, proj2d, x2d, mod, g_post, w_dn, w_na, w_pl, w_out)


def _layer(x3, l, mod_row0, s0, k_ctx, v_ctx, pw, tm_proj, tm_merge, delta_cfg):
    B, T, _ = x3.shape
    n = B * T
    x2d = x3.reshape(n, D_MODEL)
    is_ctx = k_ctx is None
    res = _project(x2d, l, pw["mod"], pw["g_pre"], pw["w_main"], pw["w_ba"], pw["alog_v"], pw["dtb_v"],
                   None if is_ctx else T // tm_proj, mod_row0, want_kv=is_ctx, tm=tm_proj)
    proj2d, bl2d = res[0], res[1]
    proj3 = proj2d.reshape(B, T, P_MAIN)
    y_dn, s_fin = _delta_mixer(proj3, bl2d.reshape(B, T, LANE), l, pw["conv_w"], pw["g_norm"], s0,
                               want_state=is_ctx, HB=delta_cfg[0], NU=delta_cfg[1])
    if is_ctx:
        y_na = _ctx_attention(proj3)
    else:
        y_na = _neighbourhood_attention(proj3, l, k_ctx, v_ctx, pw["bias2"])
    y_pl = _pool_mixer(proj3, l, pw["pool_w"], pw["pool_scale"])
    out2d = _merge(y_dn.reshape(n, DN_WIDTH), y_na.reshape(n, NA_WIDTH), y_pl.reshape(n, POOL_WIDTH), proj2d,
                   x2d, l, pw["mod"], pw["g_post"], pw["w_dn"], pw["w_na"], pw["w_pl"], pw["w_out"],
                   None if is_ctx else T // tm_merge, mod_row0, tm=tm_merge)
    return out2d.reshape(B, T, D_MODEL), res[2:], s_fin


def _lane_vecs(v, offset):
    return jnp.zeros((DEPTH, 1, LANE), F32).at[:, 0, offset:offset + 2 * DN_HEADS].set(v.reshape(DEPTH, -1))


def _bias_windows(na_bias):
    r = jnp.zeros(na_bias.shape[:-1] + (2 * GRID_W - 1,), F32)
    r = r.at[..., GRID_W - NA_WIN_C:GRID_W + NA_WIN_C - 1].set(na_bias)
    skew = jnp.tile(r, (1, 1, 1, GRID_W + 1))[..., :GRID_W * 2 * GRID_W]
    skew = skew.reshape(na_bias.shape[:-1] + (GRID_W, 2 * GRID_W))[..., :GRID_W]
    bias_t = skew[..., ::-1, :]
    return jnp.concatenate([bias_t[:, :, :-1], bias_t[:, :, 1:]], axis=-1)


def kernel(x_prompt, x_sample, c, cache_k_na, cache_v_na, state_dn, c_ctx, w_ada, b_ada, g_pre, g_post, w_in,
           conv_dn, a_log_dn, dt_bias_dn, g_norm_dn, na_bias, pool_w, pool_scale, w_br_dn, w_br_na, w_br_pl,
           w_out):
    B, T, _ = x_prompt.shape
    DB, DT, _ = x_sample.shape
    L = cache_k_na.shape[2]

    cc = jnp.zeros((8, D_MODEL), F32).at[0].set(c_ctx).at[1:1 + DB].set(c)
    w_main, w_ba = _regroup_weights(w_in)
    pw = dict(
        mod=_modulation(cc, w_ada, b_ada),
        g_pre=g_pre[:, None], g_post=g_post[:, None], g_norm=g_norm_dn[:, None], pool_scale=pool_scale[:, None],
        w_main=w_main, w_ba=w_ba,
        alog_v=_lane_vecs(a_log_dn, 2 * DN_HEADS), dtb_v=_lane_vecs(dt_bias_dn, 2 * DN_HEADS),
        conv_w=conv_dn, bias2=_bias_windows(na_bias), pool_w=pool_w.astype(BF16),
        w_dn=w_br_dn.astype(BF16), w_na=w_br_na.astype(BF16), w_pl=w_br_pl.astype(BF16), w_out=w_out.astype(BF16))
    k_ctx = cache_k_na.reshape(DB, DEPTH, L, NA_WIDTH)
    v_ctx = cache_v_na.reshape(DB, DEPTH, L, NA_WIDTH)

    xp, xs = x_prompt, x_sample
    k_list, v_list, s_list = [], [], []
    for l in range(DEPTH):
        xp, (k_l, v_l), s_fin = _layer(xp, l, 0, None, None, None, pw, tm_proj=1024, tm_merge=512,
                                       delta_cfg=(4, 1))
        k_list.append(k_l.reshape(B, T, NA_HEADS, NA_HD))
        v_list.append(v_l.reshape(B, T, NA_HEADS, NA_HD))
        s_list.append(s_fin)
        xs, _, _ = _layer(xs, l, 1, state_dn, k_ctx, v_ctx, pw, tm_proj=1024, tm_merge=512, delta_cfg=(2, 2))
    return (xp, xs, jnp.stack(k_list, axis=1), jnp.stack(v_list, axis=1), jnp.stack(s_list, axis=1))
```

```python
import functools

import numpy as np
import jax
import jax.numpy as jnp
from jax import lax
from jax.experimental import pallas as pl
from jax.experimental.pallas import tpu as pltpu

F32 = jnp.float32
BF16 = jnp.bfloat16

D_MODEL = 1024
DEPTH = 4
GRID_W = 64
NORM_EPS = 1e-6
NEG_INF = -1e30

DN_HEADS = 4
DN_DK = 128
DN_WIDTH = 512
DN_CHUNK = 64
DN_BLK_SHIFT = 4
DN_UNIT = 4

NA_HEADS = 8
NA_HD = 64
NA_WIDTH = 512
NA_WIN_R = 8
NA_WIN_C = 16
NA_ROWS_PER_STEP = 4

POOL_WINDOWS = (2, 4, 8, 16)
POOL_GC = 128
POOL_WIDTH = 512
POOL_PAD = 8

LANE = 128
P_MAIN = 8192
CB_DN_Q, CB_DN_K, CB_DN_V, CB_DN_Z = 0, 4, 8, 12
CB_NA_K, CB_NA_V = 16, 20
CB_NA_Q, CB_NA_Z = 24, 28
CB_PL_U, CB_PL_Z = 32, 36
CB_GATE = 40
SRC_COLS = ((0, 2048), (2576, 3600), (2064, 2576), (3600, 8208))
OFF_BA_SRC = 2048
N_BA = 16

VMEM_LIMIT = 56 * 1024 * 1024


def _sigmoid(x):
    return 1.0 / (1.0 + jnp.exp(-x))


def _silu(x):
    return x * _sigmoid(x)


def _softplus(x):
    return jnp.maximum(x, 0.0) + jnp.log1p(jnp.exp(-jnp.abs(x)))


def _dot(a, b):
    return jnp.dot(a, b, preferred_element_type=F32)


def _dot_nt(a, b):
    return lax.dot_general(a, b, (((1,), (1,)), ((), ())), preferred_element_type=F32)


def _dot_tn(a, b):
    return lax.dot_general(a, b, (((0,), (0,)), ((), ())), preferred_element_type=F32)


def _aligned(x, m):
    return x if isinstance(x, int) else pl.multiple_of(x, m)


def _params(sem):
    return pltpu.CompilerParams(dimension_semantics=sem, vmem_limit_bytes=VMEM_LIMIT)


def _mod_kernel(cc_ref, w_ref, b_ref, o_ref):
    a = _silu(cc_ref[...]).astype(BF16)
    o_ref[0] = _dot(a, w_ref[0].astype(BF16)) + b_ref[0]


def _modulation(cc, w_ada, b_ada):
    tn = 1024
    return pl.pallas_call(
        _mod_kernel,
        grid=(DEPTH, 3 * D_MODEL // tn),
        in_specs=[
            pl.BlockSpec((8, D_MODEL), lambda l, n: (0, 0)),
            pl.BlockSpec((1, D_MODEL, tn), lambda l, n: (l, 0, n)),
            pl.BlockSpec((1, 1, tn), lambda l, n: (l, 0, n)),
        ],
        out_specs=pl.BlockSpec((1, 8, tn), lambda l, n: (l, 0, n)),
        out_shape=jax.ShapeDtypeStruct((DEPTH, 8, 3 * D_MODEL), F32),
        compiler_params=_params(("parallel", "parallel")),
        name="adaln_mod",
    )(cc, w_ada, b_ada.reshape(DEPTH, 1, 3 * D_MODEL))


def _regroup_kernel(w_ref, o_ref, ba_ref):
    dst = 0
    step = 4 * LANE
    for a, b in SRC_COLS:
        for c0 in range(a, b, step):
            n = min(step, b - c0)
            o_ref[0, :, dst:dst + n] = w_ref[0, :, c0:c0 + n].astype(BF16)
            dst += n
    ba = w_ref[0, :, OFF_BA_SRC:OFF_BA_SRC + LANE]
    lane = lax.broadcasted_iota(jnp.int32, ba.shape, 1)
    ba_ref[0] = jnp.where(lane < N_BA, ba, 0.0).astype(BF16)


def _regroup_weights(w_in, rows=256):
    depth, d_in, n_in = w_in.shape
    return pl.pallas_call(
        _regroup_kernel,
        grid=(depth, d_in // rows),
        in_specs=[pl.BlockSpec((1, rows, n_in), lambda l, r: (l, r, 0))],
        out_specs=[pl.BlockSpec((1, rows, P_MAIN), lambda l, r: (l, r, 0)),
                   pl.BlockSpec((1, rows, LANE), lambda l, r: (l, r, 0))],
        out_shape=[jax.ShapeDtypeStruct((depth, d_in, P_MAIN), BF16),
                   jax.ShapeDtypeStruct((depth, d_in, LANE), BF16)],
        compiler_params=_params(("parallel", "parallel")),
        name="regroup_weights",
    )(w_in)


def _mod_row(mod_ref, tiles_per_batch, mod_row0):
    if tiles_per_batch is None:
        return mod_ref[0, 0:1, :]
    row = mod_row0 + pl.program_id(0) // tiles_per_batch
    return mod_ref[0, pl.ds(row, 1), :]


def _proj_kernel(*refs, tiles_per_batch, mod_row0, kv_tile, n_alias):
    x_ref, mod_ref, g_ref, w_ref, wba_ref, alog_ref, dtb_ref = refs[:7]
    o_ref, bl_ref = refs[7 + n_alias:9 + n_alias]
    k_ref, v_ref = refs[9 + n_alias:11 + n_alias] if kv_tile is not None else (None, None)
    h_ref = refs[-1]

    @pl.when(pl.program_id(1) == 0)
    def _():
        mod = _mod_row(mod_ref, tiles_per_batch, mod_row0)
        shift = mod[:, 0:D_MODEL]
        scale = mod[:, D_MODEL:2 * D_MODEL]
        x = x_ref[...]
        y = x * lax.rsqrt(jnp.mean(x * x, axis=-1, keepdims=True) + NORM_EPS) * g_ref[0]
        hb = (y * (1.0 + scale) + shift).astype(BF16)
        h_ref[...] = hb
        ba = _dot(hb, wba_ref[0])
        lane = lax.broadcasted_iota(jnp.int32, ba.shape, 1)
        bl_ref[...] = jnp.where(lane < 2 * DN_HEADS, _sigmoid(ba),
                                -jnp.exp(alog_ref[0]) * _softplus(ba + dtb_ref[0]))

    acc = _dot(h_ref[...], w_ref[0])
    o_ref[...] = acc.astype(BF16)
    if kv_tile is not None:
        @pl.when(pl.program_id(1) == kv_tile)
        def _():
            seqs, _, t_len, _ = k_ref.shape
            k_ref[:, 0] = acc[:, 0:NA_WIDTH].reshape(seqs, t_len, NA_WIDTH)
            v_ref[:, 0] = acc[:, NA_WIDTH:2 * NA_WIDTH].reshape(seqs, t_len, NA_WIDTH)


def _project(x2d, l, mod, g_pre, w_main, w_ba, alog_v, dtb_v, tiles_per_batch, mod_row0, kv, tm, tn=1024):
    ntok = x2d.shape[0]
    kv_tile = (CB_NA_K * LANE) // tn if kv is not None else None
    prev = [] if kv is None or not hasattr(kv[0], "shape") else list(kv)
    kern = functools.partial(_proj_kernel, tiles_per_batch=tiles_per_batch, mod_row0=mod_row0, kv_tile=kv_tile,
                             n_alias=len(prev))
    vec = pl.BlockSpec((1, 1, LANE), lambda m, n: (l, 0, 0))
    in_specs = [
        pl.BlockSpec((tm, D_MODEL), lambda m, n: (m, 0)),
        pl.BlockSpec((1, 8, 3 * D_MODEL), lambda m, n: (l, 0, 0)),
        pl.BlockSpec((1, 1, D_MODEL), lambda m, n: (l, 0, 0)),
        pl.BlockSpec((1, D_MODEL, tn), lambda m, n: (l, 0, n)),
        pl.BlockSpec((1, D_MODEL, LANE), lambda m, n: (l, 0, 0)),
        vec, vec,
    ] + [pl.BlockSpec(memory_space=pl.ANY)] * len(prev)
    out_specs = [pl.BlockSpec((tm, tn), lambda m, n: (m, n)),
                 pl.BlockSpec((tm, LANE), lambda m, n: (m, 0))]
    out_shape = [jax.ShapeDtypeStruct((ntok, P_MAIN), BF16),
                 jax.ShapeDtypeStruct((ntok, LANE), F32)]
    aliases = {}
    if kv is not None:
        nb, t_len = (prev[0].shape[0], prev[0].shape[2]) if prev else kv
        out_specs += [pl.BlockSpec((tm // t_len, 1, t_len, NA_WIDTH), lambda m, n: (m, l, 0, 0))] * 2
        out_shape += [jax.ShapeDtypeStruct((nb, DEPTH, t_len, NA_WIDTH), F32)] * 2
        aliases = {7 + i: 2 + i for i in range(len(prev))}
    return pl.pallas_call(
        kern,
        grid=(ntok // tm, P_MAIN // tn),
        in_specs=in_specs,
        out_specs=out_specs,
        out_shape=out_shape,
        scratch_shapes=[pltpu.VMEM((tm, D_MODEL), BF16)],
        input_output_aliases=aliases,
        compiler_params=_params(("parallel", "arbitrary")),
        name="in_proj",
    )(x2d, mod, g_pre, w_main, w_ba, alog_v, dtb_v, *prev)


def _delta_kernel(*refs, T, HB, NU, has_s0, want_state, alias_state):
    it = iter(refs)
    q_ref, k_ref, v_ref, z_ref, bl_ref = (next(it) for _ in range(5))
    cwq_ref, cwk_ref, cwv_ref, gn_ref = (next(it) for _ in range(4))
    s0_ref = next(it) if has_s0 else None
    if alias_state:
        next(it)
    y_ref = next(it)
    sfin_ref = next(it) if want_state else None
    pad_ref, qn_ref, kn_ref, vn_ref, aq_ref, bm_ref, oacc_ref, gl_ref = (next(it) for _ in range(8))

    C = DN_CHUNK
    NC = T // C
    RB = min(T, 256)
    UR = DN_UNIT * C
    head0 = pl.program_id(1) * HB

    zeros8 = jnp.zeros((8, LANE), F32)
    pad_ref[0:8, :] = zeros8
    pad_ref[T + 8:T + 16, :] = zeros8

    def conv_into(x_ref, cw_ref, dst_ref, hh, mode):
        cols = slice(hh * LANE, (hh + 1) * LANE)
        for r0 in range(0, T, RB):
            pad_ref[8 + r0:8 + r0 + RB, :] = x_ref[0, r0:r0 + RB, cols].astype(F32)
        w = cw_ref[0, :, cols]
        for r0 in range(0, T, RB):
            y = (pad_ref[7 + r0:7 + r0 + RB, :] * w[0:1, :]
                 + pad_ref[8 + r0:8 + r0 + RB, :] * w[1:2, :]
                 + pad_ref[9 + r0:9 + r0 + RB, :] * w[2:3, :])
            y = _silu(y)
            if mode != "v":
                y = y * lax.rsqrt(jnp.sum(y * y, axis=-1, keepdims=True) + NORM_EPS)
            if mode == "q":
                y = y * (DN_DK ** -0.5)
            dst_ref[hh, r0:r0 + RB, :] = y

    for hh in range(HB):
        conv_into(q_ref, cwq_ref, qn_ref, hh, "q")
        conv_into(k_ref, cwk_ref, kn_ref, hh, "k")
        conv_into(v_ref, cwv_ref, vn_ref, hh, "v")

    u_r = lax.broadcasted_iota(jnp.int32, (UR, LANE), 0) & (C - 1)
    u_c = lax.broadcasted_iota(jnp.int32, (UR, LANE), 1)
    is_b = u_c >= C
    u_j = u_c & (C - 1)
    dist = jnp.where(is_b, u_j - u_r, u_r - u_j)
    incl_u = dist >= 0
    strict_u = dist > 0
    rowsum_u = dist <= 0
    blk = (jnp.right_shift(u_r, DN_BLK_SHIFT) == jnp.right_shift(u_j, DN_BLK_SHIFT))[0:C, :]
    bd_r = lax.broadcasted_iota(jnp.int32, (UR, UR), 0)
    bd_c = lax.broadcasted_iota(jnp.int32, (UR, UR), 1)
    same_chunk = jnp.right_shift(bd_r, 6) == jnp.right_shift(bd_c, 6)
    bd_lower = (same_chunk & (bd_r >= bd_c)).astype(BF16)
    bd_upper = (same_chunk & (bd_r <= bd_c)).astype(BF16)
    bd_ones = same_chunk.astype(BF16)
    half_l = (lax.broadcasted_iota(jnp.int32, (C, LANE), 1) < C)

    def split_hi_mid(x):
        hi = x.astype(BF16)
        mid = (x - hi.astype(F32)).astype(BF16)
        return jnp.concatenate([hi, mid], axis=1)

    def dot01(m01, parts):
        r = _dot(m01, parts)
        return r[:, 0:LANE] + r[:, LANE:2 * LANE]

    def block_diag(x):
        return jnp.concatenate([jnp.where(half_l, x, 0.0), jnp.where(half_l, 0.0, x)], axis=0).astype(BF16)

    def mm2s(xs, ys):
        return [_dot(x.astype(BF16), block_diag(y)) for x, y in zip(xs, ys)]

    def tri_inv_minus_eye(lms):
        md = [jnp.where(blk, -lm, 0.0) for lm in lms]
        md2 = mm2s(md, md)
        md4 = mm2s(md2, md2)
        md3 = mm2s(md, md2)
        md8 = mm2s(md4, md4)
        px = [a + b + c for a, b, c in zip(md, md2, md3)]
        md12 = mm2s(md4, md8)
        qx = [a + b + c for a, b, c in zip(md4, md8, md12)]
        pq = mm2s(px, qx)
        dx = [a + b + c for a, b, c in zip(px, qx, pq)]
        loff = [jnp.where(blk, 0.0, lm) for lm in lms]
        dl = mm2s(dx, loff)
        n1 = [a + b for a, b in zip(loff, dl)]
        n2 = mm2s(n1, n1)
        n3 = mm2s(n1, n2)
        rx = [b - a - c for a, b, c in zip(n1, n2, n3)]
        rd = mm2s(rx, dx)
        return [a + b + c for a, b, c in zip(rx, dx, rd)]

    lane_u = u_c

    def pick(x, idx):
        return jnp.broadcast_to(jnp.sum(jnp.where(lane_u == idx, x, 0.0), axis=1, keepdims=True), x.shape)

    zeros_r = jnp.zeros((C, 2 * LANE), BF16)

    def local_group(g, carry):
        units = [(hh, g * NU + s) for hh in range(HB) for s in range(NU)]
        U = []
        for hh, un in units:
            rows = pl.ds(_aligned(un * UR, UR), UR)
            bl = bl_ref[0, rows, :]
            hg = head0 + hh
            U.append(dict(hh=hh, un=un, rows=rows,
                          bb_f=pick(bl, hg), bb_b=pick(bl, DN_HEADS + hg),
                          la_f=pick(bl, 2 * DN_HEADS + hg), la_b=pick(bl, 3 * DN_HEADS + hg)))
        for u in U:
            u["sf"] = split_hi_mid(u["la_f"])
            u["sb"] = split_hi_mid(u["la_b"])
            u["sr"] = split_hi_mid(jnp.where(rowsum_u, jnp.where(is_b, u["la_b"], u["la_f"]), 0.0))
        for u in U:
            u["gc_f"] = dot01(bd_lower, u["sf"])
        for u in U:
            u["gc_b"] = dot01(bd_upper, u["sb"])
        for u in U:
            u["g_row"] = dot01(bd_ones, u["sr"])
        for u in U:
            hh, rows = u["hh"], u["rows"]
            u["k"] = kn_ref[hh, rows, :]
            u["q"] = qn_ref[hh, rows, :]
            u["k16"] = u["k"].astype(BF16)
            u["q16"] = u["q"].astype(BF16)
        chunks = [(u, ci) for u in U for ci in range(DN_UNIT)]
        crs = [slice(ci * C, (ci + 1) * C) for _, ci in chunks]
        grams = [_dot_nt(jnp.concatenate([u["k16"][cr], u["q16"][cr]], axis=0),
                         jnp.concatenate([u["k16"][cr], u["k16"][cr]], axis=0))
                 for (u, _), cr in zip(chunks, crs)]
        for u in U:
            u["decay"] = jnp.exp(jnp.where(incl_u, jnp.where(is_b, u["gc_b"], u["gc_f"]) - u["g_row"], NEG_INF))
            u["bb_p"] = jnp.where(is_b, u["bb_b"], u["bb_f"])
        lms = [jnp.where(strict_u[cr], gm[0:C] * u["bb_p"][cr] * u["decay"][cr], 0.0)
               for (u, _), cr, gm in zip(chunks, crs, grams)]
        attn = [(gm[C:2 * C] * u["decay"][cr]).astype(BF16) for (u, _), cr, gm in zip(chunks, crs, grams)]
        txs = tri_inv_minus_eye(lms)
        for u in U:
            v = vn_ref[u["hh"], u["rows"], :]
            eg_f = jnp.exp(u["gc_f"])
            eg_b = jnp.exp(u["gc_b"])
            kb_f = u["k"] * u["bb_f"]
            kb_b = u["k"] * u["bb_b"]
            u["rhs_f"] = jnp.concatenate([v * u["bb_f"], kb_f * eg_f], axis=1)
            u["rhs_b"] = jnp.concatenate([v * u["bb_b"], kb_b * eg_b], axis=1)
            u["r16_f"] = u["rhs_f"].astype(BF16)
            u["r16_b"] = u["rhs_b"].astype(BF16)
            u["qg_f"] = u["q"] * eg_f
            u["qg_b"] = u["q"] * eg_b
        uws = [_dot(tx.astype(BF16), jnp.concatenate(
                    [jnp.concatenate([u["r16_f"][cr], zeros_r], axis=1),
                     jnp.concatenate([zeros_r, u["r16_b"][cr]], axis=1)], axis=0))
               for (u, _), cr, tx in zip(chunks, crs, txs)]
        wu_f = [(u["rhs_f"][cr] + uw[:, 0:2 * LANE]) for (u, _), cr, uw in zip(chunks, crs, uws)]
        wu_b = [(u["rhs_b"][cr] + uw[:, 2 * LANE:4 * LANE]) for (u, _), cr, uw in zip(chunks, crs, uws)]
        wu16_f = [x.astype(BF16) for x in wu_f]
        wu16_b = [x.astype(BF16) for x in wu_b]
        aws = [_dot(a, jnp.concatenate([jnp.concatenate([xf, zeros_r], axis=1),
                                        jnp.concatenate([zeros_r, xb], axis=1)], axis=0))
               for a, xf, xb in zip(attn, wu16_f, wu16_b)]
        kg_f, kg_b, gls = [], [], []
        for (u, ci), cr in zip(chunks, crs):
            gl_f = u["gc_f"][ci * C + C - 1:ci * C + C, :]
            gl_b = u["gc_b"][ci * C:ci * C + 1, :]
            kg_f.append((u["k"][cr] * jnp.exp(gl_f - u["gc_f"][cr])).astype(BF16))
            kg_b.append((u["k"][cr] * jnp.exp(gl_b - u["gc_b"][cr])).astype(BF16))
            gls.append((jnp.exp(gl_f), jnp.exp(gl_b)))
        kwu_f = [_dot_tn(kg, x) for kg, x in zip(kg_f, wu16_f)]
        kwu_b = [_dot_tn(kg, x) for kg, x in zip(kg_b, wu16_b)]
        for idx, ((u, ci), cr) in enumerate(zip(chunks, crs)):
            hh = u["hh"]
            c = u["un"] * DN_UNIT + ci
            crow = pl.ds(_aligned(c * C, C), C)
            aw = aws[idx]
            for d, kwu, qg, a0 in ((0, kwu_f[idx], u["qg_f"], 0), (1, kwu_b[idx], u["qg_b"], 2 * LANE)):
                aq_ref[hh, d, c, 0:DN_DK, :] = kwu[:, LANE:2 * LANE].astype(BF16)
                aq_ref[hh, d, c, DN_DK:DN_DK + C, :] = (qg[cr] - aw[:, a0 + LANE:a0 + 2 * LANE]).astype(BF16)
                bm_ref[hh, d, c] = kwu[:, 0:LANE]
                oacc_ref[hh, d, crow, :] = aw[:, a0:a0 + LANE]
                gl_ref[hh, d, c] = jnp.broadcast_to(gls[idx][d], (8, LANE))
        return carry

    n_groups = T // (NU * UR)
    if n_groups == 1:
        local_group(0, 0)
    else:
        lax.fori_loop(0, n_groups, local_group, 0)

    def body(i, carry):
        cidx = (i, NC - 1 - i)
        rs = [_dot(aq_ref[hh, d, cidx[d]], carry[2 * hh + d].astype(BF16)) for hh in range(HB) for d in range(2)]
        new = []
        for hh in range(HB):
            for d in range(2):
                r = rs[2 * hh + d]
                c = cidx[d]
                rows = pl.ds(pl.multiple_of(c * C, C), C)
                oacc_ref[hh, d, rows, :] = oacc_ref[hh, d, rows, :] + r[DN_DK:DN_DK + C]
                new.append(carry[2 * hh + d] * gl_ref[hh, d, c, 0:1, :] + bm_ref[hh, d, c] - r[0:DN_DK])
        return tuple(new)

    if has_s0:
        init = tuple(s0_ref[0, 0, d, hh] for hh in range(HB) for d in range(2))
    else:
        init = tuple(jnp.zeros((DN_DK, LANE), F32) for _ in range(2 * HB))
    fin = lax.fori_loop(0, NC, body, init, unroll=NC <= DN_UNIT)
    if want_state:
        for hh in range(HB):
            for d in range(2):
                sfin_ref[0, 0, d, hh] = fin[2 * hh + d]

    for hh in range(HB):
        cols = slice(hh * LANE, (hh + 1) * LANE)
        for r0 in range(0, T, RB):
            o = oacc_ref[hh, 0, r0:r0 + RB, :] + oacc_ref[hh, 1, r0:r0 + RB, :]
            o = o * lax.rsqrt(jnp.mean(o * o, axis=-1, keepdims=True) + NORM_EPS) * gn_ref[0]
            y_ref[0, r0:r0 + RB, cols] = (o * _silu(z_ref[0, r0:r0 + RB, cols].astype(F32))).astype(BF16)


def _delta_mixer(proj3, bl3, l, conv_w, g_norm, s0, want_state, s_prev, HB, NU):
    B, T, _ = proj3.shape
    has_s0 = s0 is not None
    NC = T // DN_CHUNK
    W = HB * LANE
    kern = functools.partial(_delta_kernel, T=T, HB=HB, NU=NU, has_s0=has_s0, want_state=want_state,
                             alias_state=s_prev is not None)
    col = lambda cb: pl.BlockSpec((1, T, W), lambda b, h, cb=cb: (b, 0, cb // HB + h))
    cw = lambda cb: pl.BlockSpec((1, 3, W), lambda b, h, cb=cb: (l, 0, cb // HB + h))
    in_specs = [col(CB_DN_Q), col(CB_DN_K), col(CB_DN_V), col(CB_DN_Z),
                pl.BlockSpec((1, T, LANE), lambda b, h: (b, 0, 0)),
                cw(0), cw(4), cw(8), pl.BlockSpec((1, 1, LANE), lambda b, h: (l, 0, 0))]
    args = [proj3, proj3, proj3, proj3, bl3, conv_w, conv_w, conv_w, g_norm]
    if has_s0:
        in_specs.append(pl.BlockSpec((1, 1, 2, HB, DN_DK, LANE), lambda b, h: (b, l, 0, h, 0, 0)))
        args.append(s0)
    aliases = {}
    if s_prev is not None:
        aliases = {len(args): 1}
        in_specs.append(pl.BlockSpec(memory_space=pl.ANY))
        args.append(s_prev)
    out_specs = [pl.BlockSpec((1, T, W), lambda b, h: (b, 0, h))]
    out_shape = [jax.ShapeDtypeStruct((B, T, DN_WIDTH), BF16)]
    if want_state:
        out_specs.append(pl.BlockSpec((1, 1, 2, HB, DN_DK, LANE), lambda b, h: (b, l, 0, h, 0, 0)))
        out_shape.append(jax.ShapeDtypeStruct((B, DEPTH, 2, DN_HEADS, DN_DK, LANE), F32))
    res = pl.pallas_call(
        kern,
        grid=(B, DN_HEADS // HB),
        in_specs=in_specs,
        out_specs=out_specs,
        out_shape=out_shape,
        scratch_shapes=[
            pltpu.VMEM((T + 16, LANE), F32),
            pltpu.VMEM((HB, T, LANE), F32), pltpu.VMEM((HB, T, LANE), F32), pltpu.VMEM((HB, T, LANE), F32),
            pltpu.VMEM((HB, 2, NC, DN_DK + DN_CHUNK, LANE), BF16),
            pltpu.VMEM((HB, 2, NC, DN_DK, LANE), F32),
            pltpu.VMEM((HB, 2, T, LANE), F32),
            pltpu.VMEM((HB, 2, NC, 8, LANE), F32),
        ],
        input_output_aliases=aliases,
        compiler_params=_params(("parallel", "parallel")),
        name="delta_mixer",
    )(*args)
    return (res[0], res[1]) if want_state else (res[0], None)


def _head_masks(rows, dtype, value=1.0):
    lane = lax.broadcasted_iota(jnp.int32, (rows, LANE), 1)
    return (jnp.where(lane < NA_HD, value, 0.0).astype(dtype), jnp.where(lane >= NA_HD, value, 0.0).astype(dtype))


NA_SCALE = NA_HD ** -0.5


def _ctx_attn_kernel(q_ref, k_ref, v_ref, z_ref, y_ref, *, T):
    m0, m1 = _head_masks(T, BF16, NA_SCALE)
    f0, f1 = _head_masks(T, F32)
    pairs = range(NA_WIDTH // LANE)
    cols = [slice(p * LANE, (p + 1) * LANE) for p in pairs]
    s = [_dot_nt(q_ref[0, :, cols[p]] * hm, k_ref[0, :, cols[p]]) for p in pairs for hm in (m0, m1)]
    e = [jnp.exp(x - jnp.max(x, axis=-1, keepdims=True)) for x in s]
    o = [_dot(e[2 * p + hh].astype(BF16), v_ref[0, :, cols[p]]) for p in pairs for hh in range(2)]
    o = [x / jnp.sum(ee, axis=-1, keepdims=True) for x, ee in zip(o, e)]
    for p in pairs:
        out = o[2 * p] * f0 + o[2 * p + 1] * f1
        y_ref[0, :, cols[p]] = (out * _silu(z_ref[0, :, cols[p]].astype(F32))).astype(BF16)


def _ctx_attention(proj3):
    B, T, _ = proj3.shape
    wide = NA_WIDTH // LANE
    spec = lambda cb: pl.BlockSpec((1, T, NA_WIDTH), lambda b, cb=cb: (b, 0, cb // wide))
    return pl.pallas_call(
        functools.partial(_ctx_attn_kernel, T=T),
        grid=(B,),
        in_specs=[spec(CB_NA_Q), spec(CB_NA_K), spec(CB_NA_V), spec(CB_NA_Z)],
        out_specs=pl.BlockSpec((1, T, NA_WIDTH), lambda b: (b, 0, 0)),
        out_shape=jax.ShapeDtypeStruct((B, T, NA_WIDTH), BF16),
        compiler_params=_params(("parallel",)),
        name="ctx_attention",
    )(proj3, proj3, proj3, proj3)


def _na_kernel(q_ref, k_ref, v_ref, z_ref, kc_ref, vc_ref, bias_ref, y_ref, sctx_ref, *, T, L):
    W = GRID_W
    rows_total = T // W
    n_win = NA_WIN_R * W
    RS = NA_ROWS_PER_STEP
    m0, m1 = _head_masks(W, BF16, NA_SCALE)
    f0, f1 = _head_masks(W, F32)
    kcb = kc_ref[0, 0].astype(BF16)
    vcb = vc_ref[0, 0].astype(BF16)

    RB = 256
    mb0, mb1 = _head_masks(RB, BF16, NA_SCALE)
    for r0 in range(0, T, RB):
        q = q_ref[0, r0:r0 + RB, :]
        sctx_ref[0, r0:r0 + RB, :] = _dot_nt(q * mb0, kcb)
        sctx_ref[1, r0:r0 + RB, :] = _dot_nt(q * mb1, kcb)

    def step(i, carry):
        rr = [i * RS + j for j in range(RS)]
        rs_ = [jnp.clip(r - NA_WIN_R // 2, 0, rows_total - NA_WIN_R) for r in rr]
        qrows = [pl.ds(pl.multiple_of(r * W, W), W) for r in rr]
        krows = [pl.ds(pl.multiple_of(rs * W, W), n_win) for rs in rs_]
        qs = []
        for qr in qrows:
            q = q_ref[0, qr, :]
            qs.append(jnp.concatenate([q * m0, q * m1], axis=0))
        s = [_dot_nt(x, k_ref[0, kr, :]) for x, kr in zip(qs, krows)]
        e1, e2, den = [], [], []
        for j in range(RS):
            dr0 = rs_[j] - rr[j] + NA_WIN_R - 1
            bias = jnp.concatenate(
                [jnp.concatenate([bias_ref[0, hh, dr0 + 2 * m] for m in range(NA_WIN_R // 2)], axis=1)
                 for hh in range(2)], axis=0)
            sl = s[j] + bias
            sc = jnp.concatenate([sctx_ref[0, qrows[j], :], sctx_ref[1, qrows[j], :]], axis=0)
            mx = jnp.maximum(jnp.max(sl, axis=-1, keepdims=True), jnp.max(sc, axis=-1, keepdims=True))
            a = jnp.exp(sl - mx)
            b = jnp.exp(sc - mx)
            e1.append(a.astype(BF16))
            e2.append(b.astype(BF16))
            den.append(jnp.sum(a, axis=-1, keepdims=True) + jnp.sum(b, axis=-1, keepdims=True))
        o = [_dot(a, v_ref[0, kr, :]) + _dot(b, vcb) for a, b, kr in zip(e1, e2, krows)]
        for j in range(RS):
            oj = o[j] / den[j]
            out = oj[0:W, :] * f0 + oj[W:2 * W, :] * f1
            y_ref[0, qrows[j], :] = (out * _silu(z_ref[0, qrows[j], :].astype(F32))).astype(BF16)
        return carry

    lax.fori_loop(0, rows_total // RS, step, 0)


def _neighbourhood_attention(proj3, l, k_ctx, v_ctx, bias2):
    B, T, _ = proj3.shape
    L = k_ctx.shape[2]
    col = lambda cb: pl.BlockSpec((1, T, LANE), lambda b, p, cb=cb: (b, 0, cb + p))
    ctx = pl.BlockSpec((1, 1, L, LANE), lambda b, p: (b, l, 0, p))
    return pl.pallas_call(
        functools.partial(_na_kernel, T=T, L=L),
        grid=(B, NA_WIDTH // LANE),
        in_specs=[col(CB_NA_Q), col(CB_NA_K), col(CB_NA_V), col(CB_NA_Z), ctx, ctx,
                  pl.BlockSpec((1, 2, 2 * NA_WIN_R - 2, GRID_W, LANE), lambda b, p: (l, p, 0, 0, 0))],
        out_specs=pl.BlockSpec((1, T, LANE), lambda b, p: (b, 0, p)),
        out_shape=jax.ShapeDtypeStruct((B, T, NA_WIDTH), BF16),
        scratch_shapes=[pltpu.VMEM((2, T, L), F32)],
        compiler_params=_params(("parallel", "parallel")),
        name="nbr_attention",
    )(proj3, proj3, proj3, proj3, k_ctx, v_ctx, bias2)


def _pool_kernel(u_ref, z_ref, pw_ref, ps_ref, y_ref, x_buf, s2_buf, s4_buf, s8_buf, s16_buf, *, T):
    P = POOL_PAD
    E = T + 2 * P
    RB = 256
    bufs = (x_buf, s2_buf, s4_buf, s8_buf, s16_buf)
    for b in bufs:
        zeros = jnp.zeros((P, b.shape[1]), F32)
        b[0:P, :] = zeros
        b[E + P:E + 2 * P, :] = zeros
    zeros = jnp.zeros((P, POOL_WIDTH), F32)
    x_buf[P:2 * P, :] = zeros
    x_buf[T + 2 * P:T + 3 * P, :] = zeros
    for r0 in range(0, T, RB):
        x_buf[2 * P + r0:2 * P + r0 + RB, :] = u_ref[0, r0:r0 + RB, :].astype(F32)

    def level(src, dst, lo, hi, lane0):
        for e0 in range(0, E, RB):
            n = min(RB, E - e0)
            dst[P + e0:P + e0 + n, :] = (src[P + e0 + lo:P + e0 + lo + n, lane0:]
                                         + src[P + e0 + hi:P + e0 + hi + n, lane0:])

    level(x_buf, s2_buf, -1, 0, 0)
    level(s2_buf, s4_buf, -1, 1, LANE)
    level(s4_buf, s8_buf, -2, 2, LANE)
    level(s8_buf, s16_buf, -4, 4, LANE)

    sums = (s2_buf, s4_buf, s8_buf, s16_buf)
    for r0 in range(0, T, RB):
        t = r0 + lax.broadcasted_iota(jnp.int32, (RB, 1), 0)
        for g, win in enumerate(POOL_WINDOWS):
            cols = slice(g * POOL_GC, (g + 1) * POOL_GC)
            lo = jnp.maximum(t - win // 2, 0)
            hi = jnp.minimum(t + win // 2 - 1, T - 1)
            cnt = (hi - lo + 1).astype(F32)
            tot = sums[g][2 * P + r0:2 * P + r0 + RB, 0:POOL_GC]
            pooled = tot / cnt - x_buf[2 * P + r0:2 * P + r0 + RB, cols]
            y = _dot(pooled.astype(BF16), pw_ref[0, g]) * ps_ref[0, :, cols]
            y_ref[0, r0:r0 + RB, cols] = (y * _silu(z_ref[0, r0:r0 + RB, cols].astype(F32))).astype(BF16)


def _pool_mixer(proj3, l, pool_w, pool_scale):
    B, T, _ = proj3.shape
    rows = T + 4 * POOL_PAD
    wide = POOL_WIDTH // LANE
    return pl.pallas_call(
        functools.partial(_pool_kernel, T=T),
        grid=(B,),
        in_specs=[
            pl.BlockSpec((1, T, POOL_WIDTH), lambda b: (b, 0, CB_PL_U // wide)),
            pl.BlockSpec((1, T, POOL_WIDTH), lambda b: (b, 0, CB_PL_Z // wide)),
            pl.BlockSpec((1, len(POOL_WINDOWS), POOL_GC, POOL_GC), lambda b: (l, 0, 0, 0)),
            pl.BlockSpec((1, 1, POOL_WIDTH), lambda b: (l, 0, 0)),
        ],
        out_specs=pl.BlockSpec((1, T, POOL_WIDTH), lambda b: (b, 0, 0)),
        out_shape=jax.ShapeDtypeStruct((B, T, POOL_WIDTH), BF16),
        scratch_shapes=[
            pltpu.VMEM((rows, 4 * LANE), F32), pltpu.VMEM((rows, 4 * LANE), F32),
            pltpu.VMEM((rows, 3 * LANE), F32), pltpu.VMEM((rows, 2 * LANE), F32),
            pltpu.VMEM((rows, LANE), F32),
        ],
        compiler_params=_params(("parallel",)),
        name="pool_mixer",
    )(proj3, proj3, pool_w, pool_scale)


def _merge_kernel(ydn_ref, yna_ref, ypl_ref, gdn_ref, gna_ref, gpl_ref, x_ref, mod_ref, gpost_ref,
                  wd_ref, wn_ref, wp_ref, wo_ref, o_ref, *, tiles_per_batch, mod_row0):
    merged = (_sigmoid(gdn_ref[...].astype(F32)) * _dot(ydn_ref[...], wd_ref[0])
              + _sigmoid(gna_ref[...].astype(F32)) * _dot(yna_ref[...], wn_ref[0])
              + _sigmoid(gpl_ref[...].astype(F32)) * _dot(ypl_ref[...], wp_ref[0]))
    out = _dot(merged.astype(BF16), wo_ref[0])
    out = out * lax.rsqrt(jnp.mean(out * out, axis=-1, keepdims=True) + NORM_EPS) * gpost_ref[0]
    gate = _mod_row(mod_ref, tiles_per_batch, mod_row0)[:, 2 * D_MODEL:3 * D_MODEL]
    o_ref[...] = x_ref[...] + gate * out


def _merge(y_dn, y_na, y_pl, proj2d, x2d, l, mod, g_post, w_dn, w_na, w_pl, w_out, tiles_per_batch, mod_row0, tm):
    ntok = x2d.shape[0]
    g0 = CB_GATE * LANE // D_MODEL
    br = pl.BlockSpec((tm, DN_WIDTH), lambda m: (m, 0))
    gate = lambda j: pl.BlockSpec((tm, D_MODEL), lambda m, j=j: (m, g0 + j))
    wbr = pl.BlockSpec((1, DN_WIDTH, D_MODEL), lambda m: (l, 0, 0))
    return pl.pallas_call(
        functools.partial(_merge_kernel, tiles_per_batch=tiles_per_batch, mod_row0=mod_row0),
        grid=(ntok // tm,),
        in_specs=[br, br, br, gate(0), gate(1), gate(2),
                  pl.BlockSpec((tm, D_MODEL), lambda m: (m, 0)),
                  pl.BlockSpec((1, 8, 3 * D_MODEL), lambda m: (l, 0, 0)),
                  pl.BlockSpec((1, 1, D_MODEL), lambda m: (l, 0, 0)),
                  wbr, wbr, wbr,
                  pl.BlockSpec((1, D_MODEL, D_MODEL), lambda m: (l, 0, 0))],
        out_specs=pl.BlockSpec((tm, D_MODEL), lambda m: (m, 0)),
        out_shape=jax.ShapeDtypeStruct((ntok, D_MODEL), F32),
        compiler_params=_params(("parallel",)),
        name="merge_out",
    )(y_dn, y_na, y_pl, proj2d, proj2d, proj2d, x2d, mod, g_post, w_dn, w_na, w_pl, w_out)


def _layer(x3, l, mod_row0, s0, k_ctx, v_ctx, pw, tm_proj, tm_merge, delta_cfg, carry=None):
    B, T, _ = x3.shape
    n = B * T
    x2d = x3.reshape(n, D_MODEL)
    is_ctx = k_ctx is None
    res = _project(x2d, l, pw["mod"], pw["g_pre"], pw["w_main"], pw["w_ba"], pw["alog_v"], pw["dtb_v"],
                   None if is_ctx else T // tm_proj, mod_row0,
                   kv=None if not is_ctx else ((B, T) if carry is None else carry[:2]), tm=tm_proj)
    proj2d, bl2d = res[0], res[1]
    proj3 = proj2d.reshape(B, T, P_MAIN)
    y_dn, s_fin = _delta_mixer(proj3, bl2d.reshape(B, T, LANE), l, pw["conv_w"], pw["g_norm"], s0,
                               want_state=is_ctx, s_prev=None if carry is None else carry[2],
                               HB=delta_cfg[0], NU=delta_cfg[1])
    if is_ctx:
        y_na = _ctx_attention(proj3)
    else:
        y_na = _neighbourhood_attention(proj3, l, k_ctx, v_ctx, pw["bias2"])
    y_pl = _pool_mixer(proj3, l, pw["pool_w"], pw["pool_scale"])
    out2d = _merge(y_dn.reshape(n, DN_WIDTH), y_na.reshape(n, NA_WIDTH), y_pl.reshape(n, POOL_WIDTH), proj2d,
                   x2d, l, pw["mod"], pw["g_post"], pw["w_dn"], pw["w_na"], pw["w_pl"], pw["w_out"],
                   None if is_ctx else T // tm_merge, mod_row0, tm=tm_merge)
    return out2d.reshape(B, T, D_MODEL), (tuple(res[2:]) + (s_fin,) if is_ctx else None)


def _lane_vecs(v, offset):
    return jnp.zeros((DEPTH, 1, LANE), F32).at[:, 0, offset:offset + 2 * DN_HEADS].set(v.reshape(DEPTH, -1))


def _bias_windows(na_bias):
    period = 2 * GRID_W - 1
    half = NA_WIN_C - 1
    zeros = jnp.zeros(na_bias.shape[:-1] + (period - 2 * half - 1,), F32)
    r = jnp.concatenate([na_bias[..., half:], zeros, na_bias[..., :half]], axis=-1)
    rows = jnp.tile(r, (1, 1, 1, GRID_W))[..., :GRID_W * (period - 1)]
    bias_t = rows.reshape(na_bias.shape[:-1] + (GRID_W, period - 1))[..., :GRID_W]
    cols = np.arange(GRID_W)
    cs = np.clip(cols - NA_WIN_C // 2, 0, GRID_W - NA_WIN_C)[:, None]
    col_ok = (cols[None, :] >= cs) & (cols[None, :] < cs + NA_WIN_C)
    bias_t = jnp.where(col_ok, bias_t, NEG_INF)
    return jnp.concatenate([bias_t[:, :, :-1], bias_t[:, :, 1:]], axis=-1)


def kernel(x_prompt, x_sample, c, cache_k_na, cache_v_na, state_dn, c_ctx, w_ada, b_ada, g_pre, g_post, w_in,
           conv_dn, a_log_dn, dt_bias_dn, g_norm_dn, na_bias, pool_w, pool_scale, w_br_dn, w_br_na, w_br_pl,
           w_out):
    B, T, _ = x_prompt.shape
    DB, DT, _ = x_sample.shape
    L = cache_k_na.shape[2]

    cc = jnp.zeros((8, D_MODEL), F32).at[0].set(c_ctx).at[1:1 + DB].set(c)
    w_main, w_ba = _regroup_weights(w_in)
    pw = dict(
        mod=_modulation(cc, w_ada, b_ada),
        g_pre=g_pre[:, None], g_post=g_post[:, None], g_norm=g_norm_dn[:, None], pool_scale=pool_scale[:, None],
        w_main=w_main, w_ba=w_ba,
        alog_v=_lane_vecs(a_log_dn, 2 * DN_HEADS), dtb_v=_lane_vecs(dt_bias_dn, 2 * DN_HEADS),
        conv_w=conv_dn, bias2=_bias_windows(na_bias), pool_w=pool_w.astype(BF16),
        w_dn=w_br_dn.astype(BF16), w_na=w_br_na.astype(BF16), w_pl=w_br_pl.astype(BF16), w_out=w_out.astype(BF16))
    k_ctx = cache_k_na.reshape(DB, DEPTH, L, NA_WIDTH)
    v_ctx = cache_v_na.reshape(DB, DEPTH, L, NA_WIDTH)

    xp, xs = x_prompt, x_sample
    carry = None
    for l in range(DEPTH):
        xp, carry = _layer(xp, l, 0, None, None, None, pw, tm_proj=1024, tm_merge=512, delta_cfg=(4, 1),
                           carry=carry)
        xs, _ = _layer(xs, l, 1, state_dn, k_ctx, v_ctx, pw, tm_proj=1024, tm_merge=512, delta_cfg=(2, 2))
    k_all, v_all, s_all = carry
    return (xp, xs, k_all.reshape(B, DEPTH, T, NA_HEADS, NA_HD), v_all.reshape(B, DEPTH, T, NA_HEADS, NA_HD), s_all)
```

```python
import functools

import numpy as np
import jax
import jax.numpy as jnp
from jax import lax
from jax.experimental import pallas as pl
from jax.experimental.pallas import tpu as pltpu

F32 = jnp.float32
BF16 = jnp.bfloat16

D_MODEL = 1024
DEPTH = 4
GRID_W = 64
NORM_EPS = 1e-6
NEG_INF = -1e30

DN_HEADS = 4
DN_DK = 128
DN_WIDTH = 512
DN_CHUNK = 64
DN_BLK_SHIFT = 4
DN_UNIT = 4

NA_HEADS = 8
NA_HD = 64
NA_WIDTH = 512
NA_WIN_R = 8
NA_WIN_C = 16
NA_ROWS_PER_STEP = 8

POOL_WINDOWS = (2, 4, 8, 16)
POOL_GC = 128
POOL_WIDTH = 512
POOL_HALO = 16
POOL_ROWS = 256

LANE = 128
P_MAIN = 8192
CB_DN_Q, CB_DN_K, CB_DN_V, CB_DN_Z = 0, 4, 8, 12
CB_NA_K, CB_NA_V = 16, 20
CB_NA_Q, CB_NA_Z = 24, 28
CB_PL_U, CB_PL_Z = 32, 36
CB_GATE = 40
SRC_COLS = ((0, 2048), (2576, 3600), (2064, 2576), (3600, 8208))
OFF_BA_SRC = 2048
N_BA = 16

VMEM_LIMIT = 56 * 1024 * 1024


def _sigmoid(x):
    return 0.5 * jnp.tanh(0.5 * x) + 0.5


def _silu(x):
    return x * _sigmoid(x)


def _softplus(x):
    return jnp.maximum(x, 0.0) + jnp.log1p(jnp.exp(-jnp.abs(x)))


def _dot(a, b):
    return jnp.dot(a, b, preferred_element_type=F32)


def _dot_nt(a, b):
    return lax.dot_general(a, b, (((1,), (1,)), ((), ())), preferred_element_type=F32)


def _dot_tn(a, b):
    return lax.dot_general(a, b, (((0,), (0,)), ((), ())), preferred_element_type=F32)


def _aligned(x, m):
    return x if isinstance(x, int) else pl.multiple_of(x, m)


def _params(sem):
    return pltpu.CompilerParams(dimension_semantics=sem, vmem_limit_bytes=VMEM_LIMIT)


def _mod_kernel(cc_ref, w_ref, b_ref, o_ref):
    a = _silu(cc_ref[...]).astype(BF16)
    o_ref[0] = _dot(a, w_ref[0].astype(BF16)) + b_ref[0]


def _modulation(cc, w_ada, b_ada):
    tn = 1024
    return pl.pallas_call(
        _mod_kernel,
        grid=(DEPTH, 3 * D_MODEL // tn),
        in_specs=[
            pl.BlockSpec((8, D_MODEL), lambda l, n: (0, 0)),
            pl.BlockSpec((1, D_MODEL, tn), lambda l, n: (l, 0, n)),
            pl.BlockSpec((1, 1, tn), lambda l, n: (l, 0, n)),
        ],
        out_specs=pl.BlockSpec((1, 8, tn), lambda l, n: (l, 0, n)),
        out_shape=jax.ShapeDtypeStruct((DEPTH, 8, 3 * D_MODEL), F32),
        compiler_params=_params(("parallel", "parallel")),
        name="adaln_mod",
    )(cc, w_ada, b_ada.reshape(DEPTH, 1, 3 * D_MODEL))


def _regroup_kernel(w_ref, o_ref, ba_ref):
    dst = 0
    step = 4 * LANE
    for a, b in SRC_COLS:
        for c0 in range(a, b, step):
            n = min(step, b - c0)
            o_ref[0, :, dst:dst + n] = w_ref[0, :, c0:c0 + n].astype(BF16)
            dst += n
    ba = w_ref[0, :, OFF_BA_SRC:OFF_BA_SRC + LANE]
    lane = lax.broadcasted_iota(jnp.int32, ba.shape, 1)
    ba_ref[0] = jnp.where(lane < N_BA, ba, 0.0).astype(BF16)


def _regroup_weights(w_in, rows=256):
    depth, d_in, n_in = w_in.shape
    return pl.pallas_call(
        _regroup_kernel,
        grid=(depth, d_in // rows),
        in_specs=[pl.BlockSpec((1, rows, n_in), lambda l, r: (l, r, 0))],
        out_specs=[pl.BlockSpec((1, rows, P_MAIN), lambda l, r: (l, r, 0)),
                   pl.BlockSpec((1, rows, LANE), lambda l, r: (l, r, 0))],
        out_shape=[jax.ShapeDtypeStruct((depth, d_in, P_MAIN), BF16),
                   jax.ShapeDtypeStruct((depth, d_in, LANE), BF16)],
        compiler_params=_params(("parallel", "parallel")),
        name="regroup_weights",
    )(w_in)


def _mod_row(mod_ref, tiles_per_batch, mod_row0):
    if tiles_per_batch is None:
        return mod_ref[0, 0:1, :]
    row = mod_row0 + pl.program_id(0) // tiles_per_batch
    return mod_ref[0, pl.ds(row, 1), :]


def _proj_kernel(*refs, tiles_per_batch, mod_row0, kv_tile, n_alias):
    x_ref, mod_ref, g_ref, w_ref, wba_ref, alog_ref, dtb_ref = refs[:7]
    o_ref, bl_ref = refs[7 + n_alias:9 + n_alias]
    k_ref, v_ref = refs[9 + n_alias:11 + n_alias] if kv_tile is not None else (None, None)
    h_ref = refs[-1]

    @pl.when(pl.program_id(1) == 0)
    def _():
        mod = _mod_row(mod_ref, tiles_per_batch, mod_row0)
        shift = mod[:, 0:D_MODEL]
        scale = mod[:, D_MODEL:2 * D_MODEL]
        x = x_ref[...]
        y = x * lax.rsqrt(jnp.mean(x * x, axis=-1, keepdims=True) + NORM_EPS) * g_ref[0]
        hb = (y * (1.0 + scale) + shift).astype(BF16)
        h_ref[...] = hb
        ba = _dot(hb, wba_ref[0])
        lane = lax.broadcasted_iota(jnp.int32, ba.shape, 1)
        bl_ref[...] = jnp.where(lane < 2 * DN_HEADS, _sigmoid(ba),
                                -jnp.exp(alog_ref[0]) * _softplus(ba + dtb_ref[0]))

    acc = _dot(h_ref[...], w_ref[0])
    o_ref[...] = acc.astype(BF16)
    if kv_tile is not None:
        @pl.when(pl.program_id(1) == kv_tile)
        def _():
            seqs, _, t_len, _ = k_ref.shape
            k_ref[:, 0] = acc[:, 0:NA_WIDTH].reshape(seqs, t_len, NA_WIDTH)
            v_ref[:, 0] = acc[:, NA_WIDTH:2 * NA_WIDTH].reshape(seqs, t_len, NA_WIDTH)


def _project(x2d, l, mod, g_pre, w_main, w_ba, alog_v, dtb_v, tiles_per_batch, mod_row0, kv, tm, tn=2048):
    ntok = x2d.shape[0]
    kv_tile = (CB_NA_K * LANE) // tn if kv is not None else None
    prev = [] if kv is None else list(kv)
    kern = functools.partial(_proj_kernel, tiles_per_batch=tiles_per_batch, mod_row0=mod_row0, kv_tile=kv_tile,
                             n_alias=len(prev))
    vec = pl.BlockSpec((1, 1, LANE), lambda m, n: (l, 0, 0))
    in_specs = [
        pl.BlockSpec((tm, D_MODEL), lambda m, n: (m, 0)),
        pl.BlockSpec((1, 8, 3 * D_MODEL), lambda m, n: (l, 0, 0)),
        pl.BlockSpec((1, 1, D_MODEL), lambda m, n: (l, 0, 0)),
        pl.BlockSpec((1, D_MODEL, tn), lambda m, n: (l, 0, n)),
        pl.BlockSpec((1, D_MODEL, LANE), lambda m, n: (l, 0, 0)),
        vec, vec,
    ] + [pl.BlockSpec(memory_space=pl.ANY)] * len(prev)
    out_specs = [pl.BlockSpec((tm, tn), lambda m, n: (m, n)),
                 pl.BlockSpec((tm, LANE), lambda m, n: (m, 0))]
    out_shape = [jax.ShapeDtypeStruct((ntok, P_MAIN), BF16),
                 jax.ShapeDtypeStruct((ntok, LANE), F32)]
    aliases = {}
    if kv is not None:
        t_len = prev[0].shape[2]
        out_specs += [pl.BlockSpec((tm // t_len, 1, t_len, NA_WIDTH), lambda m, n: (m, l, 0, 0))] * 2
        out_shape += [jax.ShapeDtypeStruct(p.shape, p.dtype) for p in prev]
        aliases = {7 + i: 2 + i for i in range(len(prev))}
    return pl.pallas_call(
        kern,
        grid=(ntok // tm, P_MAIN // tn),
        in_specs=in_specs,
        out_specs=out_specs,
        out_shape=out_shape,
        scratch_shapes=[pltpu.VMEM((tm, D_MODEL), BF16)],
        input_output_aliases=aliases,
        compiler_params=_params(("parallel", "arbitrary")),
        name="in_proj",
    )(x2d, mod, g_pre, w_main, w_ba, alog_v, dtb_v, *prev)


def _delta_kernel(*refs, T, HB, NU, has_s0, want_state):
    it = iter(refs)
    q_ref, k_ref, v_ref, z_ref, bl_ref = (next(it) for _ in range(5))
    cwq_ref, cwk_ref, cwv_ref, gn_ref = (next(it) for _ in range(4))
    s0_ref = next(it) if has_s0 else None
    if want_state:
        next(it)
    y_ref = next(it)
    sfin_ref = next(it) if want_state else None
    pad_ref, qn_ref, kn_ref, vn_ref, aq_ref, bm_ref, oacc_ref, gl_ref = (next(it) for _ in range(8))

    C = DN_CHUNK
    NC = T // C
    RB = min(T, 256)
    UR = DN_UNIT * C
    head0 = pl.program_id(1) * HB

    zeros8 = jnp.zeros((8, LANE), F32)
    pad_ref[0:8, :] = zeros8
    pad_ref[T + 8:T + 16, :] = zeros8

    def conv_into(x_ref, cw_ref, dst_ref, hh, mode):
        cols = slice(hh * LANE, (hh + 1) * LANE)
        for r0 in range(0, T, RB):
            pad_ref[8 + r0:8 + r0 + RB, :] = x_ref[0, r0:r0 + RB, cols].astype(F32)
        w = cw_ref[0, :, cols]
        for r0 in range(0, T, RB):
            y = (pad_ref[7 + r0:7 + r0 + RB, :] * w[0:1, :]
                 + pad_ref[8 + r0:8 + r0 + RB, :] * w[1:2, :]
                 + pad_ref[9 + r0:9 + r0 + RB, :] * w[2:3, :])
            y = _silu(y)
            if mode != "v":
                y = y * lax.rsqrt(jnp.sum(y * y, axis=-1, keepdims=True) + NORM_EPS)
            if mode == "q":
                y = y * (DN_DK ** -0.5)
            dst_ref[hh, r0:r0 + RB, :] = y

    for hh in range(HB):
        conv_into(q_ref, cwq_ref, qn_ref, hh, "q")
        conv_into(k_ref, cwk_ref, kn_ref, hh, "k")
        conv_into(v_ref, cwv_ref, vn_ref, hh, "v")

    u_r = lax.broadcasted_iota(jnp.int32, (UR, LANE), 0) & (C - 1)
    u_c = lax.broadcasted_iota(jnp.int32, (UR, LANE), 1)
    is_b = u_c >= C
    u_j = u_c & (C - 1)
    dist = jnp.where(is_b, u_j - u_r, u_r - u_j)
    incl_u = dist >= 0
    strict_u = dist > 0
    rowsum_u = dist <= 0
    blk = (jnp.right_shift(u_r, DN_BLK_SHIFT) == jnp.right_shift(u_j, DN_BLK_SHIFT))[0:C, :]
    bd_r = lax.broadcasted_iota(jnp.int32, (UR, UR), 0)
    bd_c = lax.broadcasted_iota(jnp.int32, (UR, UR), 1)
    same_chunk = jnp.right_shift(bd_r, 6) == jnp.right_shift(bd_c, 6)
    bd_lower = (same_chunk & (bd_r >= bd_c)).astype(BF16)
    bd_upper = (same_chunk & (bd_r <= bd_c)).astype(BF16)
    bd_ones = same_chunk.astype(BF16)
    half_l = (lax.broadcasted_iota(jnp.int32, (C, LANE), 1) < C)

    def split_hi_mid(x):
        hi = x.astype(BF16)
        mid = (x - hi.astype(F32)).astype(BF16)
        return jnp.concatenate([hi, mid], axis=1)

    def dot01(m01, parts):
        r = _dot(m01, parts)
        return r[:, 0:LANE] + r[:, LANE:2 * LANE]

    def block_diag(x):
        return jnp.concatenate([jnp.where(half_l, x, 0.0), jnp.where(half_l, 0.0, x)], axis=0).astype(BF16)

    def mm2s(xs, ys):
        return [_dot(x.astype(BF16), block_diag(y)) for x, y in zip(xs, ys)]

    def tri_inv_minus_eye(lms):
        md = [jnp.where(blk, -lm, 0.0) for lm in lms]
        md2 = mm2s(md, md)
        md4 = mm2s(md2, md2)
        md3 = mm2s(md, md2)
        md8 = mm2s(md4, md4)
        px = [a + b + c for a, b, c in zip(md, md2, md3)]
        md12 = mm2s(md4, md8)
        qx = [a + b + c for a, b, c in zip(md4, md8, md12)]
        pq = mm2s(px, qx)
        dx = [a + b + c for a, b, c in zip(px, qx, pq)]
        loff = [jnp.where(blk, 0.0, lm) for lm in lms]
        dl = mm2s(dx, loff)
        n1 = [a + b for a, b in zip(loff, dl)]
        n2 = mm2s(n1, n1)
        n3 = mm2s(n1, n2)
        rx = [b - a - c for a, b, c in zip(n1, n2, n3)]
        rd = mm2s(rx, dx)
        return [a + b + c for a, b, c in zip(rx, dx, rd)]

    lane_u = u_c

    def pick(x, idx):
        return jnp.broadcast_to(jnp.sum(jnp.where(lane_u == idx, x, 0.0), axis=1, keepdims=True), x.shape)

    zeros_r = jnp.zeros((C, 2 * LANE), BF16)

    def local_group(g, carry):
        units = [(hh, g * NU + s) for hh in range(HB) for s in range(NU)]
        U = []
        for hh, un in units:
            rows = pl.ds(_aligned(un * UR, UR), UR)
            bl = bl_ref[0, rows, :]
            hg = head0 + hh
            U.append(dict(hh=hh, un=un, rows=rows,
                          bb_f=pick(bl, hg), bb_b=pick(bl, DN_HEADS + hg),
                          la_f=pick(bl, 2 * DN_HEADS + hg), la_b=pick(bl, 3 * DN_HEADS + hg)))
        for u in U:
            u["sf"] = split_hi_mid(u["la_f"])
            u["sb"] = split_hi_mid(u["la_b"])
            u["sr"] = split_hi_mid(jnp.where(rowsum_u, jnp.where(is_b, u["la_b"], u["la_f"]), 0.0))
        for u in U:
            u["gc_f"] = dot01(bd_lower, u["sf"])
        for u in U:
            u["gc_b"] = dot01(bd_upper, u["sb"])
        for u in U:
            u["g_row"] = dot01(bd_ones, u["sr"])
        for u in U:
            hh, rows = u["hh"], u["rows"]
            u["k"] = kn_ref[hh, rows, :]
            u["q"] = qn_ref[hh, rows, :]
            u["k16"] = u["k"].astype(BF16)
            u["q16"] = u["q"].astype(BF16)
        chunks = [(u, ci) for u in U for ci in range(DN_UNIT)]
        crs = [slice(ci * C, (ci + 1) * C) for _, ci in chunks]
        grams = [_dot_nt(jnp.concatenate([u["k16"][cr], u["q16"][cr]], axis=0),
                         jnp.concatenate([u["k16"][cr], u["k16"][cr]], axis=0))
                 for (u, _), cr in zip(chunks, crs)]
        for u in U:
            u["decay"] = jnp.exp(jnp.where(incl_u, jnp.where(is_b, u["gc_b"], u["gc_f"]) - u["g_row"], NEG_INF))
            u["bb_p"] = jnp.where(is_b, u["bb_b"], u["bb_f"])
        lms = [jnp.where(strict_u[cr], gm[0:C] * u["bb_p"][cr] * u["decay"][cr], 0.0)
               for (u, _), cr, gm in zip(chunks, crs, grams)]
        attn = [(gm[C:2 * C] * u["decay"][cr]).astype(BF16) for (u, _), cr, gm in zip(chunks, crs, grams)]
        txs = tri_inv_minus_eye(lms)
        for u in U:
            v = vn_ref[u["hh"], u["rows"], :]
            eg_f = jnp.exp(u["gc_f"])
            eg_b = jnp.exp(u["gc_b"])
            kb_f = u["k"] * u["bb_f"]
            kb_b = u["k"] * u["bb_b"]
            u["rhs_f"] = jnp.concatenate([v * u["bb_f"], kb_f * eg_f], axis=1)
            u["rhs_b"] = jnp.concatenate([v * u["bb_b"], kb_b * eg_b], axis=1)
            u["r16_f"] = u["rhs_f"].astype(BF16)
            u["r16_b"] = u["rhs_b"].astype(BF16)
            u["qg_f"] = u["q"] * eg_f
            u["qg_b"] = u["q"] * eg_b
        uws = [_dot(tx.astype(BF16), jnp.concatenate(
                    [jnp.concatenate([u["r16_f"][cr], zeros_r], axis=1),
                     jnp.concatenate([zeros_r, u["r16_b"][cr]], axis=1)], axis=0))
               for (u, _), cr, tx in zip(chunks, crs, txs)]
        wu_f = [(u["rhs_f"][cr] + uw[:, 0:2 * LANE]) for (u, _), cr, uw in zip(chunks, crs, uws)]
        wu_b = [(u["rhs_b"][cr] + uw[:, 2 * LANE:4 * LANE]) for (u, _), cr, uw in zip(chunks, crs, uws)]
        wu16_f = [x.astype(BF16) for x in wu_f]
        wu16_b = [x.astype(BF16) for x in wu_b]
        aws = [_dot(a, jnp.concatenate([jnp.concatenate([xf, zeros_r], axis=1),
                                        jnp.concatenate([zeros_r, xb], axis=1)], axis=0))
               for a, xf, xb in zip(attn, wu16_f, wu16_b)]
        kg_f, kg_b, gls = [], [], []
        for (u, ci), cr in zip(chunks, crs):
            gl_f = u["gc_f"][ci * C + C - 1:ci * C + C, :]
            gl_b = u["gc_b"][ci * C:ci * C + 1, :]
            kg_f.append((u["k"][cr] * jnp.exp(gl_f - u["gc_f"][cr])).astype(BF16))
            kg_b.append((u["k"][cr] * jnp.exp(gl_b - u["gc_b"][cr])).astype(BF16))
            gls.append((jnp.exp(gl_f), jnp.exp(gl_b)))
        kwu_f = [_dot_tn(kg, x) for kg, x in zip(kg_f, wu16_f)]
        kwu_b = [_dot_tn(kg, x) for kg, x in zip(kg_b, wu16_b)]
        for idx, ((u, ci), cr) in enumerate(zip(chunks, crs)):
            hh = u["hh"]
            c = u["un"] * DN_UNIT + ci
            crow = pl.ds(_aligned(c * C, C), C)
            aw = aws[idx]
            for d, kwu, qg, a0 in ((0, kwu_f[idx], u["qg_f"], 0), (1, kwu_b[idx], u["qg_b"], 2 * LANE)):
                aq_ref[hh, d, c, 0:DN_DK, :] = kwu[:, LANE:2 * LANE].astype(BF16)
                aq_ref[hh, d, c, DN_DK:DN_DK + C, :] = (qg[cr] - aw[:, a0 + LANE:a0 + 2 * LANE]).astype(BF16)
                bm_ref[hh, d, c] = kwu[:, 0:LANE]
                oacc_ref[hh, d, crow, :] = aw[:, a0:a0 + LANE]
                gl_ref[hh, d, c] = jnp.broadcast_to(gls[idx][d], (8, LANE))
        return carry

    n_groups = T // (NU * UR)
    if n_groups == 1:
        local_group(0, 0)
    else:
        lax.fori_loop(0, n_groups, local_group, 0)

    def body(i, carry):
        cidx = (i, NC - 1 - i)
        rs = [_dot(aq_ref[hh, d, cidx[d]], carry[2 * hh + d].astype(BF16)) for hh in range(HB) for d in range(2)]
        new = []
        for hh in range(HB):
            for d in range(2):
                r = rs[2 * hh + d]
                c = cidx[d]
                rows = pl.ds(pl.multiple_of(c * C, C), C)
                oacc_ref[hh, d, rows, :] = oacc_ref[hh, d, rows, :] + r[DN_DK:DN_DK + C]
                new.append(carry[2 * hh + d] * gl_ref[hh, d, c, 0:1, :] + bm_ref[hh, d, c] - r[0:DN_DK])
        return tuple(new)

    if has_s0:
        init = tuple(s0_ref[0, 0, d, hh] for hh in range(HB) for d in range(2))
    else:
        init = tuple(jnp.zeros((DN_DK, LANE), F32) for _ in range(2 * HB))
    fin = lax.fori_loop(0, NC, body, init, unroll=NC <= DN_UNIT)
    if want_state:
        for hh in range(HB):
            for d in range(2):
                sfin_ref[0, 0, d, hh] = fin[2 * hh + d]

    for hh in range(HB):
        cols = slice(hh * LANE, (hh + 1) * LANE)
        for r0 in range(0, T, RB):
            o = oacc_ref[hh, 0, r0:r0 + RB, :] + oacc_ref[hh, 1, r0:r0 + RB, :]
            o = o * lax.rsqrt(jnp.mean(o * o, axis=-1, keepdims=True) + NORM_EPS) * gn_ref[0]
            y_ref[0, r0:r0 + RB, cols] = (o * _silu(z_ref[0, r0:r0 + RB, cols].astype(F32))).astype(BF16)


def _delta_mixer(proj3, bl3, l, conv_w, g_norm, s0, s_prev, HB, NU):
    B, T, _ = proj3.shape
    has_s0 = s0 is not None
    want_state = s_prev is not None
    NC = T // DN_CHUNK
    W = HB * LANE
    kern = functools.partial(_delta_kernel, T=T, HB=HB, NU=NU, has_s0=has_s0, want_state=want_state)
    col = lambda cb: pl.BlockSpec((1, T, W), lambda b, h, cb=cb: (b, 0, cb // HB + h))
    cw = lambda cb: pl.BlockSpec((1, 3, W), lambda b, h, cb=cb: (l, 0, cb // HB + h))
    in_specs = [col(CB_DN_Q), col(CB_DN_K), col(CB_DN_V), col(CB_DN_Z),
                pl.BlockSpec((1, T, LANE), lambda b, h: (b, 0, 0)),
                cw(0), cw(4), cw(8), pl.BlockSpec((1, 1, LANE), lambda b, h: (l, 0, 0))]
    args = [proj3, proj3, proj3, proj3, bl3, conv_w, conv_w, conv_w, g_norm]
    if has_s0:
        in_specs.append(pl.BlockSpec((1, 1, 2, HB, DN_DK, LANE), lambda b, h: (b, l, 0, h, 0, 0)))
        args.append(s0)
    aliases = {}
    if want_state:
        aliases = {len(args): 1}
        in_specs.append(pl.BlockSpec(memory_space=pl.ANY))
        args.append(s_prev)
    out_specs = [pl.BlockSpec((1, T, W), lambda b, h: (b, 0, h))]
    out_shape = [jax.ShapeDtypeStruct((B, T, DN_WIDTH), BF16)]
    if want_state:
        out_specs.append(pl.BlockSpec((1, 1, 2, HB, DN_DK, LANE), lambda b, h: (b, l, 0, h, 0, 0)))
        out_shape.append(jax.ShapeDtypeStruct(s_prev.shape, s_prev.dtype))
    res = pl.pallas_call(
        kern,
        grid=(B, DN_HEADS // HB),
        in_specs=in_specs,
        out_specs=out_specs,
        out_shape=out_shape,
        scratch_shapes=[
            pltpu.VMEM((T + 16, LANE), F32),
            pltpu.VMEM((HB, T, LANE), F32), pltpu.VMEM((HB, T, LANE), F32), pltpu.VMEM((HB, T, LANE), F32),
            pltpu.VMEM((HB, 2, NC, DN_DK + DN_CHUNK, LANE), BF16),
            pltpu.VMEM((HB, 2, NC, DN_DK, LANE), F32),
            pltpu.VMEM((HB, 2, T, LANE), F32),
            pltpu.VMEM((HB, 2, NC, 8, LANE), F32),
        ],
        input_output_aliases=aliases,
        compiler_params=_params(("parallel", "parallel")),
        name="delta_mixer",
    )(*args)
    return (res[0], res[1]) if want_state else (res[0], None)


def _head_masks(rows, dtype, value=1.0):
    lane = lax.broadcasted_iota(jnp.int32, (rows, LANE), 1)
    return (jnp.where(lane < NA_HD, value, 0.0).astype(dtype), jnp.where(lane >= NA_HD, value, 0.0).astype(dtype))


NA_SCALE = NA_HD ** -0.5


def _ctx_attn_kernel(q_ref, k_ref, v_ref, z_ref, y_ref, *, T):
    m0, m1 = _head_masks(T, BF16, NA_SCALE)
    f0, f1 = _head_masks(T, F32)
    pairs = range(NA_WIDTH // LANE)
    cols = [slice(p * LANE, (p + 1) * LANE) for p in pairs]
    s = [_dot_nt(q_ref[0, :, cols[p]] * hm, k_ref[0, :, cols[p]]) for p in pairs for hm in (m0, m1)]
    e = [jnp.exp(x - jnp.max(x, axis=-1, keepdims=True)) for x in s]
    o = [_dot(e[2 * p + hh].astype(BF16), v_ref[0, :, cols[p]]) for p in pairs for hh in range(2)]
    o = [x / jnp.sum(ee, axis=-1, keepdims=True) for x, ee in zip(o, e)]
    for p in pairs:
        out = o[2 * p] * f0 + o[2 * p + 1] * f1
        y_ref[0, :, cols[p]] = (out * _silu(z_ref[0, :, cols[p]].astype(F32))).astype(BF16)


def _ctx_attention(proj3):
    B, T, _ = proj3.shape
    wide = NA_WIDTH // LANE
    spec = lambda cb: pl.BlockSpec((1, T, NA_WIDTH), lambda b, cb=cb: (b, 0, cb // wide))
    return pl.pallas_call(
        functools.partial(_ctx_attn_kernel, T=T),
        grid=(B,),
        in_specs=[spec(CB_NA_Q), spec(CB_NA_K), spec(CB_NA_V), spec(CB_NA_Z)],
        out_specs=pl.BlockSpec((1, T, NA_WIDTH), lambda b: (b, 0, 0)),
        out_shape=jax.ShapeDtypeStruct((B, T, NA_WIDTH), BF16),
        compiler_params=_params(("parallel",)),
        name="ctx_attention",
    )(proj3, proj3, proj3, proj3)


def _na_kernel(q_ref, k_ref, v_ref, z_ref, kc_ref, vc_ref, bias_ref, y_ref, sctx_ref, *, T, L):
    W = GRID_W
    rows_total = T // W
    n_win = NA_WIN_R * W
    RS = NA_ROWS_PER_STEP
    m0, m1 = _head_masks(W, BF16, NA_SCALE)
    f0, f1 = _head_masks(W, F32)
    kcb = kc_ref[0, 0].astype(BF16)
    vcb = vc_ref[0, 0].astype(BF16)

    RB = 256
    mb0, mb1 = _head_masks(RB, BF16, NA_SCALE)
    for r0 in range(0, T, RB):
        q = q_ref[0, r0:r0 + RB, :]
        sctx_ref[0, r0:r0 + RB, :] = _dot_nt(q * mb0, kcb)
        sctx_ref[1, r0:r0 + RB, :] = _dot_nt(q * mb1, kcb)

    def step(i, carry):
        rr = [i * RS + j for j in range(RS)]
        rs_ = [jnp.clip(r - NA_WIN_R // 2, 0, rows_total - NA_WIN_R) for r in rr]
        qrows = [pl.ds(pl.multiple_of(r * W, W), W) for r in rr]
        krows = [pl.ds(pl.multiple_of(rs * W, W), n_win) for rs in rs_]
        qs = []
        for qr in qrows:
            q = q_ref[0, qr, :]
            qs.append(jnp.concatenate([q * m0, q * m1], axis=0))
        s = [_dot_nt(x, k_ref[0, kr, :]) for x, kr in zip(qs, krows)]
        e1, e2, den = [], [], []
        for j in range(RS):
            dr0 = rs_[j] - rr[j] + NA_WIN_R - 1
            bias = jnp.concatenate(
                [jnp.concatenate([bias_ref[0, hh, dr0 + 2 * m] for m in range(NA_WIN_R // 2)], axis=1)
                 for hh in range(2)], axis=0)
            sl = s[j] + bias
            sc = jnp.concatenate([sctx_ref[0, qrows[j], :], sctx_ref[1, qrows[j], :]], axis=0)
            mx = jnp.maximum(jnp.max(sl, axis=-1, keepdims=True), jnp.max(sc, axis=-1, keepdims=True))
            a = jnp.exp(sl - mx)
            b = jnp.exp(sc - mx)
            e1.append(a.astype(BF16))
            e2.append(b.astype(BF16))
            den.append(jnp.sum(a, axis=-1, keepdims=True) + jnp.sum(b, axis=-1, keepdims=True))
        o = [_dot(a, v_ref[0, kr, :]) + _dot(b, vcb) for a, b, kr in zip(e1, e2, krows)]
        for j in range(RS):
            oj = o[j] / den[j]
            out = oj[0:W, :] * f0 + oj[W:2 * W, :] * f1
            y_ref[0, qrows[j], :] = (out * _silu(z_ref[0, qrows[j], :].astype(F32))).astype(BF16)
        return carry

    lax.fori_loop(0, rows_total // RS, step, 0)


def _neighbourhood_attention(proj3, l, k_ctx, v_ctx, bias2):
    B, T, _ = proj3.shape
    L = k_ctx.shape[2]
    col = lambda cb: pl.BlockSpec((1, T, LANE), lambda b, p, cb=cb: (b, 0, cb + p))
    ctx = pl.BlockSpec((1, 1, L, LANE), lambda b, p: (b, l, 0, p))
    return pl.pallas_call(
        functools.partial(_na_kernel, T=T, L=L),
        grid=(B, NA_WIDTH // LANE),
        in_specs=[col(CB_NA_Q), col(CB_NA_K), col(CB_NA_V), col(CB_NA_Z), ctx, ctx,
                  pl.BlockSpec((1, 2, 2 * NA_WIN_R - 2, GRID_W, LANE), lambda b, p: (l, p, 0, 0, 0))],
        out_specs=pl.BlockSpec((1, T, LANE), lambda b, p: (b, 0, p)),
        out_shape=jax.ShapeDtypeStruct((B, T, NA_WIDTH), BF16),
        scratch_shapes=[pltpu.VMEM((2, T, L), F32)],
        compiler_params=_params(("parallel", "parallel")),
        name="nbr_attention",
    )(proj3, proj3, proj3, proj3, k_ctx, v_ctx, bias2)


def _pool_kernel(u_ref, z_ref, band_ref, pw_ref, ps_ref, y_ref, x_buf, *, T):
    P = POOL_HALO
    RB = POOL_ROWS
    zeros = jnp.zeros((P, POOL_WIDTH), BF16)
    x_buf[0:P, :] = zeros
    x_buf[T + P:T + 2 * P, :] = zeros
    for r0 in range(0, T, RB):
        x_buf[P + r0:P + r0 + RB, :] = u_ref[0, r0:r0 + RB, :]
    groups = list(enumerate(POOL_WINDOWS))
    cols = [slice(g * POOL_GC, (g + 1) * POOL_GC) for g, _ in groups]
    for r0 in range(0, T, RB):
        t = r0 + lax.broadcasted_iota(jnp.int32, (RB, 1), 0)
        tots = [_dot(band_ref[g], x_buf[r0:r0 + RB + 2 * P, cols[g]]) for g, _ in groups]
        pooled = []
        for g, win in groups:
            lo = jnp.maximum(t - win // 2, 0)
            hi = jnp.minimum(t + win // 2 - 1, T - 1)
            inv_cnt = 1.0 / (hi - lo + 1).astype(F32)
            pooled.append((tots[g] * inv_cnt - u_ref[0, r0:r0 + RB, cols[g]].astype(F32)).astype(BF16))
        ys = [_dot(pooled[g], pw_ref[0, g]) for g, _ in groups]
        for g, _ in groups:
            y = ys[g] * ps_ref[0, :, cols[g]]
            y_ref[0, r0:r0 + RB, cols[g]] = (y * _silu(z_ref[0, r0:r0 + RB, cols[g]].astype(F32))).astype(BF16)


def _pool_bands():
    off = np.arange(POOL_ROWS + 2 * POOL_HALO)[None, :] - POOL_HALO - np.arange(POOL_ROWS)[:, None]
    return jnp.asarray(np.stack([(off >= -(w // 2)) & (off <= w // 2 - 1) for w in POOL_WINDOWS]), BF16)


def _pool_mixer(proj3, l, pool_w, pool_scale):
    B, T, _ = proj3.shape
    wide = POOL_WIDTH // LANE
    bands = _pool_bands()
    return pl.pallas_call(
        functools.partial(_pool_kernel, T=T),
        grid=(B,),
        in_specs=[
            pl.BlockSpec((1, T, POOL_WIDTH), lambda b: (b, 0, CB_PL_U // wide)),
            pl.BlockSpec((1, T, POOL_WIDTH), lambda b: (b, 0, CB_PL_Z // wide)),
            pl.BlockSpec(bands.shape, lambda b: (0, 0, 0)),
            pl.BlockSpec((1, len(POOL_WINDOWS), POOL_GC, POOL_GC), lambda b: (l, 0, 0, 0)),
            pl.BlockSpec((1, 1, POOL_WIDTH), lambda b: (l, 0, 0)),
        ],
        out_specs=pl.BlockSpec((1, T, POOL_WIDTH), lambda b: (b, 0, 0)),
        out_shape=jax.ShapeDtypeStruct((B, T, POOL_WIDTH), BF16),
        scratch_shapes=[pltpu.VMEM((T + 2 * POOL_HALO, POOL_WIDTH), BF16)],
        compiler_params=_params(("parallel",)),
        name="pool_mixer",
    )(proj3, proj3, bands, pool_w, pool_scale)


def _merge_kernel(ydn_ref, yna_ref, ypl_ref, gdn_ref, gna_ref, gpl_ref, x_ref, mod_ref, gpost_ref,
                  wd_ref, wn_ref, wp_ref, wo_ref, o_ref, *, tiles_per_batch, mod_row0):
    merged = (_sigmoid(gdn_ref[...].astype(F32)) * _dot(ydn_ref[...], wd_ref[0])
              + _sigmoid(gna_ref[...].astype(F32)) * _dot(yna_ref[...], wn_ref[0])
              + _sigmoid(gpl_ref[...].astype(F32)) * _dot(ypl_ref[...], wp_ref[0]))
    out = _dot(merged.astype(BF16), wo_ref[0])
    out = out * lax.rsqrt(jnp.mean(out * out, axis=-1, keepdims=True) + NORM_EPS) * gpost_ref[0]
    gate = _mod_row(mod_ref, tiles_per_batch, mod_row0)[:, 2 * D_MODEL:3 * D_MODEL]
    o_ref[...] = x_ref[...] + gate * out


def _merge(y_dn, y_na, y_pl, proj2d, x2d, l, mod, g_post, w_dn, w_na, w_pl, w_out, tiles_per_batch, mod_row0, tm):
    ntok = x2d.shape[0]
    g0 = CB_GATE * LANE // D_MODEL
    br = pl.BlockSpec((tm, DN_WIDTH), lambda m: (m, 0))
    gate = lambda j: pl.BlockSpec((tm, D_MODEL), lambda m, j=j: (m, g0 + j))
    wbr = pl.BlockSpec((1, DN_WIDTH, D_MODEL), lambda m: (l, 0, 0))
    return pl.pallas_call(
        functools.partial(_merge_kernel, tiles_per_batch=tiles_per_batch, mod_row0=mod_row0),
        grid=(ntok // tm,),
        in_specs=[br, br, br, gate(0), gate(1), gate(2),
                  pl.BlockSpec((tm, D_MODEL), lambda m: (m, 0)),
                  pl.BlockSpec((1, 8, 3 * D_MODEL), lambda m: (l, 0, 0)),
                  pl.BlockSpec((1, 1, D_MODEL), lambda m: (l, 0, 0)),
                  wbr, wbr, wbr,
                  pl.BlockSpec((1, D_MODEL, D_MODEL), lambda m: (l, 0, 0))],
        out_specs=pl.BlockSpec((tm, D_MODEL), lambda m: (m, 0)),
        out_shape=jax.ShapeDtypeStruct((ntok, D_MODEL), F32),
        compiler_params=_params(("parallel",)),
        name="merge_out",
    )(y_dn, y_na, y_pl, proj2d, proj2d, proj2d, x2d, mod, g_post, w_dn, w_na, w_pl, w_out)


def _layer(x3, l, mod_row0, s0, k_ctx, v_ctx, pw, tm_proj, tm_merge, delta_cfg, carry=None):
    B, T, _ = x3.shape
    n = B * T
    x2d = x3.reshape(n, D_MODEL)
    is_ctx = k_ctx is None
    res = _project(x2d, l, pw["mod"], pw["g_pre"], pw["w_main"], pw["w_ba"], pw["alog_v"], pw["dtb_v"],
                   None if is_ctx else T // tm_proj, mod_row0,
                   kv=carry[:2] if is_ctx else None, tm=tm_proj)
    proj2d, bl2d = res[0], res[1]
    proj3 = proj2d.reshape(B, T, P_MAIN)
    y_dn, s_fin = _delta_mixer(proj3, bl2d.reshape(B, T, LANE), l, pw["conv_w"], pw["g_norm"], s0,
                               s_prev=carry[2] if is_ctx else None,
                               HB=delta_cfg[0], NU=delta_cfg[1])
    if is_ctx:
        y_na = _ctx_attention(proj3)
    else:
        y_na = _neighbourhood_attention(proj3, l, k_ctx, v_ctx, pw["bias2"])
    y_pl = _pool_mixer(proj3, l, pw["pool_w"], pw["pool_scale"])
    out2d = _merge(y_dn.reshape(n, DN_WIDTH), y_na.reshape(n, NA_WIDTH), y_pl.reshape(n, POOL_WIDTH), proj2d,
                   x2d, l, pw["mod"], pw["g_post"], pw["w_dn"], pw["w_na"], pw["w_pl"], pw["w_out"],
                   None if is_ctx else T // tm_merge, mod_row0, tm=tm_merge)
    return out2d.reshape(B, T, D_MODEL), (tuple(res[2:]) + (s_fin,) if is_ctx else None)


def _lane_vecs(v, offset):
    return jnp.zeros((DEPTH, 1, LANE), F32).at[:, 0, offset:offset + 2 * DN_HEADS].set(v.reshape(DEPTH, -1))


def _bias_windows(na_bias):
    period = 2 * GRID_W - 1
    half = NA_WIN_C - 1
    zeros = jnp.zeros(na_bias.shape[:-1] + (period - 2 * half - 1,), F32)
    r = jnp.concatenate([na_bias[..., half:], zeros, na_bias[..., :half]], axis=-1)
    rows = jnp.tile(r, (1, 1, 1, GRID_W))[..., :GRID_W * (period - 1)]
    bias_t = rows.reshape(na_bias.shape[:-1] + (GRID_W, period - 1))[..., :GRID_W]
    cols = np.arange(GRID_W)
    cs = np.clip(cols - NA_WIN_C // 2, 0, GRID_W - NA_WIN_C)[:, None]
    col_ok = (cols[None, :] >= cs) & (cols[None, :] < cs + NA_WIN_C)
    bias_t = jnp.where(col_ok, bias_t, NEG_INF)
    return jnp.concatenate([bias_t[:, :, :-1], bias_t[:, :, 1:]], axis=-1)


def kernel(x_prompt, x_sample, c, cache_k_na, cache_v_na, state_dn, c_ctx, w_ada, b_ada, g_pre, g_post, w_in,
           conv_dn, a_log_dn, dt_bias_dn, g_norm_dn, na_bias, pool_w, pool_scale, w_br_dn, w_br_na, w_br_pl,
           w_out):
    B, T, _ = x_prompt.shape
    DB, DT, _ = x_sample.shape
    L = cache_k_na.shape[2]

    cc = jnp.zeros((8, D_MODEL), F32).at[0].set(c_ctx).at[1:1 + DB].set(c)
    w_main, w_ba = _regroup_weights(w_in)
    pw = dict(
        mod=_modulation(cc, w_ada, b_ada),
        g_pre=g_pre[:, None], g_post=g_post[:, None], g_norm=g_norm_dn[:, None], pool_scale=pool_scale[:, None],
        w_main=w_main, w_ba=w_ba,
        alog_v=_lane_vecs(a_log_dn, 2 * DN_HEADS), dtb_v=_lane_vecs(dt_bias_dn, 2 * DN_HEADS),
        conv_w=conv_dn, bias2=_bias_windows(na_bias), pool_w=pool_w.astype(BF16),
        w_dn=w_br_dn.astype(BF16), w_na=w_br_na.astype(BF16), w_pl=w_br_pl.astype(BF16), w_out=w_out.astype(BF16))
    k_ctx = cache_k_na.reshape(DB, DEPTH, L, NA_WIDTH)
    v_ctx = cache_v_na.reshape(DB, DEPTH, L, NA_WIDTH)

    xp, xs = x_prompt, x_sample
    carry = (jnp.zeros((B, DEPTH, T, NA_WIDTH), F32), jnp.zeros((B, DEPTH, T, NA_WIDTH), F32),
             jnp.zeros((B, DEPTH, 2, DN_HEADS, DN_DK, LANE), F32))
    for l in range(DEPTH):
        xp, carry = _layer(xp, l, 0, None, None, None, pw, tm_proj=1024, tm_merge=512, delta_cfg=(4, 1),
                           carry=carry)
        xs, _ = _layer(xs, l, 1, state_dn, k_ctx, v_ctx, pw, tm_proj=1024, tm_merge=512, delta_cfg=(2, 2))
    k_all, v_all, s_all = carry
    return (xp, xs, k_all.reshape(B, DEPTH, T, NA_HEADS, NA_HD), v_all.reshape(B, DEPTH, T, NA_HEADS, NA_HD), s_all)
```

```python
import functools

import numpy as np
import jax
import jax.numpy as jnp
from jax import lax
from jax.experimental import pallas as pl
from jax.experimental.pallas import tpu as pltpu

F32 = jnp.float32
BF16 = jnp.bfloat16

D_MODEL = 1024
DEPTH = 4
GRID_W = 64
NORM_EPS = 1e-6
NEG_INF = -1e30

DN_HEADS = 4
DN_DK = 128
DN_WIDTH = 512
DN_CHUNK = 64
DN_BLK_SHIFT = 4
DN_UNIT = 4

NA_HEADS = 8
NA_HD = 64
NA_WIDTH = 512
NA_WIN_R = 8
NA_WIN_C = 16
NA_ROWS_PER_STEP = 8

POOL_WINDOWS = (2, 4, 8, 16)
POOL_GC = 128
POOL_WIDTH = 512
POOL_HALO = 16
POOL_ROWS = 256

LANE = 128
P_MAIN = 8192
CB_DN_Q, CB_DN_K, CB_DN_V, CB_DN_Z = 0, 4, 8, 12
CB_NA_K, CB_NA_V = 16, 20
CB_NA_Q, CB_NA_Z = 24, 28
CB_PL_U, CB_PL_Z = 32, 36
CB_GATE = 40
SRC_COLS = ((0, 2048), (2576, 3600), (2064, 2576), (3600, 8208))
OFF_BA_SRC = 2048
N_BA = 16

VMEM_LIMIT = 56 * 1024 * 1024


def _sigmoid(x):
    return 0.5 * jnp.tanh(0.5 * x) + 0.5


def _silu(x):
    return x * _sigmoid(x)


def _softplus(x):
    return jnp.maximum(x, 0.0) + jnp.log1p(jnp.exp(-jnp.abs(x)))


def _dot(a, b):
    return jnp.dot(a, b, preferred_element_type=F32)


def _dot_nt(a, b):
    return lax.dot_general(a, b, (((1,), (1,)), ((), ())), preferred_element_type=F32)


def _dot_tn(a, b):
    return lax.dot_general(a, b, (((0,), (0,)), ((), ())), preferred_element_type=F32)


def _aligned(x, m):
    return x if isinstance(x, int) else pl.multiple_of(x, m)


def _params(sem):
    return pltpu.CompilerParams(dimension_semantics=sem, vmem_limit_bytes=VMEM_LIMIT)


def _mod_kernel(cc_ref, w_ref, b_ref, o_ref):
    a = _silu(cc_ref[...]).astype(BF16)
    o_ref[0] = _dot(a, w_ref[0].astype(BF16)) + b_ref[0]


def _modulation(cc, w_ada, b_ada):
    tn = 1024
    return pl.pallas_call(
        _mod_kernel,
        grid=(DEPTH, 3 * D_MODEL // tn),
        in_specs=[
            pl.BlockSpec((8, D_MODEL), lambda l, n: (0, 0)),
            pl.BlockSpec((1, D_MODEL, tn), lambda l, n: (l, 0, n)),
            pl.BlockSpec((1, 1, tn), lambda l, n: (l, 0, n)),
        ],
        out_specs=pl.BlockSpec((1, 8, tn), lambda l, n: (l, 0, n)),
        out_shape=jax.ShapeDtypeStruct((DEPTH, 8, 3 * D_MODEL), F32),
        compiler_params=_params(("parallel", "parallel")),
        name="adaln_mod",
    )(cc, w_ada, b_ada.reshape(DEPTH, 1, 3 * D_MODEL))


def _regroup_kernel(w_ref, o_ref, ba_ref):
    dst = 0
    step = 4 * LANE
    for a, b in SRC_COLS:
        for c0 in range(a, b, step):
            n = min(step, b - c0)
            o_ref[0, :, dst:dst + n] = w_ref[0, :, c0:c0 + n].astype(BF16)
            dst += n
    ba = w_ref[0, :, OFF_BA_SRC:OFF_BA_SRC + LANE]
    lane = lax.broadcasted_iota(jnp.int32, ba.shape, 1)
    ba_ref[0] = jnp.where(lane < N_BA, ba, 0.0).astype(BF16)


def _regroup_weights(w_in, rows=256):
    depth, d_in, n_in = w_in.shape
    return pl.pallas_call(
        _regroup_kernel,
        grid=(depth, d_in // rows),
        in_specs=[pl.BlockSpec((1, rows, n_in), lambda l, r: (l, r, 0))],
        out_specs=[pl.BlockSpec((1, rows, P_MAIN), lambda l, r: (l, r, 0)),
                   pl.BlockSpec((1, rows, LANE), lambda l, r: (l, r, 0))],
        out_shape=[jax.ShapeDtypeStruct((depth, d_in, P_MAIN), BF16),
                   jax.ShapeDtypeStruct((depth, d_in, LANE), BF16)],
        compiler_params=_params(("parallel", "parallel")),
        name="regroup_weights",
    )(w_in)


def _mod_row(mod_ref, tiles_per_batch, mod_row0):
    if tiles_per_batch is None:
        return mod_ref[0, 0:1, :]
    row = mod_row0 + pl.program_id(0) // tiles_per_batch
    return mod_ref[0, pl.ds(row, 1), :]


def _proj_kernel(*refs, tiles_per_batch, mod_row0, kv_tile, n_alias):
    x_ref, mod_ref, g_ref, w_ref, wba_ref, alog_ref, dtb_ref = refs[:7]
    o_ref, bl_ref = refs[7 + n_alias:9 + n_alias]
    k_ref, v_ref = refs[9 + n_alias:11 + n_alias] if kv_tile is not None else (None, None)
    h_ref = refs[-1]

    @pl.when(pl.program_id(1) == 0)
    def _():
        mod = _mod_row(mod_ref, tiles_per_batch, mod_row0)
        shift = mod[:, 0:D_MODEL]
        scale = mod[:, D_MODEL:2 * D_MODEL]
        x = x_ref[...]
        y = x * lax.rsqrt(jnp.mean(x * x, axis=-1, keepdims=True) + NORM_EPS) * g_ref[0]
        hb = (y * (1.0 + scale) + shift).astype(BF16)
        h_ref[...] = hb
        ba = _dot(hb, wba_ref[0])
        lane = lax.broadcasted_iota(jnp.int32, ba.shape, 1)
        bl_ref[...] = jnp.where(lane < 2 * DN_HEADS, _sigmoid(ba),
                                -jnp.exp(alog_ref[0]) * _softplus(ba + dtb_ref[0]))

    acc = _dot(h_ref[...], w_ref[0])
    o_ref[...] = acc.astype(BF16)
    if kv_tile is not None:
        @pl.when(pl.program_id(1) == kv_tile)
        def _():
            seqs, _, t_len, _ = k_ref.shape
            k_ref[:, 0] = acc[:, 0:NA_WIDTH].reshape(seqs, t_len, NA_WIDTH)
            v_ref[:, 0] = acc[:, NA_WIDTH:2 * NA_WIDTH].reshape(seqs, t_len, NA_WIDTH)


def _project(x2d, l, mod, g_pre, w_main, w_ba, alog_v, dtb_v, tiles_per_batch, mod_row0, kv, tm, tn=2048):
    ntok = x2d.shape[0]
    kv_tile = (CB_NA_K * LANE) // tn if kv is not None else None
    prev = [] if kv is None else list(kv)
    kern = functools.partial(_proj_kernel, tiles_per_batch=tiles_per_batch, mod_row0=mod_row0, kv_tile=kv_tile,
                             n_alias=len(prev))
    vec = pl.BlockSpec((1, 1, LANE), lambda m, n: (l, 0, 0))
    in_specs = [
        pl.BlockSpec((tm, D_MODEL), lambda m, n: (m, 0)),
        pl.BlockSpec((1, 8, 3 * D_MODEL), lambda m, n: (l, 0, 0)),
        pl.BlockSpec((1, 1, D_MODEL), lambda m, n: (l, 0, 0)),
        pl.BlockSpec((1, D_MODEL, tn), lambda m, n: (l, 0, n)),
        pl.BlockSpec((1, D_MODEL, LANE), lambda m, n: (l, 0, 0)),
        vec, vec,
    ] + [pl.BlockSpec(memory_space=pl.ANY)] * len(prev)
    out_specs = [pl.BlockSpec((tm, tn), lambda m, n: (m, n)),
                 pl.BlockSpec((tm, LANE), lambda m, n: (m, 0))]
    out_shape = [jax.ShapeDtypeStruct((ntok, P_MAIN), BF16),
                 jax.ShapeDtypeStruct((ntok, LANE), F32)]
    aliases = {}
    if kv is not None:
        t_len = prev[0].shape[2]
        out_specs += [pl.BlockSpec((tm // t_len, 1, t_len, NA_WIDTH), lambda m, n: (m, l, 0, 0))] * 2
        out_shape += [jax.ShapeDtypeStruct(p.shape, p.dtype) for p in prev]
        aliases = {7 + i: 2 + i for i in range(len(prev))}
    return pl.pallas_call(
        kern,
        grid=(ntok // tm, P_MAIN // tn),
        in_specs=in_specs,
        out_specs=out_specs,
        out_shape=out_shape,
        scratch_shapes=[pltpu.VMEM((tm, D_MODEL), BF16)],
        input_output_aliases=aliases,
        compiler_params=_params(("parallel", "arbitrary")),
        name="in_proj",
    )(x2d, mod, g_pre, w_main, w_ba, alog_v, dtb_v, *prev)


def _delta_kernel(*refs, T, HB, NU, has_s0, want_state):
    it = iter(refs)
    q_ref, k_ref, v_ref, z_ref, bl_ref = (next(it) for _ in range(5))
    cwq_ref, cwk_ref, cwv_ref, gn_ref = (next(it) for _ in range(4))
    s0_ref = next(it) if has_s0 else None
    if want_state:
        next(it)
    y_ref = next(it)
    sfin_ref = next(it) if want_state else None
    pad_ref, qn_ref, kn_ref, vn_ref, aq_ref, bm_ref, oacc_ref, gl_ref = (next(it) for _ in range(8))

    C = DN_CHUNK
    NC = T // C
    RB = min(T, 256)
    UR = DN_UNIT * C
    head0 = pl.program_id(1) * HB

    zeros8 = jnp.zeros((8, LANE), F32)
    pad_ref[0:8, :] = zeros8
    pad_ref[T + 8:T + 16, :] = zeros8

    def conv_into(x_ref, cw_ref, dst_ref, hh, mode):
        cols = slice(hh * LANE, (hh + 1) * LANE)
        for r0 in range(0, T, RB):
            pad_ref[8 + r0:8 + r0 + RB, :] = x_ref[0, r0:r0 + RB, cols].astype(F32)
        w = cw_ref[0, :, cols]
        for r0 in range(0, T, RB):
            y = (pad_ref[7 + r0:7 + r0 + RB, :] * w[0:1, :]
                 + pad_ref[8 + r0:8 + r0 + RB, :] * w[1:2, :]
                 + pad_ref[9 + r0:9 + r0 + RB, :] * w[2:3, :])
            y = _silu(y)
            if mode != "v":
                y = y * lax.rsqrt(jnp.sum(y * y, axis=-1, keepdims=True) + NORM_EPS)
            if mode == "q":
                y = y * (DN_DK ** -0.5)
            dst_ref[hh, r0:r0 + RB, :] = y

    for hh in range(HB):
        conv_into(q_ref, cwq_ref, qn_ref, hh, "q")
        conv_into(k_ref, cwk_ref, kn_ref, hh, "k")
        conv_into(v_ref, cwv_ref, vn_ref, hh, "v")

    u_r = lax.broadcasted_iota(jnp.int32, (UR, LANE), 0) & (C - 1)
    u_c = lax.broadcasted_iota(jnp.int32, (UR, LANE), 1)
    is_b = u_c >= C
    u_j = u_c & (C - 1)
    dist = jnp.where(is_b, u_j - u_r, u_r - u_j)
    incl_u = dist >= 0
    strict_u = dist > 0
    rowsum_u = dist <= 0
    blk = (jnp.right_shift(u_r, DN_BLK_SHIFT) == jnp.right_shift(u_j, DN_BLK_SHIFT))[0:C, :]
    bd_r = lax.broadcasted_iota(jnp.int32, (UR, UR), 0)
    bd_c = lax.broadcasted_iota(jnp.int32, (UR, UR), 1)
    same_chunk = jnp.right_shift(bd_r, 6) == jnp.right_shift(bd_c, 6)
    bd_lower = (same_chunk & (bd_r >= bd_c)).astype(BF16)
    bd_upper = (same_chunk & (bd_r <= bd_c)).astype(BF16)
    bd_ones = same_chunk.astype(BF16)
    half_l = (lax.broadcasted_iota(jnp.int32, (C, LANE), 1) < C)

    def split_hi_mid(x):
        hi = x.astype(BF16)
        mid = (x - hi.astype(F32)).astype(BF16)
        return jnp.concatenate([hi, mid], axis=1)

    def dot01(m01, parts):
        r = _dot(m01, parts)
        return r[:, 0:LANE] + r[:, LANE:2 * LANE]

    def block_diag(x):
        xb = x.astype(BF16)
        zb = jnp.zeros_like(xb)
        return jnp.concatenate([jnp.where(half_l, xb, zb), jnp.where(half_l, zb, xb)], axis=0)

    def mm2s(xs, ys):
        return [_dot(x.astype(BF16), block_diag(y)) for x, y in zip(xs, ys)]

    def tri_inv_minus_eye(lms):
        md = [jnp.where(blk, -lm, 0.0) for lm in lms]
        md2 = mm2s(md, md)
        md4 = mm2s(md2, md2)
        md3 = mm2s(md, md2)
        md8 = mm2s(md4, md4)
        px = [a + b + c for a, b, c in zip(md, md2, md3)]
        md12 = mm2s(md4, md8)
        qx = [a + b + c for a, b, c in zip(md4, md8, md12)]
        pq = mm2s(px, qx)
        dx = [a + b + c for a, b, c in zip(px, qx, pq)]
        loff = [jnp.where(blk, 0.0, lm) for lm in lms]
        dl = mm2s(dx, loff)
        n1 = [a + b for a, b in zip(loff, dl)]
        n2 = mm2s(n1, n1)
        n3 = mm2s(n1, n2)
        rx = [b - a - c for a, b, c in zip(n1, n2, n3)]
        rd = mm2s(rx, dx)
        return [a + b + c for a, b, c in zip(rx, dx, rd)]

    lane_u = u_c

    def pick(x, idx):
        return jnp.broadcast_to(jnp.sum(jnp.where(lane_u == idx, x, 0.0), axis=1, keepdims=True), x.shape)

    zeros_r = jnp.zeros((C, 2 * LANE), BF16)

    def local_group(g, carry):
        units = [(hh, g * NU + s) for hh in range(HB) for s in range(NU)]
        U = []
        for hh, un in units:
            rows = pl.ds(_aligned(un * UR, UR), UR)
            bl = bl_ref[0, rows, :]
            hg = head0 + hh
            U.append(dict(hh=hh, un=un, rows=rows,
                          bb_f=pick(bl, hg), bb_b=pick(bl, DN_HEADS + hg),
                          la_f=pick(bl, 2 * DN_HEADS + hg), la_b=pick(bl, 3 * DN_HEADS + hg)))
        for u in U:
            u["sf"] = split_hi_mid(u["la_f"])
            u["sb"] = split_hi_mid(u["la_b"])
            u["sr"] = split_hi_mid(jnp.where(rowsum_u, jnp.where(is_b, u["la_b"], u["la_f"]), 0.0))
        for u in U:
            u["gc_f"] = dot01(bd_lower, u["sf"])
        for u in U:
            u["gc_b"] = dot01(bd_upper, u["sb"])
        for u in U:
            u["g_row"] = dot01(bd_ones, u["sr"])
        for u in U:
            hh, rows = u["hh"], u["rows"]
            u["k"] = kn_ref[hh, rows, :]
            u["q"] = qn_ref[hh, rows, :]
            u["k16"] = u["k"].astype(BF16)
            u["q16"] = u["q"].astype(BF16)
        chunks = [(u, ci) for u in U for ci in range(DN_UNIT)]
        crs = [slice(ci * C, (ci + 1) * C) for _, ci in chunks]
        grams = [_dot_nt(jnp.concatenate([u["k16"][cr], u["q16"][cr]], axis=0),
                         jnp.concatenate([u["k16"][cr], u["k16"][cr]], axis=0))
                 for (u, _), cr in zip(chunks, crs)]
        for u in U:
            u["decay"] = jnp.exp(jnp.where(incl_u, jnp.where(is_b, u["gc_b"], u["gc_f"]) - u["g_row"], NEG_INF))
            u["bb_p"] = jnp.where(is_b, u["bb_b"], u["bb_f"])
        lms = [jnp.where(strict_u[cr], gm[0:C] * u["bb_p"][cr] * u["decay"][cr], 0.0)
               for (u, _), cr, gm in zip(chunks, crs, grams)]
        attn = [(gm[C:2 * C] * u["decay"][cr]).astype(BF16) for (u, _), cr, gm in zip(chunks, crs, grams)]
        txs = tri_inv_minus_eye(lms)
        for u in U:
            v = vn_ref[u["hh"], u["rows"], :]
            eg_f = jnp.exp(u["gc_f"])
            eg_b = jnp.exp(u["gc_b"])
            kb_f = u["k"] * u["bb_f"]
            kb_b = u["k"] * u["bb_b"]
            u["rhs_f"] = jnp.concatenate([v * u["bb_f"], kb_f * eg_f], axis=1)
            u["rhs_b"] = jnp.concatenate([v * u["bb_b"], kb_b * eg_b], axis=1)
            u["r16_f"] = u["rhs_f"].astype(BF16)
            u["r16_b"] = u["rhs_b"].astype(BF16)
            u["qg_f"] = u["q"] * eg_f
            u["qg_b"] = u["q"] * eg_b
        uws = [_dot(tx.astype(BF16), jnp.concatenate(
                    [jnp.concatenate([u["r16_f"][cr], zeros_r], axis=1),
                     jnp.concatenate([zeros_r, u["r16_b"][cr]], axis=1)], axis=0))
               for (u, _), cr, tx in zip(chunks, crs, txs)]
        wu_f = [(u["rhs_f"][cr] + uw[:, 0:2 * LANE]) for (u, _), cr, uw in zip(chunks, crs, uws)]
        wu_b = [(u["rhs_b"][cr] + uw[:, 2 * LANE:4 * LANE]) for (u, _), cr, uw in zip(chunks, crs, uws)]
        wu16_f = [x.astype(BF16) for x in wu_f]
        wu16_b = [x.astype(BF16) for x in wu_b]
        aws = [_dot(a, jnp.concatenate([jnp.concatenate([xf, zeros_r], axis=1),
                                        jnp.concatenate([zeros_r, xb], axis=1)], axis=0))
               for a, xf, xb in zip(attn, wu16_f, wu16_b)]
        kg_f, kg_b, gls = [], [], []
        for (u, ci), cr in zip(chunks, crs):
            gl_f = u["gc_f"][ci * C + C - 1:ci * C + C, :]
            gl_b = u["gc_b"][ci * C:ci * C + 1, :]
            kg_f.append((u["k"][cr] * jnp.exp(gl_f - u["gc_f"][cr])).astype(BF16))
            kg_b.append((u["k"][cr] * jnp.exp(gl_b - u["gc_b"][cr])).astype(BF16))
            gls.append((jnp.exp(gl_f), jnp.exp(gl_b)))
        kwu_f = [_dot_tn(kg, x) for kg, x in zip(kg_f, wu16_f)]
        kwu_b = [_dot_tn(kg, x) for kg, x in zip(kg_b, wu16_b)]
        for idx, ((u, ci), cr) in enumerate(zip(chunks, crs)):
            hh = u["hh"]
            c = u["un"] * DN_UNIT + ci
            crow = pl.ds(_aligned(c * C, C), C)
            aw = aws[idx]
            for d, kwu, qg, a0 in ((0, kwu_f[idx], u["qg_f"], 0), (1, kwu_b[idx], u["qg_b"], 2 * LANE)):
                aq_ref[hh, d, c, 0:DN_DK, :] = kwu[:, LANE:2 * LANE].astype(BF16)
                aq_ref[hh, d, c, DN_DK:DN_DK + C, :] = (qg[cr] - aw[:, a0 + LANE:a0 + 2 * LANE]).astype(BF16)
                bm_ref[hh, d, c] = kwu[:, 0:LANE]
                oacc_ref[hh, d, crow, :] = aw[:, a0:a0 + LANE]
                gl_ref[hh, d, c] = jnp.broadcast_to(gls[idx][d], (8, LANE))
        return carry

    n_groups = T // (NU * UR)
    if n_groups == 1:
        local_group(0, 0)
    else:
        lax.fori_loop(0, n_groups, local_group, 0)

    def body(i, carry):
        cidx = (i, NC - 1 - i)
        rs = [_dot(aq_ref[hh, d, cidx[d]], carry[2 * hh + d].astype(BF16)) for hh in range(HB) for d in range(2)]
        new = []
        for hh in range(HB):
            for d in range(2):
                r = rs[2 * hh + d]
                c = cidx[d]
                rows = pl.ds(pl.multiple_of(c * C, C), C)
                oacc_ref[hh, d, rows, :] = oacc_ref[hh, d, rows, :] + r[DN_DK:DN_DK + C]
                new.append(carry[2 * hh + d] * gl_ref[hh, d, c, 0:1, :] + bm_ref[hh, d, c] - r[0:DN_DK])
        return tuple(new)

    if has_s0:
        init = tuple(s0_ref[0, 0, d, hh] for hh in range(HB) for d in range(2))
    else:
        init = tuple(jnp.zeros((DN_DK, LANE), F32) for _ in range(2 * HB))
    fin = lax.fori_loop(0, NC, body, init, unroll=NC <= DN_UNIT)
    if want_state:
        for hh in range(HB):
            for d in range(2):
                sfin_ref[0, 0, d, hh] = fin[2 * hh + d]

    for hh in range(HB):
        cols = slice(hh * LANE, (hh + 1) * LANE)
        for r0 in range(0, T, RB):
            o = oacc_ref[hh, 0, r0:r0 + RB, :] + oacc_ref[hh, 1, r0:r0 + RB, :]
            o = o * lax.rsqrt(jnp.mean(o * o, axis=-1, keepdims=True) + NORM_EPS) * gn_ref[0]
            y_ref[0, r0:r0 + RB, cols] = (o * _silu(z_ref[0, r0:r0 + RB, cols].astype(F32))).astype(BF16)


def _delta_mixer(proj3, bl3, l, conv_w, g_norm, s0, s_prev, HB, NU):
    B, T, _ = proj3.shape
    has_s0 = s0 is not None
    want_state = s_prev is not None
    NC = T // DN_CHUNK
    W = HB * LANE
    kern = functools.partial(_delta_kernel, T=T, HB=HB, NU=NU, has_s0=has_s0, want_state=want_state)
    col = lambda cb: pl.BlockSpec((1, T, W), lambda b, h, cb=cb: (b, 0, cb // HB + h))
    cw = lambda cb: pl.BlockSpec((1, 3, W), lambda b, h, cb=cb: (l, 0, cb // HB + h))
    in_specs = [col(CB_DN_Q), col(CB_DN_K), col(CB_DN_V), col(CB_DN_Z),
                pl.BlockSpec((1, T, LANE), lambda b, h: (b, 0, 0)),
                cw(0), cw(4), cw(8), pl.BlockSpec((1, 1, LANE), lambda b, h: (l, 0, 0))]
    args = [proj3, proj3, proj3, proj3, bl3, conv_w, conv_w, conv_w, g_norm]
    if has_s0:
        in_specs.append(pl.BlockSpec((1, 1, 2, HB, DN_DK, LANE), lambda b, h: (b, l, 0, h, 0, 0)))
        args.append(s0)
    aliases = {}
    if want_state:
        aliases = {len(args): 1}
        in_specs.append(pl.BlockSpec(memory_space=pl.ANY))
        args.append(s_prev)
    out_specs = [pl.BlockSpec((1, T, W), lambda b, h: (b, 0, h))]
    out_shape = [jax.ShapeDtypeStruct((B, T, DN_WIDTH), BF16)]
    if want_state:
        out_specs.append(pl.BlockSpec((1, 1, 2, HB, DN_DK, LANE), lambda b, h: (b, l, 0, h, 0, 0)))
        out_shape.append(jax.ShapeDtypeStruct(s_prev.shape, s_prev.dtype))
    res = pl.pallas_call(
        kern,
        grid=(B, DN_HEADS // HB),
        in_specs=in_specs,
        out_specs=out_specs,
        out_shape=out_shape,
        scratch_shapes=[
            pltpu.VMEM((T + 16, LANE), F32),
            pltpu.VMEM((HB, T, LANE), F32), pltpu.VMEM((HB, T, LANE), F32), pltpu.VMEM((HB, T, LANE), F32),
            pltpu.VMEM((HB, 2, NC, DN_DK + DN_CHUNK, LANE), BF16),
            pltpu.VMEM((HB, 2, NC, DN_DK, LANE), F32),
            pltpu.VMEM((HB, 2, T, LANE), F32),
            pltpu.VMEM((HB, 2, NC, 8, LANE), F32),
        ],
        input_output_aliases=aliases,
        compiler_params=_params(("parallel", "parallel")),
        name="delta_mixer",
    )(*args)
    return (res[0], res[1]) if want_state else (res[0], None)


def _head_masks(rows, dtype, value=1.0):
    lane = lax.broadcasted_iota(jnp.int32, (rows, LANE), 1)
    return (jnp.where(lane < NA_HD, value, 0.0).astype(dtype), jnp.where(lane >= NA_HD, value, 0.0).astype(dtype))


NA_SCALE = NA_HD ** -0.5


def _ctx_attn_body(q_ref, k_ref, v_ref, z_ref, y_ref, T):
    m0, m1 = _head_masks(T, BF16, NA_SCALE)
    f0, f1 = _head_masks(T, F32)
    cols = [slice(p * LANE, (p + 1) * LANE) for p in range(NA_WIDTH // LANE)]
    pairs = [(nb, c) for nb in range(q_ref.shape[0]) for c in cols]
    s = [_dot_nt(q_ref[nb, :, c] * hm, k_ref[nb, :, c]) for nb, c in pairs for hm in (m0, m1)]
    e = [jnp.exp(x - jnp.max(x, axis=-1, keepdims=True)) for x in s]
    o = [_dot(e[2 * i + hh].astype(BF16), v_ref[nb, :, c]) for i, (nb, c) in enumerate(pairs) for hh in range(2)]
    o = [x / jnp.sum(ee, axis=-1, keepdims=True) for x, ee in zip(o, e)]
    for i, (nb, c) in enumerate(pairs):
        out = o[2 * i] * f0 + o[2 * i + 1] * f1
        y_ref[nb, :, c] = (out * _silu(z_ref[nb, :, c].astype(F32))).astype(BF16)


def _na_kernel(q_ref, k_ref, v_ref, z_ref, kc_ref, vc_ref, bias_ref, y_ref, sctx_ref, *, T, L):
    W = GRID_W
    rows_total = T // W
    n_win = NA_WIN_R * W
    RS = NA_ROWS_PER_STEP
    m0, m1 = _head_masks(W, BF16, NA_SCALE)
    f0, f1 = _head_masks(W, F32)
    kcb = kc_ref[0, 0].astype(BF16)
    vcb = vc_ref[0, 0].astype(BF16)

    RB = 256
    mb0, mb1 = _head_masks(RB, BF16, NA_SCALE)
    for r0 in range(0, T, RB):
        q = q_ref[0, r0:r0 + RB, :]
        sctx_ref[0, r0:r0 + RB, :] = _dot_nt(q * mb0, kcb)
        sctx_ref[1, r0:r0 + RB, :] = _dot_nt(q * mb1, kcb)

    def step(i, carry):
        rr = [i * RS + j for j in range(RS)]
        rs_ = [jnp.clip(r - NA_WIN_R // 2, 0, rows_total - NA_WIN_R) for r in rr]
        qrows = [pl.ds(pl.multiple_of(r * W, W), W) for r in rr]
        krows = [pl.ds(pl.multiple_of(rs * W, W), n_win) for rs in rs_]
        qs = []
        for qr in qrows:
            q = q_ref[0, qr, :]
            qs.append(jnp.concatenate([q * m0, q * m1], axis=0))
        s = [_dot_nt(x, k_ref[0, kr, :]) for x, kr in zip(qs, krows)]
        e1, e2, den = [], [], []
        for j in range(RS):
            dr0 = rs_[j] - rr[j] + NA_WIN_R - 1
            bias = jnp.concatenate(
                [jnp.concatenate([bias_ref[0, hh, dr0 + 2 * m] for m in range(NA_WIN_R // 2)], axis=1)
                 for hh in range(2)], axis=0)
            sl = s[j] + bias
            sc = jnp.concatenate([sctx_ref[0, qrows[j], :], sctx_ref[1, qrows[j], :]], axis=0)
            mx = jnp.maximum(jnp.max(sl, axis=-1, keepdims=True), jnp.max(sc, axis=-1, keepdims=True))
            a = jnp.exp(sl - mx)
            b = jnp.exp(sc - mx)
            e1.append(a.astype(BF16))
            e2.append(b.astype(BF16))
            den.append(jnp.sum(a, axis=-1, keepdims=True) + jnp.sum(b, axis=-1, keepdims=True))
        o = [_dot(a, v_ref[0, kr, :]) + _dot(b, vcb) for a, b, kr in zip(e1, e2, krows)]
        for j in range(RS):
            oj = o[j] / den[j]
            out = oj[0:W, :] * f0 + oj[W:2 * W, :] * f1
            y_ref[0, qrows[j], :] = (out * _silu(z_ref[0, qrows[j], :].astype(F32))).astype(BF16)
        return carry

    lax.fori_loop(0, rows_total // RS, step, 0)


def _neighbourhood_attention(proj3, l, k_ctx, v_ctx, bias2):
    B, T, _ = proj3.shape
    L = k_ctx.shape[2]
    col = lambda cb: pl.BlockSpec((1, T, LANE), lambda b, p, cb=cb: (b, 0, cb + p))
    ctx = pl.BlockSpec((1, 1, L, LANE), lambda b, p: (b, l, 0, p))
    return pl.pallas_call(
        functools.partial(_na_kernel, T=T, L=L),
        grid=(B, NA_WIDTH // LANE),
        in_specs=[col(CB_NA_Q), col(CB_NA_K), col(CB_NA_V), col(CB_NA_Z), ctx, ctx,
                  pl.BlockSpec((1, 2, 2 * NA_WIN_R - 2, GRID_W, LANE), lambda b, p: (l, p, 0, 0, 0))],
        out_specs=pl.BlockSpec((1, T, LANE), lambda b, p: (b, 0, p)),
        out_shape=jax.ShapeDtypeStruct((B, T, NA_WIDTH), BF16),
        scratch_shapes=[pltpu.VMEM((2, T, L), F32)],
        compiler_params=_params(("parallel", "parallel")),
        name="nbr_attention",
    )(proj3, proj3, proj3, proj3, k_ctx, v_ctx, bias2)


def _pool_body(u_ref, z_ref, band_ref, pw_ref, ps_ref, y_ref, x_buf, nb, T):
    P = POOL_HALO
    RB = POOL_ROWS
    zeros = jnp.zeros((P, POOL_WIDTH), BF16)
    x_buf[nb, 0:P, :] = zeros
    x_buf[nb, T + P:T + 2 * P, :] = zeros
    for r0 in range(0, T, RB):
        x_buf[nb, P + r0:P + r0 + RB, :] = u_ref[nb, r0:r0 + RB, :]
    groups = list(enumerate(POOL_WINDOWS))
    cols = [slice(g * POOL_GC, (g + 1) * POOL_GC) for g, _ in groups]
    for r0 in range(0, T, RB):
        t = r0 + lax.broadcasted_iota(jnp.int32, (RB, 1), 0)
        tots = [_dot(band_ref[g], x_buf[nb, r0:r0 + RB + 2 * P, cols[g]]) for g, _ in groups]
        pooled = []
        for g, win in groups:
            lo = jnp.maximum(t - win // 2, 0)
            hi = jnp.minimum(t + win // 2 - 1, T - 1)
            inv_cnt = 1.0 / (hi - lo + 1).astype(F32)
            pooled.append((tots[g] * inv_cnt - u_ref[nb, r0:r0 + RB, cols[g]].astype(F32)).astype(BF16))
        ys = [_dot(pooled[g], pw_ref[0, g]) for g, _ in groups]
        for g, _ in groups:
            y = ys[g] * ps_ref[0, :, cols[g]]
            y_ref[nb, r0:r0 + RB, cols[g]] = (y * _silu(z_ref[nb, r0:r0 + RB, cols[g]].astype(F32))).astype(BF16)


def _pool_kernel(u_ref, z_ref, band_ref, pw_ref, ps_ref, y_ref, x_buf, *, T):
    _pool_body(u_ref, z_ref, band_ref, pw_ref, ps_ref, y_ref, x_buf, 0, T)


def _ctx_mixers_kernel(q_ref, k_ref, v_ref, z_ref, u_ref, zp_ref, band_ref, pw_ref, ps_ref, yna_ref, ypl_ref,
                       x_buf, *, T):
    _ctx_attn_body(q_ref, k_ref, v_ref, z_ref, yna_ref, T)
    for nb in range(u_ref.shape[0]):
        _pool_body(u_ref, zp_ref, band_ref, pw_ref, ps_ref, ypl_ref, x_buf, nb, T)


def _pool_bands():
    off = np.arange(POOL_ROWS + 2 * POOL_HALO)[None, :] - POOL_HALO - np.arange(POOL_ROWS)[:, None]
    return jnp.asarray(np.stack([(off >= -(w // 2)) & (off <= w // 2 - 1) for w in POOL_WINDOWS]), BF16)


def _pool_mixer(proj3, l, pool_w, pool_scale):
    B, T, _ = proj3.shape
    wide = POOL_WIDTH // LANE
    bands = _pool_bands()
    return pl.pallas_call(
        functools.partial(_pool_kernel, T=T),
        grid=(B,),
        in_specs=[
            pl.BlockSpec((1, T, POOL_WIDTH), lambda b: (b, 0, CB_PL_U // wide)),
            pl.BlockSpec((1, T, POOL_WIDTH), lambda b: (b, 0, CB_PL_Z // wide)),
            pl.BlockSpec(bands.shape, lambda b: (0, 0, 0)),
            pl.BlockSpec((1, len(POOL_WINDOWS), POOL_GC, POOL_GC), lambda b: (l, 0, 0, 0)),
            pl.BlockSpec((1, 1, POOL_WIDTH), lambda b: (l, 0, 0)),
        ],
        out_specs=pl.BlockSpec((1, T, POOL_WIDTH), lambda b: (b, 0, 0)),
        out_shape=jax.ShapeDtypeStruct((B, T, POOL_WIDTH), BF16),
        scratch_shapes=[pltpu.VMEM((1, T + 2 * POOL_HALO, POOL_WIDTH), BF16)],
        compiler_params=_params(("parallel",)),
        name="pool_mixer",
    )(proj3, proj3, bands, pool_w, pool_scale)


def _ctx_mixers(proj3, l, pool_w, pool_scale, nb=2):
    B, T, _ = proj3.shape
    wide = NA_WIDTH // LANE
    bands = _pool_bands()
    spec = lambda cb: pl.BlockSpec((nb, T, NA_WIDTH), lambda b, cb=cb: (b, 0, cb // wide))
    out = pl.BlockSpec((nb, T, NA_WIDTH), lambda b: (b, 0, 0))
    return pl.pallas_call(
        functools.partial(_ctx_mixers_kernel, T=T),
        grid=(B // nb,),
        in_specs=[spec(CB_NA_Q), spec(CB_NA_K), spec(CB_NA_V), spec(CB_NA_Z), spec(CB_PL_U), spec(CB_PL_Z),
                  pl.BlockSpec(bands.shape, lambda b: (0, 0, 0)),
                  pl.BlockSpec((1, len(POOL_WINDOWS), POOL_GC, POOL_GC), lambda b: (l, 0, 0, 0)),
                  pl.BlockSpec((1, 1, POOL_WIDTH), lambda b: (l, 0, 0))],
        out_specs=[out, out],
        out_shape=[jax.ShapeDtypeStruct((B, T, NA_WIDTH), BF16), jax.ShapeDtypeStruct((B, T, POOL_WIDTH), BF16)],
        scratch_shapes=[pltpu.VMEM((nb, T + 2 * POOL_HALO, POOL_WIDTH), BF16)],
        compiler_params=_params(("parallel",)),
        name="ctx_mixers",
    )(proj3, proj3, proj3, proj3, proj3, proj3, bands, pool_w, pool_scale)


def _merge_kernel(ydn_ref, yna_ref, ypl_ref, gdn_ref, gna_ref, gpl_ref, x_ref, mod_ref, gpost_ref,
                  wd_ref, wn_ref, wp_ref, wo_ref, o_ref, *, tiles_per_batch, mod_row0):
    merged = (_sigmoid(gdn_ref[...].astype(F32)) * _dot(ydn_ref[...], wd_ref[0])
              + _sigmoid(gna_ref[...].astype(F32)) * _dot(yna_ref[...], wn_ref[0])
              + _sigmoid(gpl_ref[...].astype(F32)) * _dot(ypl_ref[...], wp_ref[0]))
    out = _dot(merged.astype(BF16), wo_ref[0])
    out = out * lax.rsqrt(jnp.mean(out * out, axis=-1, keepdims=True) + NORM_EPS) * gpost_ref[0]
    gate = _mod_row(mod_ref, tiles_per_batch, mod_row0)[:, 2 * D_MODEL:3 * D_MODEL]
    o_ref[...] = x_ref[...] + gate * out


def _merge(y_dn, y_na, y_pl, proj2d, x2d, l, mod, g_post, w_dn, w_na, w_pl, w_out, tiles_per_batch, mod_row0, tm):
    ntok = x2d.shape[0]
    g0 = CB_GATE * LANE // D_MODEL
    br = pl.BlockSpec((tm, DN_WIDTH), lambda m: (m, 0))
    gate = lambda j: pl.BlockSpec((tm, D_MODEL), lambda m, j=j: (m, g0 + j))
    wbr = pl.BlockSpec((1, DN_WIDTH, D_MODEL), lambda m: (l, 0, 0))
    return pl.pallas_call(
        functools.partial(_merge_kernel, tiles_per_batch=tiles_per_batch, mod_row0=mod_row0),
        grid=(ntok // tm,),
        in_specs=[br, br, br, gate(0), gate(1), gate(2),
                  pl.BlockSpec((tm, D_MODEL), lambda m: (m, 0)),
                  pl.BlockSpec((1, 8, 3 * D_MODEL), lambda m: (l, 0, 0)),
                  pl.BlockSpec((1, 1, D_MODEL), lambda m: (l, 0, 0)),
                  wbr, wbr, wbr,
                  pl.BlockSpec((1, D_MODEL, D_MODEL), lambda m: (l, 0, 0))],
        out_specs=pl.BlockSpec((tm, D_MODEL), lambda m: (m, 0)),
        out_shape=jax.ShapeDtypeStruct((ntok, D_MODEL), F32),
        compiler_params=_params(("parallel",)),
        name="merge_out",
    )(y_dn, y_na, y_pl, proj2d, proj2d, proj2d, x2d, mod, g_post, w_dn, w_na, w_pl, w_out)


def _layer(x3, l, mod_row0, s0, k_ctx, v_ctx, pw, tm_proj, tm_merge, delta_cfg, carry=None):
    B, T, _ = x3.shape
    n = B * T
    x2d = x3.reshape(n, D_MODEL)
    is_ctx = k_ctx is None
    res = _project(x2d, l, pw["mod"], pw["g_pre"], pw["w_main"], pw["w_ba"], pw["alog_v"], pw["dtb_v"],
                   None if is_ctx else T // tm_proj, mod_row0,
                   kv=carry[:2] if is_ctx else None, tm=tm_proj)
    proj2d, bl2d = res[0], res[1]
    proj3 = proj2d.reshape(B, T, P_MAIN)
    y_dn, s_fin = _delta_mixer(proj3, bl2d.reshape(B, T, LANE), l, pw["conv_w"], pw["g_norm"], s0,
                               s_prev=carry[2] if is_ctx else None,
                               HB=delta_cfg[0], NU=delta_cfg[1])
    if is_ctx:
        y_na, y_pl = _ctx_mixers(proj3, l, pw["pool_w"], pw["pool_scale"])
    else:
        y_na = _neighbourhood_attention(proj3, l, k_ctx, v_ctx, pw["bias2"])
        y_pl = _pool_mixer(proj3, l, pw["pool_w"], pw["pool_scale"])
    out2d = _merge(y_dn.reshape(n, DN_WIDTH), y_na.reshape(n, NA_WIDTH), y_pl.reshape(n, POOL_WIDTH), proj2d,
                   x2d, l, pw["mod"], pw["g_post"], pw["w_dn"], pw["w_na"], pw["w_pl"], pw["w_out"],
                   None if is_ctx else T // tm_merge, mod_row0, tm=tm_merge)
    return out2d.reshape(B, T, D_MODEL), (tuple(res[2:]) + (s_fin,) if is_ctx else None)


def _lane_vecs(v, offset):
    return jnp.zeros((DEPTH, 1, LANE), F32).at[:, 0, offset:offset + 2 * DN_HEADS].set(v.reshape(DEPTH, -1))


def _bias_windows(na_bias):
    period = 2 * GRID_W - 1
    half = NA_WIN_C - 1
    zeros = jnp.zeros(na_bias.shape[:-1] + (period - 2 * half - 1,), F32)
    r = jnp.concatenate([na_bias[..., half:], zeros, na_bias[..., :half]], axis=-1)
    rows = jnp.tile(r, (1, 1, 1, GRID_W))[..., :GRID_W * (period - 1)]
    bias_t = rows.reshape(na_bias.shape[:-1] + (GRID_W, period - 1))[..., :GRID_W]
    cols = np.arange(GRID_W)
    cs = np.clip(cols - NA_WIN_C // 2, 0, GRID_W - NA_WIN_C)[:, None]
    col_ok = (cols[None, :] >= cs) & (cols[None, :] < cs + NA_WIN_C)
    bias_t = jnp.where(col_ok, bias_t, NEG_INF)
    return jnp.concatenate([bias_t[:, :, :-1], bias_t[:, :, 1:]], axis=-1)


def kernel(x_prompt, x_sample, c, cache_k_na, cache_v_na, state_dn, c_ctx, w_ada, b_ada, g_pre, g_post, w_in,
           conv_dn, a_log_dn, dt_bias_dn, g_norm_dn, na_bias, pool_w, pool_scale, w_br_dn, w_br_na, w_br_pl,
           w_out):
    B, T, _ = x_prompt.shape
    DB, DT, _ = x_sample.shape
    L = cache_k_na.shape[2]

    cc = jnp.zeros((8, D_MODEL), F32).at[0].set(c_ctx).at[1:1 + DB].set(c)
    w_main, w_ba = _regroup_weights(w_in)
    pw = dict(
        mod=_modulation(cc, w_ada, b_ada),
        g_pre=g_pre[:, None], g_post=g_post[:, None], g_norm=g_norm_dn[:, None], pool_scale=pool_scale[:, None],
        w_main=w_main, w_ba=w_ba,
        alog_v=_lane_vecs(a_log_dn, 2 * DN_HEADS), dtb_v=_lane_vecs(dt_bias_dn, 2 * DN_HEADS),
        conv_w=conv_dn, bias2=_bias_windows(na_bias), pool_w=pool_w.astype(BF16),
        w_dn=w_br_dn.astype(BF16), w_na=w_br_na.astype(BF16), w_pl=w_br_pl.astype(BF16), w_out=w_out.astype(BF16))
    k_ctx = cache_k_na.reshape(DB, DEPTH, L, NA_WIDTH)
    v_ctx = cache_v_na.reshape(DB, DEPTH, L, NA_WIDTH)

    xp, xs = x_prompt, x_sample
    carry = (jnp.zeros((B, DEPTH, T, NA_WIDTH), F32), jnp.zeros((B, DEPTH, T, NA_WIDTH), F32),
             jnp.zeros((B, DEPTH, 2, DN_HEADS, DN_DK, LANE), F32))
    for l in range(DEPTH):
        xp, carry = _layer(xp, l, 0, None, None, None, pw, tm_proj=1024, tm_merge=512, delta_cfg=(4, 1),
                           carry=carry)
        xs, _ = _layer(xs, l, 1, state_dn, k_ctx, v_ctx, pw, tm_proj=1024, tm_merge=512, delta_cfg=(2, 2))
    k_all, v_all, s_all = carry
    return (xp, xs, k_all.reshape(B, DEPTH, T, NA_HEADS, NA_HD), v_all.reshape(B, DEPTH, T, NA_HEADS, NA_HD), s_all)
```

```python
import functools

import numpy as np
import jax
import jax.numpy as jnp
from jax import lax
from jax.experimental import pallas as pl
from jax.experimental.pallas import tpu as pltpu

F32 = jnp.float32
BF16 = jnp.bfloat16

D_MODEL = 1024
DEPTH = 4
GRID_W = 64
NORM_EPS = 1e-6
NEG_INF = -1e30

DN_HEADS = 4
DN_DK = 128
DN_WIDTH = 512
DN_CHUNK = 64
DN_BLK_SHIFT = 4
DN_UNIT = 4
DN_SEQ_UNROLL = 4

NA_HEADS = 8
NA_HD = 64
NA_WIDTH = 512
NA_WIN_R = 8
NA_WIN_C = 16
NA_ROWS_PER_STEP = 8

POOL_WINDOWS = (2, 4, 8, 16)
POOL_GC = 128
POOL_WIDTH = 512
POOL_HALO = 16
POOL_ROWS = 256

LANE = 128
P_MAIN = 8192
CB_DN_Q, CB_DN_K, CB_DN_V, CB_DN_Z = 0, 4, 8, 12
CB_NA_K, CB_NA_V = 16, 20
CB_NA_Q, CB_NA_Z = 24, 28
CB_PL_U, CB_PL_Z = 32, 36
CB_GATE = 40
SRC_COLS = ((0, 2048), (2576, 3600), (2064, 2576), (3600, 8208))
OFF_BA_SRC = 2048
N_BA = 16

VMEM_LIMIT = 56 * 1024 * 1024


def _sigmoid(x):
    return 0.5 * jnp.tanh(0.5 * x) + 0.5


def _silu(x):
    return x * _sigmoid(x)


def _softplus(x):
    return jnp.maximum(x, 0.0) + jnp.log1p(jnp.exp(-jnp.abs(x)))


def _dot(a, b):
    return jnp.dot(a, b, preferred_element_type=F32)


def _dot_nt(a, b):
    return lax.dot_general(a, b, (((1,), (1,)), ((), ())), preferred_element_type=F32)


def _dot_tn(a, b):
    return lax.dot_general(a, b, (((0,), (0,)), ((), ())), preferred_element_type=F32)


def _aligned(x, m):
    return x if isinstance(x, int) else pl.multiple_of(x, m)


def _params(sem):
    return pltpu.CompilerParams(dimension_semantics=sem, vmem_limit_bytes=VMEM_LIMIT)


def _mod_kernel(cc_ref, w_ref, b_ref, o_ref):
    a = _silu(cc_ref[...]).astype(BF16)
    o_ref[0] = _dot(a, w_ref[0].astype(BF16)) + b_ref[0]


def _modulation(cc, w_ada, b_ada):
    tn = 1024
    return pl.pallas_call(
        _mod_kernel,
        grid=(DEPTH, 3 * D_MODEL // tn),
        in_specs=[
            pl.BlockSpec((8, D_MODEL), lambda l, n: (0, 0)),
            pl.BlockSpec((1, D_MODEL, tn), lambda l, n: (l, 0, n)),
            pl.BlockSpec((1, 1, tn), lambda l, n: (l, 0, n)),
        ],
        out_specs=pl.BlockSpec((1, 8, tn), lambda l, n: (l, 0, n)),
        out_shape=jax.ShapeDtypeStruct((DEPTH, 8, 3 * D_MODEL), F32),
        compiler_params=_params(("parallel", "parallel")),
        name="adaln_mod",
    )(cc, w_ada, b_ada.reshape(DEPTH, 1, 3 * D_MODEL))


def _regroup_kernel(w_ref, o_ref, ba_ref):
    dst = 0
    step = 4 * LANE
    for a, b in SRC_COLS:
        for c0 in range(a, b, step):
            n = min(step, b - c0)
            o_ref[0, :, dst:dst + n] = w_ref[0, :, c0:c0 + n].astype(BF16)
            dst += n
    ba = w_ref[0, :, OFF_BA_SRC:OFF_BA_SRC + LANE]
    lane = lax.broadcasted_iota(jnp.int32, ba.shape, 1)
    ba_ref[0] = jnp.where(lane < N_BA, ba, 0.0).astype(BF16)


def _regroup_weights(w_in, rows=256):
    depth, d_in, n_in = w_in.shape
    return pl.pallas_call(
        _regroup_kernel,
        grid=(depth, d_in // rows),
        in_specs=[pl.BlockSpec((1, rows, n_in), lambda l, r: (l, r, 0))],
        out_specs=[pl.BlockSpec((1, rows, P_MAIN), lambda l, r: (l, r, 0)),
                   pl.BlockSpec((1, rows, LANE), lambda l, r: (l, r, 0))],
        out_shape=[jax.ShapeDtypeStruct((depth, d_in, P_MAIN), BF16),
                   jax.ShapeDtypeStruct((depth, d_in, LANE), BF16)],
        compiler_params=_params(("parallel", "parallel")),
        name="regroup_weights",
    )(w_in)


def _mod_row(mod_ref, tiles_per_batch, mod_row0):
    if tiles_per_batch is None:
        return mod_ref[0, 0:1, :]
    row = mod_row0 + pl.program_id(0) // tiles_per_batch
    return mod_ref[0, pl.ds(row, 1), :]


def _proj_kernel(*refs, tiles_per_batch, mod_row0, kv_tile, n_alias):
    x_ref, mod_ref, g_ref, w_ref, wba_ref, alog_ref, dtb_ref = refs[:7]
    o_ref, bl_ref = refs[7 + n_alias:9 + n_alias]
    k_ref, v_ref = refs[9 + n_alias:11 + n_alias] if kv_tile is not None else (None, None)
    h_ref = refs[-1]

    @pl.when(pl.program_id(1) == 0)
    def _():
        mod = _mod_row(mod_ref, tiles_per_batch, mod_row0)
        shift = mod[:, 0:D_MODEL]
        scale = mod[:, D_MODEL:2 * D_MODEL]
        x = x_ref[...]
        y = x * lax.rsqrt(jnp.mean(x * x, axis=-1, keepdims=True) + NORM_EPS) * g_ref[0]
        hb = (y * (1.0 + scale) + shift).astype(BF16)
        h_ref[...] = hb
        ba = _dot(hb, wba_ref[0])
        lane = lax.broadcasted_iota(jnp.int32, ba.shape, 1)
        bl_ref[...] = jnp.where(lane < 2 * DN_HEADS, _sigmoid(ba),
                                -jnp.exp(alog_ref[0]) * _softplus(ba + dtb_ref[0]))

    acc = _dot(h_ref[...], w_ref[0])
    o_ref[...] = acc.astype(BF16)
    if kv_tile is not None:
        @pl.when(pl.program_id(1) == kv_tile)
        def _():
            seqs, _, t_len, _ = k_ref.shape
            k_ref[:, 0] = acc[:, 0:NA_WIDTH].reshape(seqs, t_len, NA_WIDTH)
            v_ref[:, 0] = acc[:, NA_WIDTH:2 * NA_WIDTH].reshape(seqs, t_len, NA_WIDTH)


def _project(x2d, l, mod, g_pre, w_main, w_ba, alog_v, dtb_v, tiles_per_batch, mod_row0, kv, tm, tn=2048):
    ntok = x2d.shape[0]
    kv_tile = (CB_NA_K * LANE) // tn if kv is not None else None
    prev = [] if kv is None else list(kv)
    kern = functools.partial(_proj_kernel, tiles_per_batch=tiles_per_batch, mod_row0=mod_row0, kv_tile=kv_tile,
                             n_alias=len(prev))
    vec = pl.BlockSpec((1, 1, LANE), lambda m, n: (l, 0, 0))
    in_specs = [
        pl.BlockSpec((tm, D_MODEL), lambda m, n: (m, 0)),
        pl.BlockSpec((1, 8, 3 * D_MODEL), lambda m, n: (l, 0, 0)),
        pl.BlockSpec((1, 1, D_MODEL), lambda m, n: (l, 0, 0)),
        pl.BlockSpec((1, D_MODEL, tn), lambda m, n: (l, 0, n)),
        pl.BlockSpec((1, D_MODEL, LANE), lambda m, n: (l, 0, 0)),
        vec, vec,
    ] + [pl.BlockSpec(memory_space=pl.ANY)] * len(prev)
    out_specs = [pl.BlockSpec((tm, tn), lambda m, n: (m, n)),
                 pl.BlockSpec((tm, LANE), lambda m, n: (m, 0))]
    out_shape = [jax.ShapeDtypeStruct((ntok, P_MAIN), BF16),
                 jax.ShapeDtypeStruct((ntok, LANE), F32)]
    aliases = {}
    if kv is not None:
        t_len = prev[0].shape[2]
        out_specs += [pl.BlockSpec((tm // t_len, 1, t_len, NA_WIDTH), lambda m, n: (m, l, 0, 0))] * 2
        out_shape += [jax.ShapeDtypeStruct(p.shape, p.dtype) for p in prev]
        aliases = {7 + i: 2 + i for i in range(len(prev))}
    return pl.pallas_call(
        kern,
        grid=(ntok // tm, P_MAIN // tn),
        in_specs=in_specs,
        out_specs=out_specs,
        out_shape=out_shape,
        scratch_shapes=[pltpu.VMEM((tm, D_MODEL), BF16)],
        input_output_aliases=aliases,
        compiler_params=_params(("parallel", "arbitrary")),
        name="in_proj",
    )(x2d, mod, g_pre, w_main, w_ba, alog_v, dtb_v, *prev)


def _delta_kernel(*refs, T, HB, NU, has_s0, want_state):
    it = iter(refs)
    q_ref, k_ref, v_ref, z_ref, bl_ref = (next(it) for _ in range(5))
    cwq_ref, cwk_ref, cwv_ref, gn_ref = (next(it) for _ in range(4))
    s0_ref = next(it) if has_s0 else None
    if want_state:
        next(it)
    y_ref = next(it)
    sfin_ref = next(it) if want_state else None
    pad_ref, qn_ref, kn_ref, vn_ref, aq_ref, bm_ref, oacc_ref, gl_ref = (next(it) for _ in range(8))

    C = DN_CHUNK
    NC = T // C
    RB = min(T, 256)
    UR = DN_UNIT * C
    head0 = pl.program_id(1) * HB

    zeros8 = jnp.zeros((8, LANE), F32)
    pad_ref[0:8, :] = zeros8
    pad_ref[T + 8:T + 16, :] = zeros8

    def conv_into(x_ref, cw_ref, dst_ref, hh, mode):
        cols = slice(hh * LANE, (hh + 1) * LANE)
        for r0 in range(0, T, RB):
            pad_ref[8 + r0:8 + r0 + RB, :] = x_ref[0, r0:r0 + RB, cols].astype(F32)
        w = cw_ref[0, :, cols]
        for r0 in range(0, T, RB):
            y = (pad_ref[7 + r0:7 + r0 + RB, :] * w[0:1, :]
                 + pad_ref[8 + r0:8 + r0 + RB, :] * w[1:2, :]
                 + pad_ref[9 + r0:9 + r0 + RB, :] * w[2:3, :])
            y = _silu(y)
            if mode != "v":
                y = y * lax.rsqrt(jnp.sum(y * y, axis=-1, keepdims=True) + NORM_EPS)
            if mode == "q":
                y = y * (DN_DK ** -0.5)
            dst_ref[hh, r0:r0 + RB, :] = y

    for hh in range(HB):
        conv_into(q_ref, cwq_ref, qn_ref, hh, "q")
        conv_into(k_ref, cwk_ref, kn_ref, hh, "k")
        conv_into(v_ref, cwv_ref, vn_ref, hh, "v")

    u_r = lax.broadcasted_iota(jnp.int32, (UR, LANE), 0) & (C - 1)
    u_c = lax.broadcasted_iota(jnp.int32, (UR, LANE), 1)
    is_b = u_c >= C
    u_j = u_c & (C - 1)
    dist = jnp.where(is_b, u_j - u_r, u_r - u_j)
    incl_u = dist >= 0
    strict_u = dist > 0
    rowsum_u = dist <= 0
    blk = (jnp.right_shift(u_r, DN_BLK_SHIFT) == jnp.right_shift(u_j, DN_BLK_SHIFT))[0:C, :]
    bd_r = lax.broadcasted_iota(jnp.int32, (UR, UR), 0)
    bd_c = lax.broadcasted_iota(jnp.int32, (UR, UR), 1)
    same_chunk = jnp.right_shift(bd_r, 6) == jnp.right_shift(bd_c, 6)
    bd_lower = (same_chunk & (bd_r >= bd_c)).astype(BF16)
    bd_upper = (same_chunk & (bd_r <= bd_c)).astype(BF16)
    bd_ones = same_chunk.astype(BF16)
    half_l = (lax.broadcasted_iota(jnp.int32, (C, LANE), 1) < C)

    def split_hi_mid(x):
        hi = x.astype(BF16)
        mid = (x - hi.astype(F32)).astype(BF16)
        return jnp.concatenate([hi, mid], axis=1)

    def dot01(m01, parts):
        r = _dot(m01, parts)
        return r[:, 0:LANE] + r[:, LANE:2 * LANE]

    def block_diag(x):
        xb = x.astype(BF16)
        zb = jnp.zeros_like(xb)
        return jnp.concatenate([jnp.where(half_l, xb, zb), jnp.where(half_l, zb, xb)], axis=0)

    def mm2s(xs, ys):
        return [_dot(x.astype(BF16), block_diag(y)) for x, y in zip(xs, ys)]

    def tri_inv_minus_eye(lms):
        md = [jnp.where(blk, -lm, 0.0) for lm in lms]
        md2 = mm2s(md, md)
        md4 = mm2s(md2, md2)
        md3 = mm2s(md, md2)
        md8 = mm2s(md4, md4)
        px = [a + b + c for a, b, c in zip(md, md2, md3)]
        md12 = mm2s(md4, md8)
        qx = [a + b + c for a, b, c in zip(md4, md8, md12)]
        pq = mm2s(px, qx)
        dx = [a + b + c for a, b, c in zip(px, qx, pq)]
        loff = [jnp.where(blk, 0.0, lm) for lm in lms]
        dl = mm2s(dx, loff)
        n1 = [a + b for a, b in zip(loff, dl)]
        n2 = mm2s(n1, n1)
        n3 = mm2s(n1, n2)
        rx = [b - a - c for a, b, c in zip(n1, n2, n3)]
        rd = mm2s(rx, dx)
        return [a + b + c for a, b, c in zip(rx, dx, rd)]

    lane_u = u_c

    def pick(x, idx):
        return jnp.broadcast_to(jnp.sum(jnp.where(lane_u == idx, x, 0.0), axis=1, keepdims=True), x.shape)

    zeros_r = jnp.zeros((C, 2 * LANE), BF16)

    def local_group(g, carry):
        units = [(hh, g * NU + s) for hh in range(HB) for s in range(NU)]
        U = []
        for hh, un in units:
            rows = pl.ds(_aligned(un * UR, UR), UR)
            bl = bl_ref[0, rows, :]
            hg = head0 + hh
            U.append(dict(hh=hh, un=un, rows=rows,
                          bb_f=pick(bl, hg), bb_b=pick(bl, DN_HEADS + hg),
                          la_f=pick(bl, 2 * DN_HEADS + hg), la_b=pick(bl, 3 * DN_HEADS + hg)))
        for u in U:
            u["sf"] = split_hi_mid(u["la_f"])
            u["sb"] = split_hi_mid(u["la_b"])
            u["sr"] = split_hi_mid(jnp.where(rowsum_u, jnp.where(is_b, u["la_b"], u["la_f"]), 0.0))
        for u in U:
            u["gc_f"] = dot01(bd_lower, u["sf"])
        for u in U:
            u["gc_b"] = dot01(bd_upper, u["sb"])
        for u in U:
            u["g_row"] = dot01(bd_ones, u["sr"])
        for u in U:
            hh, rows = u["hh"], u["rows"]
            u["k"] = kn_ref[hh, rows, :]
            u["q"] = qn_ref[hh, rows, :]
            u["k16"] = u["k"].astype(BF16)
            u["q16"] = u["q"].astype(BF16)
        chunks = [(u, ci) for u in U for ci in range(DN_UNIT)]
        crs = [slice(ci * C, (ci + 1) * C) for _, ci in chunks]
        grams = [_dot_nt(jnp.concatenate([u["k16"][cr], u["q16"][cr]], axis=0),
                         jnp.concatenate([u["k16"][cr], u["k16"][cr]], axis=0))
                 for (u, _), cr in zip(chunks, crs)]
        for u in U:
            u["decay"] = jnp.exp(jnp.where(incl_u, jnp.where(is_b, u["gc_b"], u["gc_f"]) - u["g_row"], NEG_INF))
            u["bb_p"] = jnp.where(is_b, u["bb_b"], u["bb_f"])
        lms = [jnp.where(strict_u[cr], gm[0:C] * u["bb_p"][cr] * u["decay"][cr], 0.0)
               for (u, _), cr, gm in zip(chunks, crs, grams)]
        attn = [(gm[C:2 * C] * u["decay"][cr]).astype(BF16) for (u, _), cr, gm in zip(chunks, crs, grams)]
        txs = tri_inv_minus_eye(lms)
        for u in U:
            v = vn_ref[u["hh"], u["rows"], :]
            eg_f = jnp.exp(u["gc_f"])
            eg_b = jnp.exp(u["gc_b"])
            kb_f = u["k"] * u["bb_f"]
            kb_b = u["k"] * u["bb_b"]
            u["rhs_f"] = jnp.concatenate([v * u["bb_f"], kb_f * eg_f], axis=1)
            u["rhs_b"] = jnp.concatenate([v * u["bb_b"], kb_b * eg_b], axis=1)
            u["r16_f"] = u["rhs_f"].astype(BF16)
            u["r16_b"] = u["rhs_b"].astype(BF16)
            u["qg_f"] = u["q"] * eg_f
            u["qg_b"] = u["q"] * eg_b
        uws = [_dot(tx.astype(BF16), jnp.concatenate(
                    [jnp.concatenate([u["r16_f"][cr], zeros_r], axis=1),
                     jnp.concatenate([zeros_r, u["r16_b"][cr]], axis=1)], axis=0))
               for (u, _), cr, tx in zip(chunks, crs, txs)]
        wu_f = [(u["rhs_f"][cr] + uw[:, 0:2 * LANE]) for (u, _), cr, uw in zip(chunks, crs, uws)]
        wu_b = [(u["rhs_b"][cr] + uw[:, 2 * LANE:4 * LANE]) for (u, _), cr, uw in zip(chunks, crs, uws)]
        wu16_f = [x.astype(BF16) for x in wu_f]
        wu16_b = [x.astype(BF16) for x in wu_b]
        aws = [_dot(a, jnp.concatenate([jnp.concatenate([xf, zeros_r], axis=1),
                                        jnp.concatenate([zeros_r, xb], axis=1)], axis=0))
               for a, xf, xb in zip(attn, wu16_f, wu16_b)]
        kg_f, kg_b, gls = [], [], []
        for (u, ci), cr in zip(chunks, crs):
            gl_f = u["gc_f"][ci * C + C - 1:ci * C + C, :]
            gl_b = u["gc_b"][ci * C:ci * C + 1, :]
            kg_f.append((u["k"][cr] * jnp.exp(gl_f - u["gc_f"][cr])).astype(BF16))
            kg_b.append((u["k"][cr] * jnp.exp(gl_b - u["gc_b"][cr])).astype(BF16))
            gls.append((jnp.exp(gl_f), jnp.exp(gl_b)))
        kwu_f = [_dot_tn(kg, x) for kg, x in zip(kg_f, wu16_f)]
        kwu_b = [_dot_tn(kg, x) for kg, x in zip(kg_b, wu16_b)]
        for idx, ((u, ci), cr) in enumerate(zip(chunks, crs)):
            hh = u["hh"]
            c = u["un"] * DN_UNIT + ci
            crow = pl.ds(_aligned(c * C, C), C)
            aw = aws[idx]
            for d, kwu, qg, a0 in ((0, kwu_f[idx], u["qg_f"], 0), (1, kwu_b[idx], u["qg_b"], 2 * LANE)):
                aq_ref[hh, d, c, 0:DN_DK, :] = kwu[:, LANE:2 * LANE].astype(BF16)
                aq_ref[hh, d, c, DN_DK:DN_DK + C, :] = (qg[cr] - aw[:, a0 + LANE:a0 + 2 * LANE]).astype(BF16)
                bm_ref[hh, d, c] = kwu[:, 0:LANE]
                oacc_ref[hh, d, crow, :] = aw[:, a0:a0 + LANE]
                gl_ref[hh, d, c] = jnp.broadcast_to(gls[idx][d], (8, LANE))
        return carry

    n_groups = T // (NU * UR)
    if n_groups == 1:
        local_group(0, 0)
    else:
        lax.fori_loop(0, n_groups, local_group, 0)

    def body(i, carry):
        cidx = (i, NC - 1 - i)
        rs = [_dot(aq_ref[hh, d, cidx[d]], carry[2 * hh + d].astype(BF16)) for hh in range(HB) for d in range(2)]
        new = []
        for hh in range(HB):
            for d in range(2):
                r = rs[2 * hh + d]
                c = cidx[d]
                rows = pl.ds(pl.multiple_of(c * C, C), C)
                oacc_ref[hh, d, rows, :] = oacc_ref[hh, d, rows, :] + r[DN_DK:DN_DK + C]
                new.append(carry[2 * hh + d] * gl_ref[hh, d, c, 0:1, :] + bm_ref[hh, d, c] - r[0:DN_DK])
        return tuple(new)

    if has_s0:
        init = tuple(s0_ref[0, 0, d, hh] for hh in range(HB) for d in range(2))
    else:
        init = tuple(jnp.zeros((DN_DK, LANE), F32) for _ in range(2 * HB))
    fin = lax.fori_loop(0, NC, body, init, unroll=min(NC, DN_SEQ_UNROLL))
    if want_state:
        for hh in range(HB):
            for d in range(2):
                sfin_ref[0, 0, d, hh] = fin[2 * hh + d]

    for hh in range(HB):
        cols = slice(hh * LANE, (hh + 1) * LANE)
        for r0 in range(0, T, RB):
            o = oacc_ref[hh, 0, r0:r0 + RB, :] + oacc_ref[hh, 1, r0:r0 + RB, :]
            o = o * lax.rsqrt(jnp.mean(o * o, axis=-1, keepdims=True) + NORM_EPS) * gn_ref[0]
            y_ref[0, r0:r0 + RB, cols] = (o * _silu(z_ref[0, r0:r0 + RB, cols].astype(F32))).astype(BF16)


def _delta_mixer(proj3, bl3, l, conv_w, g_norm, s0, s_prev, HB, NU):
    B, T, _ = proj3.shape
    has_s0 = s0 is not None
    want_state = s_prev is not None
    NC = T // DN_CHUNK
    W = HB * LANE
    kern = functools.partial(_delta_kernel, T=T, HB=HB, NU=NU, has_s0=has_s0, want_state=want_state)
    col = lambda cb: pl.BlockSpec((1, T, W), lambda b, h, cb=cb: (b, 0, cb // HB + h))
    cw = lambda cb: pl.BlockSpec((1, 3, W), lambda b, h, cb=cb: (l, 0, cb // HB + h))
    in_specs = [col(CB_DN_Q), col(CB_DN_K), col(CB_DN_V), col(CB_DN_Z),
                pl.BlockSpec((1, T, LANE), lambda b, h: (b, 0, 0)),
                cw(0), cw(4), cw(8), pl.BlockSpec((1, 1, LANE), lambda b, h: (l, 0, 0))]
    args = [proj3, proj3, proj3, proj3, bl3, conv_w, conv_w, conv_w, g_norm]
    if has_s0:
        in_specs.append(pl.BlockSpec((1, 1, 2, HB, DN_DK, LANE), lambda b, h: (b, l, 0, h, 0, 0)))
        args.append(s0)
    aliases = {}
    if want_state:
        aliases = {len(args): 1}
        in_specs.append(pl.BlockSpec(memory_space=pl.ANY))
        args.append(s_prev)
    out_specs = [pl.BlockSpec((1, T, W), lambda b, h: (b, 0, h))]
    out_shape = [jax.ShapeDtypeStruct((B, T, DN_WIDTH), BF16)]
    if want_state:
        out_specs.append(pl.BlockSpec((1, 1, 2, HB, DN_DK, LANE), lambda b, h: (b, l, 0, h, 0, 0)))
        out_shape.append(jax.ShapeDtypeStruct(s_prev.shape, s_prev.dtype))
    res = pl.pallas_call(
        kern,
        grid=(B, DN_HEADS // HB),
        in_specs=in_specs,
        out_specs=out_specs,
        out_shape=out_shape,
        scratch_shapes=[
            pltpu.VMEM((T + 16, LANE), F32),
            pltpu.VMEM((HB, T, LANE), F32), pltpu.VMEM((HB, T, LANE), F32), pltpu.VMEM((HB, T, LANE), F32),
            pltpu.VMEM((HB, 2, NC, DN_DK + DN_CHUNK, LANE), BF16),
            pltpu.VMEM((HB, 2, NC, DN_DK, LANE), F32),
            pltpu.VMEM((HB, 2, T, LANE), F32),
            pltpu.VMEM((HB, 2, NC, 8, LANE), F32),
        ],
        input_output_aliases=aliases,
        compiler_params=_params(("parallel", "parallel")),
        name="delta_mixer",
    )(*args)
    return (res[0], res[1]) if want_state else (res[0], None)


def _head_masks(rows, dtype, value=1.0):
    lane = lax.broadcasted_iota(jnp.int32, (rows, LANE), 1)
    return (jnp.where(lane < NA_HD, value, 0.0).astype(dtype), jnp.where(lane >= NA_HD, value, 0.0).astype(dtype))


NA_SCALE = NA_HD ** -0.5


def _ctx_attn_body(q_ref, k_ref, v_ref, z_ref, y_ref, T):
    m0, m1 = _head_masks(T, BF16, NA_SCALE)
    f0, f1 = _head_masks(T, F32)
    cols = [slice(p * LANE, (p + 1) * LANE) for p in range(NA_WIDTH // LANE)]
    pairs = [(nb, c) for nb in range(q_ref.shape[0]) for c in cols]
    s = [_dot_nt(q_ref[nb, :, c] * hm, k_ref[nb, :, c]) for nb, c in pairs for hm in (m0, m1)]
    e = [jnp.exp(x - jnp.max(x, axis=-1, keepdims=True)) for x in s]
    o = [_dot(e[2 * i + hh].astype(BF16), v_ref[nb, :, c]) for i, (nb, c) in enumerate(pairs) for hh in range(2)]
    o = [x / jnp.sum(ee, axis=-1, keepdims=True) for x, ee in zip(o, e)]
    for i, (nb, c) in enumerate(pairs):
        out = o[2 * i] * f0 + o[2 * i + 1] * f1
        y_ref[nb, :, c] = (out * _silu(z_ref[nb, :, c].astype(F32))).astype(BF16)


def _na_kernel(q_ref, k_ref, v_ref, z_ref, kc_ref, vc_ref, bias_ref, y_ref, sctx_ref, *, T, L):
    W = GRID_W
    rows_total = T // W
    n_win = NA_WIN_R * W
    RS = NA_ROWS_PER_STEP
    m0, m1 = _head_masks(W, BF16, NA_SCALE)
    f0, f1 = _head_masks(W, F32)
    kcb = kc_ref[0, 0].astype(BF16)
    vcb = vc_ref[0, 0].astype(BF16)

    RB = 256
    mb0, mb1 = _head_masks(RB, BF16, NA_SCALE)
    for r0 in range(0, T, RB):
        q = q_ref[0, r0:r0 + RB, :]
        sctx_ref[0, r0:r0 + RB, :] = _dot_nt(q * mb0, kcb)
        sctx_ref[1, r0:r0 + RB, :] = _dot_nt(q * mb1, kcb)

    def step(i, carry):
        rr = [i * RS + j for j in range(RS)]
        rs_ = [jnp.clip(r - NA_WIN_R // 2, 0, rows_total - NA_WIN_R) for r in rr]
        qrows = [pl.ds(pl.multiple_of(r * W, W), W) for r in rr]
        krows = [pl.ds(pl.multiple_of(rs * W, W), n_win) for rs in rs_]
        qs = []
        for qr in qrows:
            q = q_ref[0, qr, :]
            qs.append(jnp.concatenate([q * m0, q * m1], axis=0))
        s = [_dot_nt(x, k_ref[0, kr, :]) for x, kr in zip(qs, krows)]
        e1, e2, den = [], [], []
        for j in range(RS):
            dr0 = rs_[j] - rr[j] + NA_WIN_R - 1
            bias = jnp.concatenate(
                [jnp.concatenate([bias_ref[0, hh, dr0 + 2 * m] for m in range(NA_WIN_R // 2)], axis=1)
                 for hh in range(2)], axis=0)
            sl = s[j] + bias
            sc = jnp.concatenate([sctx_ref[0, qrows[j], :], sctx_ref[1, qrows[j], :]], axis=0)
            mx = jnp.maximum(jnp.max(sl, axis=-1, keepdims=True), jnp.max(sc, axis=-1, keepdims=True))
            a = jnp.exp(sl - mx)
            b = jnp.exp(sc - mx)
            e1.append(a.astype(BF16))
            e2.append(b.astype(BF16))
            den.append(jnp.sum(a, axis=-1, keepdims=True) + jnp.sum(b, axis=-1, keepdims=True))
        o = [_dot(a, v_ref[0, kr, :]) + _dot(b, vcb) for a, b, kr in zip(e1, e2, krows)]
        for j in range(RS):
            oj = o[j] / den[j]
            out = oj[0:W, :] * f0 + oj[W:2 * W, :] * f1
            y_ref[0, qrows[j], :] = (out * _silu(z_ref[0, qrows[j], :].astype(F32))).astype(BF16)
        return carry

    lax.fori_loop(0, rows_total // RS, step, 0)


def _neighbourhood_attention(proj3, l, k_ctx, v_ctx, bias2):
    B, T, _ = proj3.shape
    L = k_ctx.shape[2]
    col = lambda cb: pl.BlockSpec((1, T, LANE), lambda b, p, cb=cb: (b, 0, cb + p))
    ctx = pl.BlockSpec((1, 1, L, LANE), lambda b, p: (b, l, 0, p))
    return pl.pallas_call(
        functools.partial(_na_kernel, T=T, L=L),
        grid=(B, NA_WIDTH // LANE),
        in_specs=[col(CB_NA_Q), col(CB_NA_K), col(CB_NA_V), col(CB_NA_Z), ctx, ctx,
                  pl.BlockSpec((1, 2, 2 * NA_WIN_R - 2, GRID_W, LANE), lambda b, p: (l, p, 0, 0, 0))],
        out_specs=pl.BlockSpec((1, T, LANE), lambda b, p: (b, 0, p)),
        out_shape=jax.ShapeDtypeStruct((B, T, NA_WIDTH), BF16),
        scratch_shapes=[pltpu.VMEM((2, T, L), F32)],
        compiler_params=_params(("parallel", "parallel")),
        name="nbr_attention",
    )(proj3, proj3, proj3, proj3, k_ctx, v_ctx, bias2)


def _pool_body(u_ref, z_ref, band_ref, pw_ref, ps_ref, y_ref, x_buf, nb, T):
    P = POOL_HALO
    RB = POOL_ROWS
    zeros = jnp.zeros((P, POOL_WIDTH), BF16)
    x_buf[nb, 0:P, :] = zeros
    x_buf[nb, T + P:T + 2 * P, :] = zeros
    for r0 in range(0, T, RB):
        x_buf[nb, P + r0:P + r0 + RB, :] = u_ref[nb, r0:r0 + RB, :]
    groups = list(enumerate(POOL_WINDOWS))
    cols = [slice(g * POOL_GC, (g + 1) * POOL_GC) for g, _ in groups]
    for r0 in range(0, T, RB):
        t = r0 + lax.broadcasted_iota(jnp.int32, (RB, 1), 0)
        tots = [_dot(band_ref[g], x_buf[nb, r0:r0 + RB + 2 * P, cols[g]]) for g, _ in groups]
        pooled = []
        for g, win in groups:
            lo = jnp.maximum(t - win // 2, 0)
            hi = jnp.minimum(t + win // 2 - 1, T - 1)
            inv_cnt = 1.0 / (hi - lo + 1).astype(F32)
            pooled.append((tots[g] * inv_cnt - u_ref[nb, r0:r0 + RB, cols[g]].astype(F32)).astype(BF16))
        ys = [_dot(pooled[g], pw_ref[0, g]) for g, _ in groups]
        for g, _ in groups:
            y = ys[g] * ps_ref[0, :, cols[g]]
            y_ref[nb, r0:r0 + RB, cols[g]] = (y * _silu(z_ref[nb, r0:r0 + RB, cols[g]].astype(F32))).astype(BF16)


def _pool_kernel(u_ref, z_ref, band_ref, pw_ref, ps_ref, y_ref, x_buf, *, T):
    _pool_body(u_ref, z_ref, band_ref, pw_ref, ps_ref, y_ref, x_buf, 0, T)


def _ctx_mixers_kernel(q_ref, k_ref, v_ref, z_ref, u_ref, zp_ref, band_ref, pw_ref, ps_ref, yna_ref, ypl_ref,
                       x_buf, *, T):
    _ctx_attn_body(q_ref, k_ref, v_ref, z_ref, yna_ref, T)
    for nb in range(u_ref.shape[0]):
        _pool_body(u_ref, zp_ref, band_ref, pw_ref, ps_ref, ypl_ref, x_buf, nb, T)


def _pool_bands():
    off = np.arange(POOL_ROWS + 2 * POOL_HALO)[None, :] - POOL_HALO - np.arange(POOL_ROWS)[:, None]
    return jnp.asarray(np.stack([(off >= -(w // 2)) & (off <= w // 2 - 1) for w in POOL_WINDOWS]), BF16)


def _pool_mixer(proj3, l, pool_w, pool_scale):
    B, T, _ = proj3.shape
    wide = POOL_WIDTH // LANE
    bands = _pool_bands()
    return pl.pallas_call(
        functools.partial(_pool_kernel, T=T),
        grid=(B,),
        in_specs=[
            pl.BlockSpec((1, T, POOL_WIDTH), lambda b: (b, 0, CB_PL_U // wide)),
            pl.BlockSpec((1, T, POOL_WIDTH), lambda b: (b, 0, CB_PL_Z // wide)),
            pl.BlockSpec(bands.shape, lambda b: (0, 0, 0)),
            pl.BlockSpec((1, len(POOL_WINDOWS), POOL_GC, POOL_GC), lambda b: (l, 0, 0, 0)),
            pl.BlockSpec((1, 1, POOL_WIDTH), lambda b: (l, 0, 0)),
        ],
        out_specs=pl.BlockSpec((1, T, POOL_WIDTH), lambda b: (b, 0, 0)),
        out_shape=jax.ShapeDtypeStruct((B, T, POOL_WIDTH), BF16),
        scratch_shapes=[pltpu.VMEM((1, T + 2 * POOL_HALO, POOL_WIDTH), BF16)],
        compiler_params=_params(("parallel",)),
        name="pool_mixer",
    )(proj3, proj3, bands, pool_w, pool_scale)


def _ctx_mixers(proj3, l, pool_w, pool_scale, nb=4):
    B, T, _ = proj3.shape
    wide = NA_WIDTH // LANE
    bands = _pool_bands()
    spec = lambda cb: pl.BlockSpec((nb, T, NA_WIDTH), lambda b, cb=cb: (b, 0, cb // wide))
    out = pl.BlockSpec((nb, T, NA_WIDTH), lambda b: (b, 0, 0))
    return pl.pallas_call(
        functools.partial(_ctx_mixers_kernel, T=T),
        grid=(B // nb,),
        in_specs=[spec(CB_NA_Q), spec(CB_NA_K), spec(CB_NA_V), spec(CB_NA_Z), spec(CB_PL_U), spec(CB_PL_Z),
                  pl.BlockSpec(bands.shape, lambda b: (0, 0, 0)),
                  pl.BlockSpec((1, len(POOL_WINDOWS), POOL_GC, POOL_GC), lambda b: (l, 0, 0, 0)),
                  pl.BlockSpec((1, 1, POOL_WIDTH), lambda b: (l, 0, 0))],
        out_specs=[out, out],
        out_shape=[jax.ShapeDtypeStruct((B, T, NA_WIDTH), BF16), jax.ShapeDtypeStruct((B, T, POOL_WIDTH), BF16)],
        scratch_shapes=[pltpu.VMEM((nb, T + 2 * POOL_HALO, POOL_WIDTH), BF16)],
        compiler_params=_params(("parallel",)),
        name="ctx_mixers",
    )(proj3, proj3, proj3, proj3, proj3, proj3, bands, pool_w, pool_scale)


def _merge_kernel(ydn_ref, yna_ref, ypl_ref, gdn_ref, gna_ref, gpl_ref, x_ref, mod_ref, gpost_ref,
                  wd_ref, wn_ref, wp_ref, wo_ref, o_ref, *, tiles_per_batch, mod_row0):
    merged = (_sigmoid(gdn_ref[...].astype(F32)) * _dot(ydn_ref[...], wd_ref[0])
              + _sigmoid(gna_ref[...].astype(F32)) * _dot(yna_ref[...], wn_ref[0])
              + _sigmoid(gpl_ref[...].astype(F32)) * _dot(ypl_ref[...], wp_ref[0]))
    out = _dot(merged.astype(BF16), wo_ref[0])
    out = out * lax.rsqrt(jnp.mean(out * out, axis=-1, keepdims=True) + NORM_EPS) * gpost_ref[0]
    gate = _mod_row(mod_ref, tiles_per_batch, mod_row0)[:, 2 * D_MODEL:3 * D_MODEL]
    o_ref[...] = x_ref[...] + gate * out


def _merge(y_dn, y_na, y_pl, proj2d, x2d, l, mod, g_post, w_dn, w_na, w_pl, w_out, tiles_per_batch, mod_row0, tm):
    ntok = x2d.shape[0]
    g0 = CB_GATE * LANE // D_MODEL
    br = pl.BlockSpec((tm, DN_WIDTH), lambda m: (m, 0))
    gate = lambda j: pl.BlockSpec((tm, D_MODEL), lambda m, j=j: (m, g0 + j))
    wbr = pl.BlockSpec((1, DN_WIDTH, D_MODEL), lambda m: (l, 0, 0))
    return pl.pallas_call(
        functools.partial(_merge_kernel, tiles_per_batch=tiles_per_batch, mod_row0=mod_row0),
        grid=(ntok // tm,),
        in_specs=[br, br, br, gate(0), gate(1), gate(2),
                  pl.BlockSpec((tm, D_MODEL), lambda m: (m, 0)),
                  pl.BlockSpec((1, 8, 3 * D_MODEL), lambda m: (l, 0, 0)),
                  pl.BlockSpec((1, 1, D_MODEL), lambda m: (l, 0, 0)),
                  wbr, wbr, wbr,
                  pl.BlockSpec((1, D_MODEL, D_MODEL), lambda m: (l, 0, 0))],
        out_specs=pl.BlockSpec((tm, D_MODEL), lambda m: (m, 0)),
        out_shape=jax.ShapeDtypeStruct((ntok, D_MODEL), F32),
        compiler_params=_params(("parallel",)),
        name="merge_out",
    )(y_dn, y_na, y_pl, proj2d, proj2d, proj2d, x2d, mod, g_post, w_dn, w_na, w_pl, w_out)


def _layer(x3, l, mod_row0, s0, k_ctx, v_ctx, pw, tm_proj, tm_merge, delta_cfg, carry=None):
    B, T, _ = x3.shape
    n = B * T
    x2d = x3.reshape(n, D_MODEL)
    is_ctx = k_ctx is None
    res = _project(x2d, l, pw["mod"], pw["g_pre"], pw["w_main"], pw["w_ba"], pw["alog_v"], pw["dtb_v"],
                   None if is_ctx else T // tm_proj, mod_row0,
                   kv=carry[:2] if is_ctx else None, tm=tm_proj)
    proj2d, bl2d = res[0], res[1]
    proj3 = proj2d.reshape(B, T, P_MAIN)
    y_dn, s_fin = _delta_mixer(proj3, bl2d.reshape(B, T, LANE), l, pw["conv_w"], pw["g_norm"], s0,
                               s_prev=carry[2] if is_ctx else None,
                               HB=delta_cfg[0], NU=delta_cfg[1])
    if is_ctx:
        y_na, y_pl = _ctx_mixers(proj3, l, pw["pool_w"], pw["pool_scale"])
    else:
        y_na = _neighbourhood_attention(proj3, l, k_ctx, v_ctx, pw["bias2"])
        y_pl = _pool_mixer(proj3, l, pw["pool_w"], pw["pool_scale"])
    out2d = _merge(y_dn.reshape(n, DN_WIDTH), y_na.reshape(n, NA_WIDTH), y_pl.reshape(n, POOL_WIDTH), proj2d,
                   x2d, l, pw["mod"], pw["g_post"], pw["w_dn"], pw["w_na"], pw["w_pl"], pw["w_out"],
                   None if is_ctx else T // tm_merge, mod_row0, tm=tm_merge)
    return out2d.reshape(B, T, D_MODEL), (tuple(res[2:]) + (s_fin,) if is_ctx else None)


def _lane_vecs(v, offset):
    return jnp.zeros((DEPTH, 1, LANE), F32).at[:, 0, offset:offset + 2 * DN_HEADS].set(v.reshape(DEPTH, -1))


def _bias_windows(na_bias):
    period = 2 * GRID_W - 1
    half = NA_WIN_C - 1
    zeros = jnp.zeros(na_bias.shape[:-1] + (period - 2 * half - 1,), F32)
    r = jnp.concatenate([na_bias[..., half:], zeros, na_bias[..., :half]], axis=-1)
    rows = jnp.tile(r, (1, 1, 1, GRID_W))[..., :GRID_W * (period - 1)]
    bias_t = rows.reshape(na_bias.shape[:-1] + (GRID_W, period - 1))[..., :GRID_W]
    cols = np.arange(GRID_W)
    cs = np.clip(cols - NA_WIN_C // 2, 0, GRID_W - NA_WIN_C)[:, None]
    col_ok = (cols[None, :] >= cs) & (cols[None, :] < cs + NA_WIN_C)
    bias_t = jnp.where(col_ok, bias_t, NEG_INF)
    return jnp.concatenate([bias_t[:, :, :-1], bias_t[:, :, 1:]], axis=-1)


def kernel(x_prompt, x_sample, c, cache_k_na, cache_v_na, state_dn, c_ctx, w_ada, b_ada, g_pre, g_post, w_in,
           conv_dn, a_log_dn, dt_bias_dn, g_norm_dn, na_bias, pool_w, pool_scale, w_br_dn, w_br_na, w_br_pl,
           w_out):
    B, T, _ = x_prompt.shape
    DB, DT, _ = x_sample.shape
    L = cache_k_na.shape[2]

    cc = jnp.zeros((8, D_MODEL), F32).at[0].set(c_ctx).at[1:1 + DB].set(c)
    w_main, w_ba = _regroup_weights(w_in)
    pw = dict(
        mod=_modulation(cc, w_ada, b_ada),
        g_pre=g_pre[:, None], g_post=g_post[:, None], g_norm=g_norm_dn[:, None], pool_scale=pool_scale[:, None],
        w_main=w_main, w_ba=w_ba,
        alog_v=_lane_vecs(a_log_dn, 2 * DN_HEADS), dtb_v=_lane_vecs(dt_bias_dn, 2 * DN_HEADS),
        conv_w=conv_dn, bias2=_bias_windows(na_bias), pool_w=pool_w.astype(BF16),
        w_dn=w_br_dn.astype(BF16), w_na=w_br_na.astype(BF16), w_pl=w_br_pl.astype(BF16), w_out=w_out.astype(BF16))
    k_ctx = cache_k_na.reshape(DB, DEPTH, L, NA_WIDTH)
    v_ctx = cache_v_na.reshape(DB, DEPTH, L, NA_WIDTH)

    xp, xs = x_prompt, x_sample
    carry = (jnp.zeros((B, DEPTH, T, NA_WIDTH), F32), jnp.zeros((B, DEPTH, T, NA_WIDTH), F32),
             jnp.zeros((B, DEPTH, 2, DN_HEADS, DN_DK, LANE), F32))
    for l in range(DEPTH):
        xp, carry = _layer(xp, l, 0, None, None, None, pw, tm_proj=1024, tm_merge=512, delta_cfg=(4, 1),
                           carry=carry)
        xs, _ = _layer(xs, l, 1, state_dn, k_ctx, v_ctx, pw, tm_proj=1024, tm_merge=512, delta_cfg=(2, 2))
    k_all, v_all, s_all = carry
    return (xp, xs, k_all.reshape(B, DEPTH, T, NA_HEADS, NA_HD), v_all.reshape(B, DEPTH, T, NA_HEADS, NA_HD), s_all)
```

```python
import functools

import numpy as np
import jax
import jax.numpy as jnp
from jax import lax
from jax.experimental import pallas as pl
from jax.experimental.pallas import tpu as pltpu

F32 = jnp.float32
BF16 = jnp.bfloat16

D_MODEL = 1024
DEPTH = 4
GRID_W = 64
NORM_EPS = 1e-6
NEG_INF = -1e30

DN_HEADS = 4
DN_DK = 128
DN_WIDTH = 512
DN_CHUNK = 64
DN_BLK_SHIFT = 4
DN_UNIT = 4
DN_SEQ_UNROLL = 4

NA_HEADS = 8
NA_HD = 64
NA_WIDTH = 512
NA_WIN_R = 8
NA_WIN_C = 16
NA_ROWS_PER_STEP = 8

POOL_WINDOWS = (2, 4, 8, 16)
POOL_GC = 128
POOL_WIDTH = 512
POOL_HALO = 16
POOL_ROWS = 256

LANE = 128
P_MAIN = 8192
CB_DN_Q, CB_DN_K, CB_DN_V, CB_DN_Z = 0, 4, 8, 12
CB_NA_K, CB_NA_V = 16, 20
CB_NA_Q, CB_NA_Z = 24, 28
CB_PL_U, CB_PL_Z = 32, 36
CB_GATE = 40
SRC_COLS = ((0, 2048), (2576, 3600), (2064, 2576), (3600, 8208))
OFF_BA_SRC = 2048
N_BA = 16

VMEM_LIMIT = 56 * 1024 * 1024


def _sigmoid(x):
    return 0.5 * jnp.tanh(0.5 * x) + 0.5


def _silu(x):
    return x * _sigmoid(x)


def _softplus(x):
    return jnp.maximum(x, 0.0) + jnp.log1p(jnp.exp(-jnp.abs(x)))


def _dot(a, b):
    return jnp.dot(a, b, preferred_element_type=F32)


def _dot_nt(a, b):
    return lax.dot_general(a, b, (((1,), (1,)), ((), ())), preferred_element_type=F32)


def _dot_tn(a, b):
    return lax.dot_general(a, b, (((0,), (0,)), ((), ())), preferred_element_type=F32)


def _aligned(x, m):
    return x if isinstance(x, int) else pl.multiple_of(x, m)


def _params(sem):
    return pltpu.CompilerParams(dimension_semantics=sem, vmem_limit_bytes=VMEM_LIMIT)


def _mod_kernel(cc_ref, w_ref, b_ref, o_ref):
    a = _silu(cc_ref[...]).astype(BF16)
    o_ref[0] = _dot(a, w_ref[0].astype(BF16)) + b_ref[0]


def _modulation(cc, w_ada, b_ada):
    tn = 1024
    return pl.pallas_call(
        _mod_kernel,
        grid=(DEPTH, 3 * D_MODEL // tn),
        in_specs=[
            pl.BlockSpec((8, D_MODEL), lambda l, n: (0, 0)),
            pl.BlockSpec((1, D_MODEL, tn), lambda l, n: (l, 0, n)),
            pl.BlockSpec((1, 1, tn), lambda l, n: (l, 0, n)),
        ],
        out_specs=pl.BlockSpec((1, 8, tn), lambda l, n: (l, 0, n)),
        out_shape=jax.ShapeDtypeStruct((DEPTH, 8, 3 * D_MODEL), F32),
        compiler_params=_params(("parallel", "parallel")),
        name="adaln_mod",
    )(cc, w_ada, b_ada.reshape(DEPTH, 1, 3 * D_MODEL))


def _regroup_kernel(w_ref, o_ref, ba_ref):
    dst = 0
    step = 4 * LANE
    for a, b in SRC_COLS:
        for c0 in range(a, b, step):
            n = min(step, b - c0)
            o_ref[0, :, dst:dst + n] = w_ref[0, :, c0:c0 + n].astype(BF16)
            dst += n
    ba = w_ref[0, :, OFF_BA_SRC:OFF_BA_SRC + LANE]
    lane = lax.broadcasted_iota(jnp.int32, ba.shape, 1)
    ba_ref[0] = jnp.where(lane < N_BA, ba, 0.0).astype(BF16)


def _regroup_weights(w_in, rows=256):
    depth, d_in, n_in = w_in.shape
    return pl.pallas_call(
        _regroup_kernel,
        grid=(depth, d_in // rows),
        in_specs=[pl.BlockSpec((1, rows, n_in), lambda l, r: (l, r, 0))],
        out_specs=[pl.BlockSpec((1, rows, P_MAIN), lambda l, r: (l, r, 0)),
                   pl.BlockSpec((1, rows, LANE), lambda l, r: (l, r, 0))],
        out_shape=[jax.ShapeDtypeStruct((depth, d_in, P_MAIN), BF16),
                   jax.ShapeDtypeStruct((depth, d_in, LANE), BF16)],
        compiler_params=_params(("parallel", "parallel")),
        name="regroup_weights",
    )(w_in)


def _mod_row(mod_ref, tiles_per_batch, mod_row0):
    if tiles_per_batch is None:
        return mod_ref[0, 0:1, :]
    row = mod_row0 + pl.program_id(0) // tiles_per_batch
    return mod_ref[0, pl.ds(row, 1), :]


def _proj_kernel(*refs, tiles_per_batch, mod_row0, kv_tile, n_alias):
    x_ref, mod_ref, g_ref, w_ref, wba_ref, alog_ref, dtb_ref = refs[:7]
    o_ref, bl_ref = refs[7 + n_alias:9 + n_alias]
    k_ref, v_ref = refs[9 + n_alias:11 + n_alias] if kv_tile is not None else (None, None)
    h_ref = refs[-1]

    @pl.when(pl.program_id(1) == 0)
    def _():
        mod = _mod_row(mod_ref, tiles_per_batch, mod_row0)
        shift = mod[:, 0:D_MODEL]
        scale = mod[:, D_MODEL:2 * D_MODEL]
        x = x_ref[...]
        y = x * lax.rsqrt(jnp.mean(x * x, axis=-1, keepdims=True) + NORM_EPS) * g_ref[0]
        hb = (y * (1.0 + scale) + shift).astype(BF16)
        h_ref[...] = hb
        ba = _dot(hb, wba_ref[0])
        lane = lax.broadcasted_iota(jnp.int32, ba.shape, 1)
        bl_ref[...] = jnp.where(lane < 2 * DN_HEADS, _sigmoid(ba),
                                -jnp.exp(alog_ref[0]) * _softplus(ba + dtb_ref[0]))

    acc = _dot(h_ref[...], w_ref[0])
    o_ref[...] = acc.astype(BF16)
    if kv_tile is not None:
        @pl.when(pl.program_id(1) == kv_tile)
        def _():
            seqs, _, t_len, _ = k_ref.shape
            k_ref[:, 0] = acc[:, 0:NA_WIDTH].reshape(seqs, t_len, NA_WIDTH)
            v_ref[:, 0] = acc[:, NA_WIDTH:2 * NA_WIDTH].reshape(seqs, t_len, NA_WIDTH)


def _project(x2d, l, mod, g_pre, w_main, w_ba, alog_v, dtb_v, tiles_per_batch, mod_row0, kv, tm, tn=2048):
    ntok = x2d.shape[0]
    kv_tile = (CB_NA_K * LANE) // tn if kv is not None else None
    prev = [] if kv is None else list(kv)
    kern = functools.partial(_proj_kernel, tiles_per_batch=tiles_per_batch, mod_row0=mod_row0, kv_tile=kv_tile,
                             n_alias=len(prev))
    vec = pl.BlockSpec((1, 1, LANE), lambda m, n: (l, 0, 0))
    in_specs = [
        pl.BlockSpec((tm, D_MODEL), lambda m, n: (m, 0)),
        pl.BlockSpec((1, 8, 3 * D_MODEL), lambda m, n: (l, 0, 0)),
        pl.BlockSpec((1, 1, D_MODEL), lambda m, n: (l, 0, 0)),
        pl.BlockSpec((1, D_MODEL, tn), lambda m, n: (l, 0, n)),
        pl.BlockSpec((1, D_MODEL, LANE), lambda m, n: (l, 0, 0)),
        vec, vec,
    ] + [pl.BlockSpec(memory_space=pl.ANY)] * len(prev)
    out_specs = [pl.BlockSpec((tm, tn), lambda m, n: (m, n)),
                 pl.BlockSpec((tm, LANE), lambda m, n: (m, 0))]
    out_shape = [jax.ShapeDtypeStruct((ntok, P_MAIN), BF16),
                 jax.ShapeDtypeStruct((ntok, LANE), F32)]
    aliases = {}
    if kv is not None:
        t_len = prev[0].shape[2]
        out_specs += [pl.BlockSpec((tm // t_len, 1, t_len, NA_WIDTH), lambda m, n: (m, l, 0, 0))] * 2
        out_shape += [jax.ShapeDtypeStruct(p.shape, p.dtype) for p in prev]
        aliases = {7 + i: 2 + i for i in range(len(prev))}
    return pl.pallas_call(
        kern,
        grid=(ntok // tm, P_MAIN // tn),
        in_specs=in_specs,
        out_specs=out_specs,
        out_shape=out_shape,
        scratch_shapes=[pltpu.VMEM((tm, D_MODEL), BF16)],
        input_output_aliases=aliases,
        compiler_params=_params(("parallel", "arbitrary")),
        name="in_proj",
    )(x2d, mod, g_pre, w_main, w_ba, alog_v, dtb_v, *prev)


def _delta_kernel(*refs, T, HB, NU, has_s0, want_state):
    it = iter(refs)
    q_ref, k_ref, v_ref, z_ref, bl_ref = (next(it) for _ in range(5))
    cwq_ref, cwk_ref, cwv_ref, gn_ref = (next(it) for _ in range(4))
    s0_ref = next(it) if has_s0 else None
    if want_state:
        next(it)
    y_ref = next(it)
    sfin_ref = next(it) if want_state else None
    pad_ref, qn_ref, kn_ref, vn_ref, aq_ref, bm_ref, oacc_ref, gl_ref = (next(it) for _ in range(8))

    C = DN_CHUNK
    NC = T // C
    RB = min(T, 256)
    UR = DN_UNIT * C
    head0 = pl.program_id(1) * HB

    zeros8 = jnp.zeros((8, LANE), F32)
    pad_ref[0:8, :] = zeros8
    pad_ref[T + 8:T + 16, :] = zeros8

    def conv_into(x_ref, cw_ref, dst_ref, hh, mode):
        cols = slice(hh * LANE, (hh + 1) * LANE)
        for r0 in range(0, T, RB):
            pad_ref[8 + r0:8 + r0 + RB, :] = x_ref[0, r0:r0 + RB, cols].astype(F32)
        w = cw_ref[0, :, cols]
        for r0 in range(0, T, RB):
            y = (pad_ref[7 + r0:7 + r0 + RB, :] * w[0:1, :]
                 + pad_ref[8 + r0:8 + r0 + RB, :] * w[1:2, :]
                 + pad_ref[9 + r0:9 + r0 + RB, :] * w[2:3, :])
            y = _silu(y)
            if mode != "v":
                y = y * lax.rsqrt(jnp.sum(y * y, axis=-1, keepdims=True) + NORM_EPS)
            if mode == "q":
                y = y * (DN_DK ** -0.5)
            dst_ref[hh, r0:r0 + RB, :] = y

    for hh in range(HB):
        conv_into(q_ref, cwq_ref, qn_ref, hh, "q")
        conv_into(k_ref, cwk_ref, kn_ref, hh, "k")
        conv_into(v_ref, cwv_ref, vn_ref, hh, "v")

    u_r = lax.broadcasted_iota(jnp.int32, (UR, LANE), 0) & (C - 1)
    u_c = lax.broadcasted_iota(jnp.int32, (UR, LANE), 1)
    is_b = u_c >= C
    u_j = u_c & (C - 1)
    dist = jnp.where(is_b, u_j - u_r, u_r - u_j)
    incl_u = dist >= 0
    strict_u = dist > 0
    rowsum_u = dist <= 0
    blk = (jnp.right_shift(u_r, DN_BLK_SHIFT) == jnp.right_shift(u_j, DN_BLK_SHIFT))[0:C, :]
    bd_r = lax.broadcasted_iota(jnp.int32, (UR, UR), 0)
    bd_c = lax.broadcasted_iota(jnp.int32, (UR, UR), 1)
    same_chunk = jnp.right_shift(bd_r, 6) == jnp.right_shift(bd_c, 6)
    bd_lower = (same_chunk & (bd_r >= bd_c)).astype(BF16)
    bd_upper = (same_chunk & (bd_r <= bd_c)).astype(BF16)
    bd_ones = same_chunk.astype(BF16)
    half_l = (lax.broadcasted_iota(jnp.int32, (C, LANE), 1) < C)

    def split_hi_mid(x):
        hi = x.astype(BF16)
        mid = (x - hi.astype(F32)).astype(BF16)
        return jnp.concatenate([hi, mid], axis=1)

    def dot01(m01, parts):
        r = _dot(m01, parts)
        return r[:, 0:LANE] + r[:, LANE:2 * LANE]

    def block_diag(x):
        xb = x.astype(BF16)
        zb = jnp.zeros_like(xb)
        return jnp.concatenate([jnp.where(half_l, xb, zb), jnp.where(half_l, zb, xb)], axis=0)

    def mm2s(xs, ys):
        return [_dot(x.astype(BF16), block_diag(y)) for x, y in zip(xs, ys)]

    def tri_inv_minus_eye(lms):
        md = [jnp.where(blk, -lm, 0.0) for lm in lms]
        md2 = mm2s(md, md)
        md4 = mm2s(md2, md2)
        md3 = mm2s(md, md2)
        md8 = mm2s(md4, md4)
        px = [a + b + c for a, b, c in zip(md, md2, md3)]
        md12 = mm2s(md4, md8)
        qx = [a + b + c for a, b, c in zip(md4, md8, md12)]
        pq = mm2s(px, qx)
        dx = [a + b + c for a, b, c in zip(px, qx, pq)]
        loff = [jnp.where(blk, 0.0, lm) for lm in lms]
        dl = mm2s(dx, loff)
        n1 = [a + b for a, b in zip(loff, dl)]
        n2 = mm2s(n1, n1)
        n3 = mm2s(n1, n2)
        rx = [b - a - c for a, b, c in zip(n1, n2, n3)]
        rd = mm2s(rx, dx)
        return [a + b + c for a, b, c in zip(rx, dx, rd)]

    lane_u = u_c

    def pick(x, idx):
        return jnp.broadcast_to(jnp.sum(jnp.where(lane_u == idx, x, 0.0), axis=1, keepdims=True), x.shape)

    zeros_r = jnp.zeros((C, 2 * LANE), BF16)

    def local_group(g, carry):
        units = [(hh, g * NU + s) for hh in range(HB) for s in range(NU)]
        U = []
        for hh, un in units:
            rows = pl.ds(_aligned(un * UR, UR), UR)
            bl = bl_ref[0, rows, :]
            hg = head0 + hh
            U.append(dict(hh=hh, un=un, rows=rows,
                          bb_f=pick(bl, hg), bb_b=pick(bl, DN_HEADS + hg),
                          la_f=pick(bl, 2 * DN_HEADS + hg), la_b=pick(bl, 3 * DN_HEADS + hg)))
        for u in U:
            u["sf"] = split_hi_mid(u["la_f"])
            u["sb"] = split_hi_mid(u["la_b"])
            u["sr"] = split_hi_mid(jnp.where(rowsum_u, jnp.where(is_b, u["la_b"], u["la_f"]), 0.0))
        for u in U:
            u["gc_f"] = dot01(bd_lower, u["sf"])
        for u in U:
            u["gc_b"] = dot01(bd_upper, u["sb"])
        for u in U:
            u["g_row"] = dot01(bd_ones, u["sr"])
        for u in U:
            hh, rows = u["hh"], u["rows"]
            u["k"] = kn_ref[hh, rows, :]
            u["q"] = qn_ref[hh, rows, :]
            u["k16"] = u["k"].astype(BF16)
            u["q16"] = u["q"].astype(BF16)
        chunks = [(u, ci) for u in U for ci in range(DN_UNIT)]
        crs = [slice(ci * C, (ci + 1) * C) for _, ci in chunks]
        grams = [_dot_nt(jnp.concatenate([u["k16"][cr], u["q16"][cr]], axis=0),
                         jnp.concatenate([u["k16"][cr], u["k16"][cr]], axis=0))
                 for (u, _), cr in zip(chunks, crs)]
        for u in U:
            u["decay"] = jnp.exp(jnp.where(incl_u, jnp.where(is_b, u["gc_b"], u["gc_f"]) - u["g_row"], NEG_INF))
            u["bb_p"] = jnp.where(is_b, u["bb_b"], u["bb_f"])
        lms = [jnp.where(strict_u[cr], gm[0:C] * u["bb_p"][cr] * u["decay"][cr], 0.0)
               for (u, _), cr, gm in zip(chunks, crs, grams)]
        attn = [(gm[C:2 * C] * u["decay"][cr]).astype(BF16) for (u, _), cr, gm in zip(chunks, crs, grams)]
        txs = tri_inv_minus_eye(lms)
        for u in U:
            v = vn_ref[u["hh"], u["rows"], :]
            eg_f = jnp.exp(u["gc_f"])
            eg_b = jnp.exp(u["gc_b"])
            kb_f = u["k"] * u["bb_f"]
            kb_b = u["k"] * u["bb_b"]
            u["rhs_f"] = jnp.concatenate([v * u["bb_f"], kb_f * eg_f], axis=1)
            u["rhs_b"] = jnp.concatenate([v * u["bb_b"], kb_b * eg_b], axis=1)
            u["r16_f"] = u["rhs_f"].astype(BF16)
            u["r16_b"] = u["rhs_b"].astype(BF16)
            u["qg_f"] = u["q"] * eg_f
            u["qg_b"] = u["q"] * eg_b
        uws = [_dot(tx.astype(BF16), jnp.concatenate(
                    [jnp.concatenate([u["r16_f"][cr], zeros_r], axis=1),
                     jnp.concatenate([zeros_r, u["r16_b"][cr]], axis=1)], axis=0))
               for (u, _), cr, tx in zip(chunks, crs, txs)]
        wu_f = [(u["rhs_f"][cr] + uw[:, 0:2 * LANE]) for (u, _), cr, uw in zip(chunks, crs, uws)]
        wu_b = [(u["rhs_b"][cr] + uw[:, 2 * LANE:4 * LANE]) for (u, _), cr, uw in zip(chunks, crs, uws)]
        wu16_f = [x.astype(BF16) for x in wu_f]
        wu16_b = [x.astype(BF16) for x in wu_b]
        aws = [_dot(a, jnp.concatenate([jnp.concatenate([xf, zeros_r], axis=1),
                                        jnp.concatenate([zeros_r, xb], axis=1)], axis=0))
               for a, xf, xb in zip(attn, wu16_f, wu16_b)]
        kg_f, kg_b, gls = [], [], []
        for (u, ci), cr in zip(chunks, crs):
            gl_f = u["gc_f"][ci * C + C - 1:ci * C + C, :]
            gl_b = u["gc_b"][ci * C:ci * C + 1, :]
            kg_f.append((u["k"][cr] * jnp.exp(gl_f - u["gc_f"][cr])).astype(BF16))
            kg_b.append((u["k"][cr] * jnp.exp(gl_b - u["gc_b"][cr])).astype(BF16))
            gls.append((jnp.exp(gl_f), jnp.exp(gl_b)))
        kwu_f = [_dot_tn(kg, x) for kg, x in zip(kg_f, wu16_f)]
        kwu_b = [_dot_tn(kg, x) for kg, x in zip(kg_b, wu16_b)]
        for idx, ((u, ci), cr) in enumerate(zip(chunks, crs)):
            hh = u["hh"]
            c = u["un"] * DN_UNIT + ci
            crow = pl.ds(_aligned(c * C, C), C)
            aw = aws[idx]
            for d, kwu, qg, a0 in ((0, kwu_f[idx], u["qg_f"], 0), (1, kwu_b[idx], u["qg_b"], 2 * LANE)):
                aq_ref[hh, d, c, 0:DN_DK, :] = kwu[:, LANE:2 * LANE].astype(BF16)
                aq_ref[hh, d, c, DN_DK:DN_DK + C, :] = (qg[cr] - aw[:, a0 + LANE:a0 + 2 * LANE]).astype(BF16)
                bm_ref[hh, d, c] = kwu[:, 0:LANE]
                oacc_ref[hh, d, crow, :] = aw[:, a0:a0 + LANE]
                gl_ref[hh, d, c] = jnp.broadcast_to(gls[idx][d], (8, LANE))
        return carry

    n_groups = T // (NU * UR)
    if n_groups == 1:
        local_group(0, 0)
    else:
        lax.fori_loop(0, n_groups, local_group, 0)

    def body(i, carry):
        cidx = (i, NC - 1 - i)
        rs = [_dot(aq_ref[hh, d, cidx[d]], carry[2 * hh + d].astype(BF16)) for hh in range(HB) for d in range(2)]
        new = []
        for hh in range(HB):
            for d in range(2):
                r = rs[2 * hh + d]
                c = cidx[d]
                rows = pl.ds(pl.multiple_of(c * C, C), C)
                oacc_ref[hh, d, rows, :] = oacc_ref[hh, d, rows, :] + r[DN_DK:DN_DK + C]
                new.append(carry[2 * hh + d] * gl_ref[hh, d, c, 0:1, :] + bm_ref[hh, d, c] - r[0:DN_DK])
        return tuple(new)

    if has_s0:
        init = tuple(s0_ref[0, 0, d, hh] for hh in range(HB) for d in range(2))
    else:
        init = tuple(jnp.zeros((DN_DK, LANE), F32) for _ in range(2 * HB))
    fin = lax.fori_loop(0, NC, body, init, unroll=min(NC, DN_SEQ_UNROLL))
    if want_state:
        for hh in range(HB):
            for d in range(2):
                sfin_ref[0, 0, d, hh] = fin[2 * hh + d]

    for hh in range(HB):
        cols = slice(hh * LANE, (hh + 1) * LANE)
        for r0 in range(0, T, RB):
            o = oacc_ref[hh, 0, r0:r0 + RB, :] + oacc_ref[hh, 1, r0:r0 + RB, :]
            o = o * lax.rsqrt(jnp.mean(o * o, axis=-1, keepdims=True) + NORM_EPS) * gn_ref[0]
            y_ref[0, r0:r0 + RB, cols] = (o * _silu(z_ref[0, r0:r0 + RB, cols].astype(F32))).astype(BF16)


def _delta_mixer(proj3, bl3, l, conv_w, g_norm, s0, s_prev, HB, NU):
    B, T, _ = proj3.shape
    has_s0 = s0 is not None
    want_state = s_prev is not None
    NC = T // DN_CHUNK
    W = HB * LANE
    kern = functools.partial(_delta_kernel, T=T, HB=HB, NU=NU, has_s0=has_s0, want_state=want_state)
    col = lambda cb: pl.BlockSpec((1, T, W), lambda b, h, cb=cb: (b, 0, cb // HB + h))
    cw = lambda cb: pl.BlockSpec((1, 3, W), lambda b, h, cb=cb: (l, 0, cb // HB + h))
    in_specs = [col(CB_DN_Q), col(CB_DN_K), col(CB_DN_V), col(CB_DN_Z),
                pl.BlockSpec((1, T, LANE), lambda b, h: (b, 0, 0)),
                cw(0), cw(4), cw(8), pl.BlockSpec((1, 1, LANE), lambda b, h: (l, 0, 0))]
    args = [proj3, proj3, proj3, proj3, bl3, conv_w, conv_w, conv_w, g_norm]
    if has_s0:
        in_specs.append(pl.BlockSpec((1, 1, 2, HB, DN_DK, LANE), lambda b, h: (b, l, 0, h, 0, 0)))
        args.append(s0)
    aliases = {}
    if want_state:
        aliases = {len(args): 1}
        in_specs.append(pl.BlockSpec(memory_space=pl.ANY))
        args.append(s_prev)
    out_specs = [pl.BlockSpec((1, T, W), lambda b, h: (b, 0, h))]
    out_shape = [jax.ShapeDtypeStruct((B, T, DN_WIDTH), BF16)]
    if want_state:
        out_specs.append(pl.BlockSpec((1, 1, 2, HB, DN_DK, LANE), lambda b, h: (b, l, 0, h, 0, 0)))
        out_shape.append(jax.ShapeDtypeStruct(s_prev.shape, s_prev.dtype))
    res = pl.pallas_call(
        kern,
        grid=(B, DN_HEADS // HB),
        in_specs=in_specs,
        out_specs=out_specs,
        out_shape=out_shape,
        scratch_shapes=[
            pltpu.VMEM((T + 16, LANE), F32),
            pltpu.VMEM((HB, T, LANE), F32), pltpu.VMEM((HB, T, LANE), F32), pltpu.VMEM((HB, T, LANE), F32),
            pltpu.VMEM((HB, 2, NC, DN_DK + DN_CHUNK, LANE), BF16),
            pltpu.VMEM((HB, 2, NC, DN_DK, LANE), F32),
            pltpu.VMEM((HB, 2, T, LANE), F32),
            pltpu.VMEM((HB, 2, NC, 8, LANE), F32),
        ],
        input_output_aliases=aliases,
        compiler_params=_params(("parallel", "parallel")),
        name="delta_mixer",
    )(*args)
    return (res[0], res[1]) if want_state else (res[0], None)


def _head_masks(rows, dtype, value=1.0):
    lane = lax.broadcasted_iota(jnp.int32, (rows, LANE), 1)
    return (jnp.where(lane < NA_HD, value, 0.0).astype(dtype), jnp.where(lane >= NA_HD, value, 0.0).astype(dtype))


NA_SCALE = NA_HD ** -0.5


def _ctx_attn_body(q_ref, k_ref, v_ref, z_ref, y_ref, T):
    m0, m1 = _head_masks(T, BF16, NA_SCALE)
    f0, f1 = _head_masks(T, F32)
    cols = [slice(p * LANE, (p + 1) * LANE) for p in range(NA_WIDTH // LANE)]
    pairs = [(nb, c) for nb in range(q_ref.shape[0]) for c in cols]
    s = [_dot_nt(q_ref[nb, :, c] * hm, k_ref[nb, :, c]) for nb, c in pairs for hm in (m0, m1)]
    e = [jnp.exp(x - jnp.max(x, axis=-1, keepdims=True)) for x in s]
    o = [_dot(e[2 * i + hh].astype(BF16), v_ref[nb, :, c]) for i, (nb, c) in enumerate(pairs) for hh in range(2)]
    o = [x / jnp.sum(ee, axis=-1, keepdims=True) for x, ee in zip(o, e)]
    for i, (nb, c) in enumerate(pairs):
        out = o[2 * i] * f0 + o[2 * i + 1] * f1
        y_ref[nb, :, c] = (out * _silu(z_ref[nb, :, c].astype(F32))).astype(BF16)


def _na_kernel(q_ref, k_ref, v_ref, z_ref, kc_ref, vc_ref, bias_ref, y_ref, sctx_ref, *, T, L):
    W = GRID_W
    rows_total = T // W
    n_win = NA_WIN_R * W
    RS = NA_ROWS_PER_STEP
    m0, m1 = _head_masks(W, BF16, NA_SCALE)
    f0, f1 = _head_masks(W, F32)
    kcb = kc_ref[0, 0].astype(BF16)
    vcb = vc_ref[0, 0].astype(BF16)

    RB = 256
    mb0, mb1 = _head_masks(RB, BF16, NA_SCALE)
    for r0 in range(0, T, RB):
        q = q_ref[0, r0:r0 + RB, :]
        sctx_ref[0, r0:r0 + RB, :] = _dot_nt(q * mb0, kcb)
        sctx_ref[1, r0:r0 + RB, :] = _dot_nt(q * mb1, kcb)

    def step(i, carry):
        rr = [i * RS + j for j in range(RS)]
        rs_ = [jnp.clip(r - NA_WIN_R // 2, 0, rows_total - NA_WIN_R) for r in rr]
        qrows = [pl.ds(pl.multiple_of(r * W, W), W) for r in rr]
        krows = [pl.ds(pl.multiple_of(rs * W, W), n_win) for rs in rs_]
        qs = []
        for qr in qrows:
            q = q_ref[0, qr, :]
            qs.append(jnp.concatenate([q * m0, q * m1], axis=0))
        s = [_dot_nt(x, k_ref[0, kr, :]) for x, kr in zip(qs, krows)]
        e1, e2, den = [], [], []
        for j in range(RS):
            dr0 = rs_[j] - rr[j] + NA_WIN_R - 1
            bias = jnp.concatenate(
                [jnp.concatenate([bias_ref[0, hh, dr0 + 2 * m] for m in range(NA_WIN_R // 2)], axis=1)
                 for hh in range(2)], axis=0)
            sl = s[j] + bias
            sc = jnp.concatenate([sctx_ref[0, qrows[j], :], sctx_ref[1, qrows[j], :]], axis=0)
            mx = jnp.maximum(jnp.max(sl, axis=-1, keepdims=True), jnp.max(sc, axis=-1, keepdims=True))
            a = jnp.exp(sl - mx)
            b = jnp.exp(sc - mx)
            e1.append(a.astype(BF16))
            e2.append(b.astype(BF16))
            den.append(jnp.sum(a, axis=-1, keepdims=True) + jnp.sum(b, axis=-1, keepdims=True))
        o = [_dot(a, v_ref[0, kr, :]) + _dot(b, vcb) for a, b, kr in zip(e1, e2, krows)]
        for j in range(RS):
            oj = o[j] / den[j]
            out = oj[0:W, :] * f0 + oj[W:2 * W, :] * f1
            y_ref[0, qrows[j], :] = (out * _silu(z_ref[0, qrows[j], :].astype(F32))).astype(BF16)
        return carry

    lax.fori_loop(0, rows_total // RS, step, 0)


def _neighbourhood_attention(proj3, l, k_ctx, v_ctx, bias2):
    B, T, _ = proj3.shape
    L = k_ctx.shape[2]
    col = lambda cb: pl.BlockSpec((1, T, LANE), lambda b, p, cb=cb: (b, 0, cb + p))
    ctx = pl.BlockSpec((1, 1, L, LANE), lambda b, p: (b, l, 0, p))
    return pl.pallas_call(
        functools.partial(_na_kernel, T=T, L=L),
        grid=(B, NA_WIDTH // LANE),
        in_specs=[col(CB_NA_Q), col(CB_NA_K), col(CB_NA_V), col(CB_NA_Z), ctx, ctx,
                  pl.BlockSpec((1, 2, 2 * NA_WIN_R - 2, GRID_W, LANE), lambda b, p: (l, p, 0, 0, 0))],
        out_specs=pl.BlockSpec((1, T, LANE), lambda b, p: (b, 0, p)),
        out_shape=jax.ShapeDtypeStruct((B, T, NA_WIDTH), BF16),
        scratch_shapes=[pltpu.VMEM((2, T, L), F32)],
        compiler_params=_params(("parallel", "parallel")),
        name="nbr_attention",
    )(proj3, proj3, proj3, proj3, k_ctx, v_ctx, bias2)


def _pool_body(u_ref, z_ref, band_ref, pw_ref, ps_ref, y_ref, x_buf, nb, T):
    P = POOL_HALO
    RB = POOL_ROWS
    zeros = jnp.zeros((P, POOL_WIDTH), BF16)
    x_buf[nb, 0:P, :] = zeros
    x_buf[nb, T + P:T + 2 * P, :] = zeros
    for r0 in range(0, T, RB):
        x_buf[nb, P + r0:P + r0 + RB, :] = u_ref[nb, r0:r0 + RB, :]
    groups = list(enumerate(POOL_WINDOWS))
    cols = [slice(g * POOL_GC, (g + 1) * POOL_GC) for g, _ in groups]
    for r0 in range(0, T, RB):
        t = r0 + lax.broadcasted_iota(jnp.int32, (RB, 1), 0)
        tots = [_dot(band_ref[g], x_buf[nb, r0:r0 + RB + 2 * P, cols[g]]) for g, _ in groups]
        pooled = []
        for g, win in groups:
            lo = jnp.maximum(t - win // 2, 0)
            hi = jnp.minimum(t + win // 2 - 1, T - 1)
            inv_cnt = 1.0 / (hi - lo + 1).astype(F32)
            pooled.append((tots[g] * inv_cnt - u_ref[nb, r0:r0 + RB, cols[g]].astype(F32)).astype(BF16))
        ys = [_dot(pooled[g], pw_ref[0, g]) for g, _ in groups]
        for g, _ in groups:
            y = ys[g] * ps_ref[0, :, cols[g]]
            y_ref[nb, r0:r0 + RB, cols[g]] = (y * _silu(z_ref[nb, r0:r0 + RB, cols[g]].astype(F32))).astype(BF16)


def _pool_kernel(u_ref, z_ref, band_ref, pw_ref, ps_ref, y_ref, x_buf, *, T):
    _pool_body(u_ref, z_ref, band_ref, pw_ref, ps_ref, y_ref, x_buf, 0, T)


def _ctx_mixers_kernel(q_ref, k_ref, v_ref, z_ref, u_ref, zp_ref, band_ref, pw_ref, ps_ref, yna_ref, ypl_ref,
                       x_buf, *, T):
    _ctx_attn_body(q_ref, k_ref, v_ref, z_ref, yna_ref, T)
    for nb in range(u_ref.shape[0]):
        _pool_body(u_ref, zp_ref, band_ref, pw_ref, ps_ref, ypl_ref, x_buf, nb, T)


def _pool_bands():
    off = np.arange(POOL_ROWS + 2 * POOL_HALO)[None, :] - POOL_HALO - np.arange(POOL_ROWS)[:, None]
    return jnp.asarray(np.stack([(off >= -(w // 2)) & (off <= w // 2 - 1) for w in POOL_WINDOWS]), BF16)


def _pool_mixer(proj3, l, pool_w, pool_scale):
    B, T, _ = proj3.shape
    wide = POOL_WIDTH // LANE
    bands = _pool_bands()
    return pl.pallas_call(
        functools.partial(_pool_kernel, T=T),
        grid=(B,),
        in_specs=[
            pl.BlockSpec((1, T, POOL_WIDTH), lambda b: (b, 0, CB_PL_U // wide)),
            pl.BlockSpec((1, T, POOL_WIDTH), lambda b: (b, 0, CB_PL_Z // wide)),
            pl.BlockSpec(bands.shape, lambda b: (0, 0, 0)),
            pl.BlockSpec((1, len(POOL_WINDOWS), POOL_GC, POOL_GC), lambda b: (l, 0, 0, 0)),
            pl.BlockSpec((1, 1, POOL_WIDTH), lambda b: (l, 0, 0)),
        ],
        out_specs=pl.BlockSpec((1, T, POOL_WIDTH), lambda b: (b, 0, 0)),
        out_shape=jax.ShapeDtypeStruct((B, T, POOL_WIDTH), BF16),
        scratch_shapes=[pltpu.VMEM((1, T + 2 * POOL_HALO, POOL_WIDTH), BF16)],
        compiler_params=_params(("parallel",)),
        name="pool_mixer",
    )(proj3, proj3, bands, pool_w, pool_scale)


def _ctx_mixers(proj3, l, pool_w, pool_scale, nb=4):
    B, T, _ = proj3.shape
    wide = NA_WIDTH // LANE
    bands = _pool_bands()
    spec = lambda cb: pl.BlockSpec((nb, T, NA_WIDTH), lambda b, cb=cb: (b, 0, cb // wide))
    out = pl.BlockSpec((nb, T, NA_WIDTH), lambda b: (b, 0, 0))
    return pl.pallas_call(
        functools.partial(_ctx_mixers_kernel, T=T),
        grid=(B // nb,),
        in_specs=[spec(CB_NA_Q), spec(CB_NA_K), spec(CB_NA_V), spec(CB_NA_Z), spec(CB_PL_U), spec(CB_PL_Z),
                  pl.BlockSpec(bands.shape, lambda b: (0, 0, 0)),
                  pl.BlockSpec((1, len(POOL_WINDOWS), POOL_GC, POOL_GC), lambda b: (l, 0, 0, 0)),
                  pl.BlockSpec((1, 1, POOL_WIDTH), lambda b: (l, 0, 0))],
        out_specs=[out, out],
        out_shape=[jax.ShapeDtypeStruct((B, T, NA_WIDTH), BF16), jax.ShapeDtypeStruct((B, T, POOL_WIDTH), BF16)],
        scratch_shapes=[pltpu.VMEM((nb, T + 2 * POOL_HALO, POOL_WIDTH), BF16)],
        compiler_params=_params(("parallel",)),
        name="ctx_mixers",
    )(proj3, proj3, proj3, proj3, proj3, proj3, bands, pool_w, pool_scale)


def _merge_kernel(ydn_ref, yna_ref, ypl_ref, gdn_ref, gna_ref, gpl_ref, x_ref, mod_ref, gpost_ref,
                  wd_ref, wn_ref, wp_ref, wo_ref, o_ref, *, tiles_per_batch, mod_row0):
    merged = (_sigmoid(gdn_ref[...].astype(F32)) * _dot(ydn_ref[...], wd_ref[0])
              + _sigmoid(gna_ref[...].astype(F32)) * _dot(yna_ref[...], wn_ref[0])
              + _sigmoid(gpl_ref[...].astype(F32)) * _dot(ypl_ref[...], wp_ref[0]))
    out = _dot(merged.astype(BF16), wo_ref[0])
    out = out * lax.rsqrt(jnp.mean(out * out, axis=-1, keepdims=True) + NORM_EPS) * gpost_ref[0]
    gate = _mod_row(mod_ref, tiles_per_batch, mod_row0)[:, 2 * D_MODEL:3 * D_MODEL]
    o_ref[...] = x_ref[...] + gate * out


def _merge(y_dn, y_na, y_pl, proj2d, x2d, l, mod, g_post, w_dn, w_na, w_pl, w_out, tiles_per_batch, mod_row0, tm):
    ntok = x2d.shape[0]
    g0 = CB_GATE * LANE // D_MODEL
    br = pl.BlockSpec((tm, DN_WIDTH), lambda m: (m, 0))
    gate = lambda j: pl.BlockSpec((tm, D_MODEL), lambda m, j=j: (m, g0 + j))
    wbr = pl.BlockSpec((1, DN_WIDTH, D_MODEL), lambda m: (l, 0, 0))
    return pl.pallas_call(
        functools.partial(_merge_kernel, tiles_per_batch=tiles_per_batch, mod_row0=mod_row0),
        grid=(ntok // tm,),
        in_specs=[br, br, br, gate(0), gate(1), gate(2),
                  pl.BlockSpec((tm, D_MODEL), lambda m: (m, 0)),
                  pl.BlockSpec((1, 8, 3 * D_MODEL), lambda m: (l, 0, 0)),
                  pl.BlockSpec((1, 1, D_MODEL), lambda m: (l, 0, 0)),
                  wbr, wbr, wbr,
                  pl.BlockSpec((1, D_MODEL, D_MODEL), lambda m: (l, 0, 0))],
        out_specs=pl.BlockSpec((tm, D_MODEL), lambda m: (m, 0)),
        out_shape=jax.ShapeDtypeStruct((ntok, D_MODEL), F32),
        compiler_params=_params(("parallel",)),
        name="merge_out",
    )(y_dn, y_na, y_pl, proj2d, proj2d, proj2d, x2d, mod, g_post, w_dn, w_na, w_pl, w_out)


def _layer(x3, l, mod_row0, s0, k_ctx, v_ctx, pw, tm_proj, tm_merge, delta_cfg, carry=None):
    B, T, _ = x3.shape
    n = B * T
    x2d = x3.reshape(n, D_MODEL)
    is_ctx = k_ctx is None
    res = _project(x2d, l, pw["mod"], pw["g_pre"], pw["w_main"], pw["w_ba"], pw["alog_v"], pw["dtb_v"],
                   None if is_ctx else T // tm_proj, mod_row0,
                   kv=carry[:2] if is_ctx else None, tm=tm_proj)
    proj2d, bl2d = res[0], res[1]
    proj3 = proj2d.reshape(B, T, P_MAIN)
    y_dn, s_fin = _delta_mixer(proj3, bl2d.reshape(B, T, LANE), l, pw["conv_w"], pw["g_norm"], s0,
                               s_prev=carry[2] if is_ctx else None,
                               HB=delta_cfg[0], NU=delta_cfg[1])
    if is_ctx:
        y_na, y_pl = _ctx_mixers(proj3, l, pw["pool_w"], pw["pool_scale"])
    else:
        y_na = _neighbourhood_attention(proj3, l, k_ctx, v_ctx, pw["bias2"])
        y_pl = _pool_mixer(proj3, l, pw["pool_w"], pw["pool_scale"])
    out2d = _merge(y_dn.reshape(n, DN_WIDTH), y_na.reshape(n, NA_WIDTH), y_pl.reshape(n, POOL_WIDTH), proj2d,
                   x2d, l, pw["mod"], pw["g_post"], pw["w_dn"], pw["w_na"], pw["w_pl"], pw["w_out"],
                   None if is_ctx else T // tm_merge, mod_row0, tm=tm_merge)
    return out2d.reshape(B, T, D_MODEL), (tuple(res[2:]) + (s_fin,) if is_ctx else None)


def _lane_vecs(v, offset):
    return jnp.zeros((DEPTH, 1, LANE), F32).at[:, 0, offset:offset + 2 * DN_HEADS].set(v.reshape(DEPTH, -1))


def _bias_windows(na_bias):
    period = 2 * GRID_W - 1
    half = NA_WIN_C - 1
    zeros = jnp.zeros(na_bias.shape[:-1] + (period - 2 * half - 1,), F32)
    r = jnp.concatenate([na_bias[..., half:], zeros, na_bias[..., :half]], axis=-1)
    rows = jnp.tile(r, (1, 1, 1, GRID_W))[..., :GRID_W * (period - 1)]
    bias_t = rows.reshape(na_bias.shape[:-1] + (GRID_W, period - 1))[..., :GRID_W]
    cols = np.arange(GRID_W)
    cs = np.clip(cols - NA_WIN_C // 2, 0, GRID_W - NA_WIN_C)[:, None]
    col_ok = (cols[None, :] >= cs) & (cols[None, :] < cs + NA_WIN_C)
    bias_t = jnp.where(col_ok, bias_t, NEG_INF)
    return jnp.concatenate([bias_t[:, :, :-1], bias_t[:, :, 1:]], axis=-1)


def kernel(x_prompt, x_sample, c, cache_k_na, cache_v_na, state_dn, c_ctx, w_ada, b_ada, g_pre, g_post, w_in,
           conv_dn, a_log_dn, dt_bias_dn, g_norm_dn, na_bias, pool_w, pool_scale, w_br_dn, w_br_na, w_br_pl,
           w_out):
    B, T, _ = x_prompt.shape
    DB, DT, _ = x_sample.shape
    L = cache_k_na.shape[2]

    cc = jnp.zeros((8, D_MODEL), F32).at[0].set(c_ctx).at[1:1 + DB].set(c)
    w_main, w_ba = _regroup_weights(w_in)
    pw = dict(
        mod=_modulation(cc, w_ada, b_ada),
        g_pre=g_pre[:, None], g_post=g_post[:, None], g_norm=g_norm_dn[:, None], pool_scale=pool_scale[:, None],
        w_main=w_main, w_ba=w_ba,
        alog_v=_lane_vecs(a_log_dn, 2 * DN_HEADS), dtb_v=_lane_vecs(dt_bias_dn, 2 * DN_HEADS),
        conv_w=conv_dn, bias2=_bias_windows(na_bias), pool_w=pool_w.astype(BF16),
        w_dn=w_br_dn.astype(BF16), w_na=w_br_na.astype(BF16), w_pl=w_br_pl.astype(BF16), w_out=w_out.astype(BF16))
    k_ctx = cache_k_na.reshape(DB, DEPTH, L, NA_WIDTH)
    v_ctx = cache_v_na.reshape(DB, DEPTH, L, NA_WIDTH)

    xp, xs = x_prompt, x_sample
    carry = (jnp.zeros((B, DEPTH, T, NA_WIDTH), F32), jnp.zeros((B, DEPTH, T, NA_WIDTH), F32),
             jnp.zeros((B, DEPTH, 2, DN_HEADS, DN_DK, LANE), F32))
    for l in range(DEPTH):
        xp, carry = _layer(xp, l, 0, None, None, None, pw, tm_proj=1024, tm_merge=1024, delta_cfg=(4, 1),
                           carry=carry)
        xs, _ = _layer(xs, l, 1, state_dn, k_ctx, v_ctx, pw, tm_proj=1024, tm_merge=1024, delta_cfg=(2, 2))
    k_all, v_all, s_all = carry
    return (xp, xs, k_all.reshape(B, DEPTH, T, NA_HEADS, NA_HD), v_all.reshape(B, DEPTH, T, NA_HEADS, NA_HD), s_all)
```

```python
import functools

import numpy as np
import jax
import jax.numpy as jnp
from jax import lax
from jax.experimental import pallas as pl
from jax.experimental.pallas import tpu as pltpu

F32 = jnp.float32
BF16 = jnp.bfloat16

D_MODEL = 1024
DEPTH = 4
GRID_W = 64
NORM_EPS = 1e-6
NEG_INF = -1e30

DN_HEADS = 4
DN_DK = 128
DN_WIDTH = 512
DN_CHUNK = 64
DN_BLK_SHIFT = 4
DN_UNIT = 4
DN_SEQ_UNROLL = 4

NA_HEADS = 8
NA_HD = 64
NA_WIDTH = 512
NA_WIN_R = 8
NA_WIN_C = 16
NA_ROWS_PER_STEP = 8

POOL_WINDOWS = (2, 4, 8, 16)
POOL_GC = 128
POOL_WIDTH = 512
POOL_HALO = 16
POOL_ROWS = 256

LANE = 128
P_MAIN = 8192
CB_DN_Q, CB_DN_K, CB_DN_V, CB_DN_Z = 0, 4, 8, 12
CB_NA_K, CB_NA_V = 16, 20
CB_NA_Q, CB_NA_Z = 24, 28
CB_PL_U, CB_PL_Z = 32, 36
CB_GATE = 40
SRC_COLS = ((0, 2048), (2576, 3600), (2064, 2576), (3600, 8208))
OFF_BA_SRC = 2048
N_BA = 16

VMEM_LIMIT = 56 * 1024 * 1024


def _sigmoid(x):
    return 0.5 * jnp.tanh(0.5 * x) + 0.5


def _silu(x):
    return x * _sigmoid(x)


def _softplus(x):
    return jnp.maximum(x, 0.0) + jnp.log1p(jnp.exp(-jnp.abs(x)))


def _dot(a, b):
    return jnp.dot(a, b, preferred_element_type=F32)


def _dot_nt(a, b):
    return lax.dot_general(a, b, (((1,), (1,)), ((), ())), preferred_element_type=F32)


def _dot_tn(a, b):
    return lax.dot_general(a, b, (((0,), (0,)), ((), ())), preferred_element_type=F32)


def _aligned(x, m):
    return x if isinstance(x, int) else pl.multiple_of(x, m)


def _params(sem):
    return pltpu.CompilerParams(dimension_semantics=sem, vmem_limit_bytes=VMEM_LIMIT)


def _mod_kernel(cc_ref, w_ref, b_ref, o_ref):
    a = _silu(cc_ref[...]).astype(BF16)
    o_ref[0] = _dot(a, w_ref[0].astype(BF16)) + b_ref[0]


def _modulation(cc, w_ada, b_ada):
    tn = 1024
    return pl.pallas_call(
        _mod_kernel,
        grid=(DEPTH, 3 * D_MODEL // tn),
        in_specs=[
            pl.BlockSpec((8, D_MODEL), lambda l, n: (0, 0)),
            pl.BlockSpec((1, D_MODEL, tn), lambda l, n: (l, 0, n)),
            pl.BlockSpec((1, 1, tn), lambda l, n: (l, 0, n)),
        ],
        out_specs=pl.BlockSpec((1, 8, tn), lambda l, n: (l, 0, n)),
        out_shape=jax.ShapeDtypeStruct((DEPTH, 8, 3 * D_MODEL), F32),
        compiler_params=_params(("parallel", "parallel")),
        name="adaln_mod",
    )(cc, w_ada, b_ada.reshape(DEPTH, 1, 3 * D_MODEL))


def _regroup_kernel(w_ref, o_ref, ba_ref):
    dst = 0
    step = 4 * LANE
    for a, b in SRC_COLS:
        for c0 in range(a, b, step):
            n = min(step, b - c0)
            o_ref[0, :, dst:dst + n] = w_ref[0, :, c0:c0 + n].astype(BF16)
            dst += n
    ba = w_ref[0, :, OFF_BA_SRC:OFF_BA_SRC + LANE]
    lane = lax.broadcasted_iota(jnp.int32, ba.shape, 1)
    ba_ref[0] = jnp.where(lane < N_BA, ba, 0.0).astype(BF16)


def _regroup_weights(w_in, rows=256):
    depth, d_in, n_in = w_in.shape
    return pl.pallas_call(
        _regroup_kernel,
        grid=(depth, d_in // rows),
        in_specs=[pl.BlockSpec((1, rows, n_in), lambda l, r: (l, r, 0))],
        out_specs=[pl.BlockSpec((1, rows, P_MAIN), lambda l, r: (l, r, 0)),
                   pl.BlockSpec((1, rows, LANE), lambda l, r: (l, r, 0))],
        out_shape=[jax.ShapeDtypeStruct((depth, d_in, P_MAIN), BF16),
                   jax.ShapeDtypeStruct((depth, d_in, LANE), BF16)],
        compiler_params=_params(("parallel", "parallel")),
        name="regroup_weights",
    )(w_in)


def _mod_row(mod_ref, tiles_per_batch, mod_row0):
    if tiles_per_batch is None:
        return mod_ref[0, 0:1, :]
    row = mod_row0 + pl.program_id(0) // tiles_per_batch
    return mod_ref[0, pl.ds(row, 1), :]


def _proj_kernel(*refs, tiles_per_batch, mod_row0, kv_tile, n_alias):
    x_ref, mod_ref, g_ref, w_ref, wba_ref, alog_ref, dtb_ref = refs[:7]
    o_ref, bl_ref = refs[7 + n_alias:9 + n_alias]
    k_ref, v_ref = refs[9 + n_alias:11 + n_alias] if kv_tile is not None else (None, None)
    h_ref = refs[-1]

    @pl.when(pl.program_id(1) == 0)
    def _():
        mod = _mod_row(mod_ref, tiles_per_batch, mod_row0)
        shift = mod[:, 0:D_MODEL]
        scale = mod[:, D_MODEL:2 * D_MODEL]
        x = x_ref[...]
        y = x * lax.rsqrt(jnp.mean(x * x, axis=-1, keepdims=True) + NORM_EPS) * g_ref[0]
        hb = (y * (1.0 + scale) + shift).astype(BF16)
        h_ref[...] = hb
        ba = _dot(hb, wba_ref[0])
        lane = lax.broadcasted_iota(jnp.int32, ba.shape, 1)
        bl_ref[...] = jnp.where(lane < 2 * DN_HEADS, _sigmoid(ba),
                                -jnp.exp(alog_ref[0]) * _softplus(ba + dtb_ref[0]))

    acc = _dot(h_ref[...], w_ref[0])
    o_ref[...] = acc.astype(BF16)
    if kv_tile is not None:
        @pl.when(pl.program_id(1) == kv_tile)
        def _():
            seqs, _, t_len, _ = k_ref.shape
            k_ref[:, 0] = acc[:, 0:NA_WIDTH].reshape(seqs, t_len, NA_WIDTH)
            v_ref[:, 0] = acc[:, NA_WIDTH:2 * NA_WIDTH].reshape(seqs, t_len, NA_WIDTH)


def _project(x2d, l, mod, g_pre, w_main, w_ba, alog_v, dtb_v, tiles_per_batch, mod_row0, kv, tm, tn=2048):
    ntok = x2d.shape[0]
    kv_tile = (CB_NA_K * LANE) // tn if kv is not None else None
    prev = [] if kv is None else list(kv)
    kern = functools.partial(_proj_kernel, tiles_per_batch=tiles_per_batch, mod_row0=mod_row0, kv_tile=kv_tile,
                             n_alias=len(prev))
    vec = pl.BlockSpec((1, 1, LANE), lambda m, n: (l, 0, 0))
    in_specs = [
        pl.BlockSpec((tm, D_MODEL), lambda m, n: (m, 0)),
        pl.BlockSpec((1, 8, 3 * D_MODEL), lambda m, n: (l, 0, 0)),
        pl.BlockSpec((1, 1, D_MODEL), lambda m, n: (l, 0, 0)),
        pl.BlockSpec((1, D_MODEL, tn), lambda m, n: (l, 0, n)),
        pl.BlockSpec((1, D_MODEL, LANE), lambda m, n: (l, 0, 0)),
        vec, vec,
    ] + [pl.BlockSpec(memory_space=pl.ANY)] * len(prev)
    out_specs = [pl.BlockSpec((tm, tn), lambda m, n: (m, n)),
                 pl.BlockSpec((tm, LANE), lambda m, n: (m, 0))]
    out_shape = [jax.ShapeDtypeStruct((ntok, P_MAIN), BF16),
                 jax.ShapeDtypeStruct((ntok, LANE), F32)]
    aliases = {}
    if kv is not None:
        t_len = prev[0].shape[2]
        out_specs += [pl.BlockSpec((tm // t_len, 1, t_len, NA_WIDTH), lambda m, n: (m, l, 0, 0))] * 2
        out_shape += [jax.ShapeDtypeStruct(p.shape, p.dtype) for p in prev]
        aliases = {7 + i: 2 + i for i in range(len(prev))}
    return pl.pallas_call(
        kern,
        grid=(ntok // tm, P_MAIN // tn),
        in_specs=in_specs,
        out_specs=out_specs,
        out_shape=out_shape,
        scratch_shapes=[pltpu.VMEM((tm, D_MODEL), BF16)],
        input_output_aliases=aliases,
        compiler_params=_params(("parallel", "arbitrary")),
        name="in_proj",
    )(x2d, mod, g_pre, w_main, w_ba, alog_v, dtb_v, *prev)


def _delta_kernel(*refs, T, HB, NU, has_s0, want_state):
    it = iter(refs)
    q_ref, k_ref, v_ref, z_ref, bl_ref = (next(it) for _ in range(5))
    cwq_ref, cwk_ref, cwv_ref, gn_ref = (next(it) for _ in range(4))
    s0_ref = next(it) if has_s0 else None
    if want_state:
        next(it)
    y_ref = next(it)
    sfin_ref = next(it) if want_state else None
    pad_ref, qn_ref, kn_ref, vn_ref, aq_ref, bm_ref, oacc_ref, gl_ref = (next(it) for _ in range(8))

    C = DN_CHUNK
    NC = T // C
    RB = min(T, 256)
    UR = DN_UNIT * C
    head0 = pl.program_id(1) * HB

    zeros8 = jnp.zeros((8, LANE), F32)
    pad_ref[0:8, :] = zeros8
    pad_ref[T + 8:T + 16, :] = zeros8

    def conv_into(x_ref, cw_ref, dst_ref, hh, mode):
        cols = slice(hh * LANE, (hh + 1) * LANE)
        for r0 in range(0, T, RB):
            pad_ref[8 + r0:8 + r0 + RB, :] = x_ref[0, r0:r0 + RB, cols].astype(F32)
        w = cw_ref[0, :, cols]
        for r0 in range(0, T, RB):
            y = (pad_ref[7 + r0:7 + r0 + RB, :] * w[0:1, :]
                 + pad_ref[8 + r0:8 + r0 + RB, :] * w[1:2, :]
                 + pad_ref[9 + r0:9 + r0 + RB, :] * w[2:3, :])
            y = _silu(y)
            if mode != "v":
                y = y * lax.rsqrt(jnp.sum(y * y, axis=-1, keepdims=True) + NORM_EPS)
            if mode == "q":
                y = y * (DN_DK ** -0.5)
            dst_ref[hh, r0:r0 + RB, :] = y

    for hh in range(HB):
        conv_into(q_ref, cwq_ref, qn_ref, hh, "q")
        conv_into(k_ref, cwk_ref, kn_ref, hh, "k")
        conv_into(v_ref, cwv_ref, vn_ref, hh, "v")

    u_r = lax.broadcasted_iota(jnp.int32, (UR, LANE), 0) & (C - 1)
    u_c = lax.broadcasted_iota(jnp.int32, (UR, LANE), 1)
    is_b = u_c >= C
    u_j = u_c & (C - 1)
    dist = jnp.where(is_b, u_j - u_r, u_r - u_j)
    incl_u = dist >= 0
    strict_u = dist > 0
    rowsum_u = dist <= 0
    blk = (jnp.right_shift(u_r, DN_BLK_SHIFT) == jnp.right_shift(u_j, DN_BLK_SHIFT))[0:C, :]
    bd_r = lax.broadcasted_iota(jnp.int32, (UR, UR), 0)
    bd_c = lax.broadcasted_iota(jnp.int32, (UR, UR), 1)
    same_chunk = jnp.right_shift(bd_r, 6) == jnp.right_shift(bd_c, 6)
    bd_lower = (same_chunk & (bd_r >= bd_c)).astype(BF16)
    bd_upper = (same_chunk & (bd_r <= bd_c)).astype(BF16)
    bd_ones = same_chunk.astype(BF16)
    half_l = (lax.broadcasted_iota(jnp.int32, (C, LANE), 1) < C)

    def split_hi_mid(x):
        hi = x.astype(BF16)
        mid = (x - hi.astype(F32)).astype(BF16)
        return jnp.concatenate([hi, mid], axis=1)

    def dot01(m01, parts):
        r = _dot(m01, parts)
        return r[:, 0:LANE] + r[:, LANE:2 * LANE]

    def block_diag(x):
        xb = x.astype(BF16)
        zb = jnp.zeros_like(xb)
        return jnp.concatenate([jnp.where(half_l, xb, zb), jnp.where(half_l, zb, xb)], axis=0)

    def mm2s(xs, ys):
        return [_dot(x.astype(BF16), block_diag(y)) for x, y in zip(xs, ys)]

    def tri_inv_minus_eye(lms):
        md = [jnp.where(blk, -lm, 0.0) for lm in lms]
        md2 = mm2s(md, md)
        md4 = mm2s(md2, md2)
        md3 = mm2s(md, md2)
        md8 = mm2s(md4, md4)
        px = [a + b + c for a, b, c in zip(md, md2, md3)]
        md12 = mm2s(md4, md8)
        qx = [a + b + c for a, b, c in zip(md4, md8, md12)]
        pq = mm2s(px, qx)
        dx = [a + b + c for a, b, c in zip(px, qx, pq)]
        loff = [jnp.where(blk, 0.0, lm) for lm in lms]
        dl = mm2s(dx, loff)
        n1 = [a + b for a, b in zip(loff, dl)]
        n2 = mm2s(n1, n1)
        n3 = mm2s(n1, n2)
        rx = [b - a - c for a, b, c in zip(n1, n2, n3)]
        rd = mm2s(rx, dx)
        return [a + b + c for a, b, c in zip(rx, dx, rd)]

    lane_u = u_c

    def pick(x, idx):
        return jnp.broadcast_to(jnp.sum(jnp.where(lane_u == idx, x, 0.0), axis=1, keepdims=True), x.shape)

    zeros_r = jnp.zeros((C, 2 * LANE), BF16)

    def local_group(g, carry):
        units = [(hh, g * NU + s) for hh in range(HB) for s in range(NU)]
        U = []
        for hh, un in units:
            rows = pl.ds(_aligned(un * UR, UR), UR)
            bl = bl_ref[0, rows, :]
            hg = head0 + hh
            U.append(dict(hh=hh, un=un, rows=rows,
                          bb_f=pick(bl, hg), bb_b=pick(bl, DN_HEADS + hg),
                          la_f=pick(bl, 2 * DN_HEADS + hg), la_b=pick(bl, 3 * DN_HEADS + hg)))
        for u in U:
            u["sf"] = split_hi_mid(u["la_f"])
            u["sb"] = split_hi_mid(u["la_b"])
            u["sr"] = split_hi_mid(jnp.where(rowsum_u, jnp.where(is_b, u["la_b"], u["la_f"]), 0.0))
        for u in U:
            u["gc_f"] = dot01(bd_lower, u["sf"])
        for u in U:
            u["gc_b"] = dot01(bd_upper, u["sb"])
        for u in U:
            u["g_row"] = dot01(bd_ones, u["sr"])
        for u in U:
            hh, rows = u["hh"], u["rows"]
            u["k"] = kn_ref[hh, rows, :]
            u["q"] = qn_ref[hh, rows, :]
            u["k16"] = u["k"].astype(BF16)
            u["q16"] = u["q"].astype(BF16)
        chunks = [(u, ci) for u in U for ci in range(DN_UNIT)]
        crs = [slice(ci * C, (ci + 1) * C) for _, ci in chunks]
        grams = [_dot_nt(jnp.concatenate([u["k16"][cr], u["q16"][cr]], axis=0),
                         jnp.concatenate([u["k16"][cr], u["k16"][cr]], axis=0))
                 for (u, _), cr in zip(chunks, crs)]
        for u in U:
            u["decay"] = jnp.exp(jnp.where(incl_u, jnp.where(is_b, u["gc_b"], u["gc_f"]) - u["g_row"], NEG_INF))
            u["bb_p"] = jnp.where(is_b, u["bb_b"], u["bb_f"])
        lms = [jnp.where(strict_u[cr], gm[0:C] * u["bb_p"][cr] * u["decay"][cr], 0.0)
               for (u, _), cr, gm in zip(chunks, crs, grams)]
        attn = [(gm[C:2 * C] * u["decay"][cr]).astype(BF16) for (u, _), cr, gm in zip(chunks, crs, grams)]
        txs = tri_inv_minus_eye(lms)
        for u in U:
            v = vn_ref[u["hh"], u["rows"], :]
            eg_f = jnp.exp(u["gc_f"])
            eg_b = jnp.exp(u["gc_b"])
            kb_f = u["k"] * u["bb_f"]
            kb_b = u["k"] * u["bb_b"]
            u["rhs_f"] = jnp.concatenate([v * u["bb_f"], kb_f * eg_f], axis=1)
            u["rhs_b"] = jnp.concatenate([v * u["bb_b"], kb_b * eg_b], axis=1)
            u["r16_f"] = u["rhs_f"].astype(BF16)
            u["r16_b"] = u["rhs_b"].astype(BF16)
            u["qg_f"] = u["q"] * eg_f
            u["qg_b"] = u["q"] * eg_b
        uws = [_dot(tx.astype(BF16), jnp.concatenate(
                    [jnp.concatenate([u["r16_f"][cr], zeros_r], axis=1),
                     jnp.concatenate([zeros_r, u["r16_b"][cr]], axis=1)], axis=0))
               for (u, _), cr, tx in zip(chunks, crs, txs)]
        wu_f = [(u["rhs_f"][cr] + uw[:, 0:2 * LANE]) for (u, _), cr, uw in zip(chunks, crs, uws)]
        wu_b = [(u["rhs_b"][cr] + uw[:, 2 * LANE:4 * LANE]) for (u, _), cr, uw in zip(chunks, crs, uws)]
        wu16_f = [x.astype(BF16) for x in wu_f]
        wu16_b = [x.astype(BF16) for x in wu_b]
        aws = [_dot(a, jnp.concatenate([jnp.concatenate([xf, zeros_r], axis=1),
                                        jnp.concatenate([zeros_r, xb], axis=1)], axis=0))
               for a, xf, xb in zip(attn, wu16_f, wu16_b)]
        kg_f, kg_b, gls = [], [], []
        for (u, ci), cr in zip(chunks, crs):
            gl_f = u["gc_f"][ci * C + C - 1:ci * C + C, :]
            gl_b = u["gc_b"][ci * C:ci * C + 1, :]
            kg_f.append((u["k"][cr] * jnp.exp(gl_f - u["gc_f"][cr])).astype(BF16))
            kg_b.append((u["k"][cr] * jnp.exp(gl_b - u["gc_b"][cr])).astype(BF16))
            gls.append((jnp.exp(gl_f), jnp.exp(gl_b)))
        kwu_f = [_dot_tn(kg, x) for kg, x in zip(kg_f, wu16_f)]
        kwu_b = [_dot_tn(kg, x) for kg, x in zip(kg_b, wu16_b)]
        for idx, ((u, ci), cr) in enumerate(zip(chunks, crs)):
            hh = u["hh"]
            c = u["un"] * DN_UNIT + ci
            crow = pl.ds(_aligned(c * C, C), C)
            aw = aws[idx]
            for d, kwu, qg, a0 in ((0, kwu_f[idx], u["qg_f"], 0), (1, kwu_b[idx], u["qg_b"], 2 * LANE)):
                aq_ref[hh, d, c, 0:DN_DK, :] = kwu[:, LANE:2 * LANE].astype(BF16)
                aq_ref[hh, d, c, DN_DK:DN_DK + C, :] = (qg[cr] - aw[:, a0 + LANE:a0 + 2 * LANE]).astype(BF16)
                bm_ref[hh, d, c] = kwu[:, 0:LANE]
                oacc_ref[hh, d, crow, :] = aw[:, a0:a0 + LANE]
                gl_ref[hh, d, c] = jnp.broadcast_to(gls[idx][d], (8, LANE))
        return carry

    n_groups = T // (NU * UR)
    if n_groups == 1:
        local_group(0, 0)
    else:
        lax.fori_loop(0, n_groups, local_group, 0)

    def body(i, carry):
        cidx = (i, NC - 1 - i)
        rs = [_dot(aq_ref[hh, d, cidx[d]], carry[2 * hh + d].astype(BF16)) for hh in range(HB) for d in range(2)]
        new = []
        for hh in range(HB):
            for d in range(2):
                r = rs[2 * hh + d]
                c = cidx[d]
                rows = pl.ds(pl.multiple_of(c * C, C), C)
                oacc_ref[hh, d, rows, :] = oacc_ref[hh, d, rows, :] + r[DN_DK:DN_DK + C]
                new.append(carry[2 * hh + d] * gl_ref[hh, d, c, 0:1, :] + bm_ref[hh, d, c] - r[0:DN_DK])
        return tuple(new)

    if has_s0:
        init = tuple(s0_ref[0, 0, d, hh] for hh in range(HB) for d in range(2))
    else:
        init = tuple(jnp.zeros((DN_DK, LANE), F32) for _ in range(2 * HB))
    fin = lax.fori_loop(0, NC, body, init, unroll=min(NC, DN_SEQ_UNROLL))
    if want_state:
        for hh in range(HB):
            for d in range(2):
                sfin_ref[0, 0, d, hh] = fin[2 * hh + d]

    for hh in range(HB):
        cols = slice(hh * LANE, (hh + 1) * LANE)
        for r0 in range(0, T, RB):
            o = oacc_ref[hh, 0, r0:r0 + RB, :] + oacc_ref[hh, 1, r0:r0 + RB, :]
            o = o * lax.rsqrt(jnp.mean(o * o, axis=-1, keepdims=True) + NORM_EPS) * gn_ref[0]
            y_ref[0, r0:r0 + RB, cols] = (o * _silu(z_ref[0, r0:r0 + RB, cols].astype(F32))).astype(BF16)


def _delta_mixer(proj3, bl3, l, conv_w, g_norm, s0, s_prev, HB, NU):
    B, T, _ = proj3.shape
    has_s0 = s0 is not None
    want_state = s_prev is not None
    NC = T // DN_CHUNK
    W = HB * LANE
    kern = functools.partial(_delta_kernel, T=T, HB=HB, NU=NU, has_s0=has_s0, want_state=want_state)
    col = lambda cb: pl.BlockSpec((1, T, W), lambda b, h, cb=cb: (b, 0, cb // HB + h))
    cw = lambda cb: pl.BlockSpec((1, 3, W), lambda b, h, cb=cb: (l, 0, cb // HB + h))
    in_specs = [col(CB_DN_Q), col(CB_DN_K), col(CB_DN_V), col(CB_DN_Z),
                pl.BlockSpec((1, T, LANE), lambda b, h: (b, 0, 0)),
                cw(0), cw(4), cw(8), pl.BlockSpec((1, 1, LANE), lambda b, h: (l, 0, 0))]
    args = [proj3, proj3, proj3, proj3, bl3, conv_w, conv_w, conv_w, g_norm]
    if has_s0:
        in_specs.append(pl.BlockSpec((1, 1, 2, HB, DN_DK, LANE), lambda b, h: (b, l, 0, h, 0, 0)))
        args.append(s0)
    aliases = {}
    if want_state:
        aliases = {len(args): 1}
        in_specs.append(pl.BlockSpec(memory_space=pl.ANY))
        args.append(s_prev)
    out_specs = [pl.BlockSpec((1, T, W), lambda b, h: (b, 0, h))]
    out_shape = [jax.ShapeDtypeStruct((B, T, DN_WIDTH), BF16)]
    if want_state:
        out_specs.append(pl.BlockSpec((1, 1, 2, HB, DN_DK, LANE), lambda b, h: (b, l, 0, h, 0, 0)))
        out_shape.append(jax.ShapeDtypeStruct(s_prev.shape, s_prev.dtype))
    res = pl.pallas_call(
        kern,
        grid=(B, DN_HEADS // HB),
        in_specs=in_specs,
        out_specs=out_specs,
        out_shape=out_shape,
        scratch_shapes=[
            pltpu.VMEM((T + 16, LANE), F32),
            pltpu.VMEM((HB, T, LANE), F32), pltpu.VMEM((HB, T, LANE), F32), pltpu.VMEM((HB, T, LANE), F32),
            pltpu.VMEM((HB, 2, NC, DN_DK + DN_CHUNK, LANE), BF16),
            pltpu.VMEM((HB, 2, NC, DN_DK, LANE), F32),
            pltpu.VMEM((HB, 2, T, LANE), F32),
            pltpu.VMEM((HB, 2, NC, 8, LANE), F32),
        ],
        input_output_aliases=aliases,
        compiler_params=_params(("parallel", "parallel")),
        name="delta_mixer",
    )(*args)
    return (res[0], res[1]) if want_state else (res[0], None)


def _head_masks(rows, dtype, value=1.0):
    lane = lax.broadcasted_iota(jnp.int32, (rows, LANE), 1)
    return (jnp.where(lane < NA_HD, value, 0.0).astype(dtype), jnp.where(lane >= NA_HD, value, 0.0).astype(dtype))


NA_SCALE = NA_HD ** -0.5


def _ctx_attn_body(q_ref, k_ref, v_ref, z_ref, y_ref, T):
    m0, m1 = _head_masks(T, BF16, NA_SCALE)
    f0, f1 = _head_masks(T, F32)
    cols = [slice(p * LANE, (p + 1) * LANE) for p in range(NA_WIDTH // LANE)]
    pairs = [(nb, c) for nb in range(q_ref.shape[0]) for c in cols]
    s = [_dot_nt(q_ref[nb, :, c] * hm, k_ref[nb, :, c]) for nb, c in pairs for hm in (m0, m1)]
    e = [jnp.exp(x - jnp.max(x, axis=-1, keepdims=True)) for x in s]
    o = [_dot(e[2 * i + hh].astype(BF16), v_ref[nb, :, c]) for i, (nb, c) in enumerate(pairs) for hh in range(2)]
    o = [x / jnp.sum(ee, axis=-1, keepdims=True) for x, ee in zip(o, e)]
    for i, (nb, c) in enumerate(pairs):
        out = o[2 * i] * f0 + o[2 * i + 1] * f1
        y_ref[nb, :, c] = (out * _silu(z_ref[nb, :, c].astype(F32))).astype(BF16)


def _na_kernel(q_ref, k_ref, v_ref, z_ref, kc_ref, vc_ref, bias_ref, y_ref, sctx_ref, *, T, L):
    W = GRID_W
    rows_total = T // W
    n_win = NA_WIN_R * W
    RS = NA_ROWS_PER_STEP
    m0, m1 = _head_masks(W, BF16, NA_SCALE)
    f0, f1 = _head_masks(W, F32)
    kcb = kc_ref[0, 0].astype(BF16)
    vcb = vc_ref[0, 0].astype(BF16)

    RB = 256
    mb0, mb1 = _head_masks(RB, BF16, NA_SCALE)
    for r0 in range(0, T, RB):
        q = q_ref[0, r0:r0 + RB, :]
        sctx_ref[0, r0:r0 + RB, :] = _dot_nt(q * mb0, kcb)
        sctx_ref[1, r0:r0 + RB, :] = _dot_nt(q * mb1, kcb)

    def step(i, carry):
        rr = [i * RS + j for j in range(RS)]
        rs_ = [jnp.clip(r - NA_WIN_R // 2, 0, rows_total - NA_WIN_R) for r in rr]
        qrows = [pl.ds(pl.multiple_of(r * W, W), W) for r in rr]
        krows = [pl.ds(pl.multiple_of(rs * W, W), n_win) for rs in rs_]
        qs = []
        for qr in qrows:
            q = q_ref[0, qr, :]
            qs.append(jnp.concatenate([q * m0, q * m1], axis=0))
        s = [_dot_nt(x, k_ref[0, kr, :]) for x, kr in zip(qs, krows)]
        e1, e2, den = [], [], []
        for j in range(RS):
            dr0 = rs_[j] - rr[j] + NA_WIN_R - 1
            bias = jnp.concatenate(
                [jnp.concatenate([bias_ref[0, hh, dr0 + 2 * m] for m in range(NA_WIN_R // 2)], axis=1)
                 for hh in range(2)], axis=0)
            sl = s[j] + bias
            sc = jnp.concatenate([sctx_ref[0, qrows[j], :], sctx_ref[1, qrows[j], :]], axis=0)
            mx = jnp.maximum(jnp.max(sl, axis=-1, keepdims=True), jnp.max(sc, axis=-1, keepdims=True))
            a = jnp.exp(sl - mx)
            b = jnp.exp(sc - mx)
            e1.append(a.astype(BF16))
            e2.append(b.astype(BF16))
            den.append(jnp.sum(a, axis=-1, keepdims=True) + jnp.sum(b, axis=-1, keepdims=True))
        o = [_dot(a, v_ref[0, kr, :]) + _dot(b, vcb) for a, b, kr in zip(e1, e2, krows)]
        for j in range(RS):
            oj = o[j] / den[j]
            out = oj[0:W, :] * f0 + oj[W:2 * W, :] * f1
            y_ref[0, qrows[j], :] = (out * _silu(z_ref[0, qrows[j], :].astype(F32))).astype(BF16)
        return carry

    lax.fori_loop(0, rows_total // RS, step, 0)


def _neighbourhood_attention(proj3, l, k_ctx, v_ctx, bias2):
    B, T, _ = proj3.shape
    L = k_ctx.shape[2]
    col = lambda cb: pl.BlockSpec((1, T, LANE), lambda b, p, cb=cb: (b, 0, cb + p))
    ctx = pl.BlockSpec((1, 1, L, LANE), lambda b, p: (b, l, 0, p))
    return pl.pallas_call(
        functools.partial(_na_kernel, T=T, L=L),
        grid=(B, NA_WIDTH // LANE),
        in_specs=[col(CB_NA_Q), col(CB_NA_K), col(CB_NA_V), col(CB_NA_Z), ctx, ctx,
                  pl.BlockSpec((1, 2, 2 * NA_WIN_R - 2, GRID_W, LANE), lambda b, p: (l, p, 0, 0, 0))],
        out_specs=pl.BlockSpec((1, T, LANE), lambda b, p: (b, 0, p)),
        out_shape=jax.ShapeDtypeStruct((B, T, NA_WIDTH), BF16),
        scratch_shapes=[pltpu.VMEM((2, T, L), F32)],
        compiler_params=_params(("parallel", "parallel")),
        name="nbr_attention",
    )(proj3, proj3, proj3, proj3, k_ctx, v_ctx, bias2)


def _pool_body(u_ref, z_ref, band_ref, pw_ref, ps_ref, y_ref, x_buf, nb, T):
    P = POOL_HALO
    RB = POOL_ROWS
    zeros = jnp.zeros((P, POOL_WIDTH), BF16)
    x_buf[nb, 0:P, :] = zeros
    x_buf[nb, T + P:T + 2 * P, :] = zeros
    for r0 in range(0, T, RB):
        x_buf[nb, P + r0:P + r0 + RB, :] = u_ref[nb, r0:r0 + RB, :]
    groups = list(enumerate(POOL_WINDOWS))
    cols = [slice(g * POOL_GC, (g + 1) * POOL_GC) for g, _ in groups]
    for r0 in range(0, T, RB):
        t = r0 + lax.broadcasted_iota(jnp.int32, (RB, 1), 0)
        tots = [_dot(band_ref[g], x_buf[nb, r0:r0 + RB + 2 * P, cols[g]]) for g, _ in groups]
        pooled = []
        for g, win in groups:
            lo = jnp.maximum(t - win // 2, 0)
            hi = jnp.minimum(t + win // 2 - 1, T - 1)
            inv_cnt = 1.0 / (hi - lo + 1).astype(F32)
            pooled.append((tots[g] * inv_cnt - u_ref[nb, r0:r0 + RB, cols[g]].astype(F32)).astype(BF16))
        ys = [_dot(pooled[g], pw_ref[0, g]) for g, _ in groups]
        for g, _ in groups:
            y = ys[g] * ps_ref[0, :, cols[g]]
            y_ref[nb, r0:r0 + RB, cols[g]] = (y * _silu(z_ref[nb, r0:r0 + RB, cols[g]].astype(F32))).astype(BF16)


def _pool_kernel(u_ref, z_ref, band_ref, pw_ref, ps_ref, y_ref, x_buf, *, T):
    _pool_body(u_ref, z_ref, band_ref, pw_ref, ps_ref, y_ref, x_buf, 0, T)


def _ctx_mixers_kernel(q_ref, k_ref, v_ref, z_ref, u_ref, zp_ref, band_ref, pw_ref, ps_ref, yna_ref, ypl_ref,
                       x_buf, *, T):
    _ctx_attn_body(q_ref, k_ref, v_ref, z_ref, yna_ref, T)
    for nb in range(u_ref.shape[0]):
        _pool_body(u_ref, zp_ref, band_ref, pw_ref, ps_ref, ypl_ref, x_buf, nb, T)


def _pool_bands():
    off = np.arange(POOL_ROWS + 2 * POOL_HALO)[None, :] - POOL_HALO - np.arange(POOL_ROWS)[:, None]
    return jnp.asarray(np.stack([(off >= -(w // 2)) & (off <= w // 2 - 1) for w in POOL_WINDOWS]), BF16)


def _pool_mixer(proj3, l, pool_w, pool_scale):
    B, T, _ = proj3.shape
    wide = POOL_WIDTH // LANE
    bands = _pool_bands()
    return pl.pallas_call(
        functools.partial(_pool_kernel, T=T),
        grid=(B,),
        in_specs=[
            pl.BlockSpec((1, T, POOL_WIDTH), lambda b: (b, 0, CB_PL_U // wide)),
            pl.BlockSpec((1, T, POOL_WIDTH), lambda b: (b, 0, CB_PL_Z // wide)),
            pl.BlockSpec(bands.shape, lambda b: (0, 0, 0)),
            pl.BlockSpec((1, len(POOL_WINDOWS), POOL_GC, POOL_GC), lambda b: (l, 0, 0, 0)),
            pl.BlockSpec((1, 1, POOL_WIDTH), lambda b: (l, 0, 0)),
        ],
        out_specs=pl.BlockSpec((1, T, POOL_WIDTH), lambda b: (b, 0, 0)),
        out_shape=jax.ShapeDtypeStruct((B, T, POOL_WIDTH), BF16),
        scratch_shapes=[pltpu.VMEM((1, T + 2 * POOL_HALO, POOL_WIDTH), BF16)],
        compiler_params=_params(("parallel",)),
        name="pool_mixer",
    )(proj3, proj3, bands, pool_w, pool_scale)


def _ctx_mixers(proj3, l, pool_w, pool_scale, nb=4):
    B, T, _ = proj3.shape
    wide = NA_WIDTH // LANE
    bands = _pool_bands()
    spec = lambda cb: pl.BlockSpec((nb, T, NA_WIDTH), lambda b, cb=cb: (b, 0, cb // wide))
    out = pl.BlockSpec((nb, T, NA_WIDTH), lambda b: (b, 0, 0))
    return pl.pallas_call(
        functools.partial(_ctx_mixers_kernel, T=T),
        grid=(B // nb,),
        in_specs=[spec(CB_NA_Q), spec(CB_NA_K), spec(CB_NA_V), spec(CB_NA_Z), spec(CB_PL_U), spec(CB_PL_Z),
                  pl.BlockSpec(bands.shape, lambda b: (0, 0, 0)),
                  pl.BlockSpec((1, len(POOL_WINDOWS), POOL_GC, POOL_GC), lambda b: (l, 0, 0, 0)),
                  pl.BlockSpec((1, 1, POOL_WIDTH), lambda b: (l, 0, 0))],
        out_specs=[out, out],
        out_shape=[jax.ShapeDtypeStruct((B, T, NA_WIDTH), BF16), jax.ShapeDtypeStruct((B, T, POOL_WIDTH), BF16)],
        scratch_shapes=[pltpu.VMEM((nb, T + 2 * POOL_HALO, POOL_WIDTH), BF16)],
        compiler_params=_params(("parallel",)),
        name="ctx_mixers",
    )(proj3, proj3, proj3, proj3, proj3, proj3, bands, pool_w, pool_scale)


def _merge_kernel(ydn_ref, yna_ref, ypl_ref, gdn_ref, gna_ref, gpl_ref, x_ref, mod_ref, gpost_ref,
                  wd_ref, wn_ref, wp_ref, wo_ref, o_ref, *, tiles_per_batch, mod_row0):
    merged = (_sigmoid(gdn_ref[...].astype(F32)) * _dot(ydn_ref[...], wd_ref[0])
              + _sigmoid(gna_ref[...].astype(F32)) * _dot(yna_ref[...], wn_ref[0])
              + _sigmoid(gpl_ref[...].astype(F32)) * _dot(ypl_ref[...], wp_ref[0]))
    out = _dot(merged.astype(BF16), wo_ref[0])
    out = out * lax.rsqrt(jnp.mean(out * out, axis=-1, keepdims=True) + NORM_EPS) * gpost_ref[0]
    gate = _mod_row(mod_ref, tiles_per_batch, mod_row0)[:, 2 * D_MODEL:3 * D_MODEL]
    o_ref[...] = x_ref[...] + gate * out


def _merge(y_dn, y_na, y_pl, proj2d, x2d, l, mod, g_post, w_dn, w_na, w_pl, w_out, tiles_per_batch, mod_row0, tm):
    ntok = x2d.shape[0]
    g0 = CB_GATE * LANE // D_MODEL
    br = pl.BlockSpec((tm, DN_WIDTH), lambda m: (m, 0))
    gate = lambda j: pl.BlockSpec((tm, D_MODEL), lambda m, j=j: (m, g0 + j))
    wbr = pl.BlockSpec((1, DN_WIDTH, D_MODEL), lambda m: (l, 0, 0))
    return pl.pallas_call(
        functools.partial(_merge_kernel, tiles_per_batch=tiles_per_batch, mod_row0=mod_row0),
        grid=(ntok // tm,),
        in_specs=[br, br, br, gate(0), gate(1), gate(2),
                  pl.BlockSpec((tm, D_MODEL), lambda m: (m, 0)),
                  pl.BlockSpec((1, 8, 3 * D_MODEL), lambda m: (l, 0, 0)),
                  pl.BlockSpec((1, 1, D_MODEL), lambda m: (l, 0, 0)),
                  wbr, wbr, wbr,
                  pl.BlockSpec((1, D_MODEL, D_MODEL), lambda m: (l, 0, 0))],
        out_specs=pl.BlockSpec((tm, D_MODEL), lambda m: (m, 0)),
        out_shape=jax.ShapeDtypeStruct((ntok, D_MODEL), F32),
        compiler_params=_params(("parallel",)),
        name="merge_out",
    )(y_dn, y_na, y_pl, proj2d, proj2d, proj2d, x2d, mod, g_post, w_dn, w_na, w_pl, w_out)


def _layer(x3, l, mod_row0, s0, k_ctx, v_ctx, pw, tm_proj, tm_merge, delta_cfg, carry=None):
    B, T, _ = x3.shape
    n = B * T
    x2d = x3.reshape(n, D_MODEL)
    is_ctx = k_ctx is None
    res = _project(x2d, l, pw["mod"], pw["g_pre"], pw["w_main"], pw["w_ba"], pw["alog_v"], pw["dtb_v"],
                   None if is_ctx else T // tm_proj, mod_row0,
                   kv=carry[:2] if is_ctx else None, tm=tm_proj)
    proj2d, bl2d = res[0], res[1]
    proj3 = proj2d.reshape(B, T, P_MAIN)
    y_dn, s_fin = _delta_mixer(proj3, bl2d.reshape(B, T, LANE), l, pw["conv_w"], pw["g_norm"], s0,
                               s_prev=carry[2] if is_ctx else None,
                               HB=delta_cfg[0], NU=delta_cfg[1])
    if is_ctx:
        y_na, y_pl = _ctx_mixers(proj3, l, pw["pool_w"], pw["pool_scale"])
    else:
        y_na = _neighbourhood_attention(proj3, l, k_ctx, v_ctx, pw["bias2"])
        y_pl = _pool_mixer(proj3, l, pw["pool_w"], pw["pool_scale"])
    out2d = _merge(y_dn.reshape(n, DN_WIDTH), y_na.reshape(n, NA_WIDTH), y_pl.reshape(n, POOL_WIDTH), proj2d,
                   x2d, l, pw["mod"], pw["g_post"], pw["w_dn"], pw["w_na"], pw["w_pl"], pw["w_out"],
                   None if is_ctx else T // tm_merge, mod_row0, tm=tm_merge)
    return out2d.reshape(B, T, D_MODEL), (tuple(res[2:]) + (s_fin,) if is_ctx else None)


def _lane_vecs(v, offset):
    return jnp.zeros((DEPTH, 1, LANE), F32).at[:, 0, offset:offset + 2 * DN_HEADS].set(v.reshape(DEPTH, -1))


def _bias_windows(na_bias):
    cols = np.arange(GRID_W)
    offset = cols[None, :] - cols[:, None] + NA_WIN_C - 1
    onehot = (offset[None] == np.arange(2 * NA_WIN_C - 1)[:, None, None]).astype(np.float32)
    bias_t = jnp.einsum("lhrd,dqk->lhrqk", na_bias, onehot, precision=lax.Precision.HIGHEST)
    cs = np.clip(cols - NA_WIN_C // 2, 0, GRID_W - NA_WIN_C)[:, None]
    col_ok = (cols[None, :] >= cs) & (cols[None, :] < cs + NA_WIN_C)
    bias_t = jnp.where(col_ok, bias_t, NEG_INF)
    return jnp.concatenate([bias_t[:, :, :-1], bias_t[:, :, 1:]], axis=-1)


def kernel(x_prompt, x_sample, c, cache_k_na, cache_v_na, state_dn, c_ctx, w_ada, b_ada, g_pre, g_post, w_in,
           conv_dn, a_log_dn, dt_bias_dn, g_norm_dn, na_bias, pool_w, pool_scale, w_br_dn, w_br_na, w_br_pl,
           w_out):
    B, T, _ = x_prompt.shape
    DB, DT, _ = x_sample.shape
    L = cache_k_na.shape[2]

    cc = jnp.zeros((8, D_MODEL), F32).at[0].set(c_ctx).at[1:1 + DB].set(c)
    w_main, w_ba = _regroup_weights(w_in)
    pw = dict(
        mod=_modulation(cc, w_ada, b_ada),
        g_pre=g_pre[:, None], g_post=g_post[:, None], g_norm=g_norm_dn[:, None], pool_scale=pool_scale[:, None],
        w_main=w_main, w_ba=w_ba,
        alog_v=_lane_vecs(a_log_dn, 2 * DN_HEADS), dtb_v=_lane_vecs(dt_bias_dn, 2 * DN_HEADS),
        conv_w=conv_dn, bias2=_bias_windows(na_bias), pool_w=pool_w.astype(BF16),
        w_dn=w_br_dn.astype(BF16), w_na=w_br_na.astype(BF16), w_pl=w_br_pl.astype(BF16), w_out=w_out.astype(BF16))
    k_ctx = cache_k_na.reshape(DB, DEPTH, L, NA_WIDTH)
    v_ctx = cache_v_na.reshape(DB, DEPTH, L, NA_WIDTH)

    xp, xs = x_prompt, x_sample
    carry = (jnp.zeros((B, DEPTH, T, NA_WIDTH), F32), jnp.zeros((B, DEPTH, T, NA_WIDTH), F32),
             jnp.zeros((B, DEPTH, 2, DN_HEADS, DN_DK, LANE), F32))
    for l in range(DEPTH):
        xp, carry = _layer(xp, l, 0, None, None, None, pw, tm_proj=1024, tm_merge=512, delta_cfg=(4, 1),
                           carry=carry)
        xs, _ = _layer(xs, l, 1, state_dn, k_ctx, v_ctx, pw, tm_proj=1024, tm_merge=512, delta_cfg=(2, 2))
    k_all, v_all, s_all = carry
    return (xp, xs, k_all.reshape(B, DEPTH, T, NA_HEADS, NA_HD), v_all.reshape(B, DEPTH, T, NA_HEADS, NA_HD), s_all)
```
